```python
import math
import jax
import jax.numpy as jnp
from jax import lax
import numpy as np

D_MODEL = 4096
BATCH = 1
SEQ = 16384
DEPTH = 2

GRID_W = 64
CTX_LEN = 256
HEAD_DIM = 128
D_MIX = D_MODEL
A_HEADS = 12
A_WIDTH = A_HEADS * HEAD_DIM
A_COLS = 5 * A_WIDTH
SCAN_CHUNK = 64
B_HEADS = 5
B_WIDTH = B_HEADS * 2 * HEAD_DIM
B_COLS = 3 * B_WIDTH
Q_BLOCK = 128
C_HEADS = 10
C_WIDTH = C_HEADS * HEAD_DIM
C_COLS = 3 * C_WIDTH
NA_KH = 8
NA_KW = 16
IN_COLS = A_COLS + B_COLS + C_COLS
ROPE_BASE = 10000.0
ROPE_AXIS_DIM = HEAD_DIM // 2
N_EXPERTS = 32
TOP_K = 4
D_EXPERT = 512
SWIGLU_ALPHA = 1.702
SWIGLU_LIMIT = 7.0
N_MOD = 6
NORM_EPS = 1e-6
F_TINY = 1e-30

kernel_name = 'hybrid_hgrn2_diffattn_natten_moe_dit'


def rmsnorm(x, g):
    xf = x.astype(jnp.float32)
    y = xf * lax.rsqrt(jnp.mean(xf * xf, axis=-1, keepdims=True) + NORM_EPS)
    return y.astype(x.dtype) * g


def grid_angles(n_tok):
    t = jnp.arange(n_tok, dtype=jnp.int32)
    half = ROPE_AXIS_DIM // 2
    inv_freq = ROPE_BASE ** (-jnp.arange(half, dtype=jnp.float32) / half)
    row = (t // GRID_W).astype(jnp.float32)
    col = (t % GRID_W).astype(jnp.float32)
    return row[:, None] * inv_freq, col[:, None] * inv_freq


def rope_axis(x, ang):
    x1, x2 = jnp.split(x, 2, axis=-1)
    cos = jnp.cos(ang).astype(x.dtype)
    sin = jnp.sin(ang).astype(x.dtype)
    return jnp.concatenate([x1 * cos - x2 * sin, x2 * cos + x1 * sin], axis=-1)


def rope_2d(x, ang_row, ang_col):
    return jnp.concatenate([rope_axis(x[..., :ROPE_AXIS_DIM], ang_row),
                            rope_axis(x[..., ROPE_AXIS_DIM:], ang_col)], axis=-1)


def hgrn_heads(t):
    b, n, _ = t.shape
    return t.reshape(b, n, A_HEADS, HEAD_DIM).transpose(0, 2, 1, 3).astype(jnp.float32)


def hgrn_gates(f_pre, lb):
    f = lb + (1.0 - lb) * jax.nn.sigmoid(f_pre)
    log_f = jnp.log(jnp.maximum(f, F_TINY))
    k = (1.0 - lb) * jax.nn.sigmoid(-f_pre)
    return k, log_f


def chunk_scan(q, k, v, log_f, s0):
    b, h, n, dk = q.shape
    dv = v.shape[-1]
    nc = n // SCAN_CHUNK

    def to_chunks(t):
        return jnp.moveaxis(t.reshape(b, h, nc, SCAN_CHUNK, t.shape[-1]), 2, 0)

    tri = jnp.tril(jnp.ones((SCAN_CHUNK, SCAN_CHUNK), dtype=bool))[:, :, None]

    def step(state, inp):
        qc, kc, vc, gc = inp
        cum = jnp.cumsum(gc, axis=-2)
        rel = jnp.where(tri, cum[..., :, None, :] - cum[..., None, :, :], 0.0)
        decay = jnp.where(tri, jnp.exp(rel), 0.0)
        attn = jnp.einsum('bhtd,bhsd,bhtsd->bhts', qc, kc, decay)
        o = jnp.einsum('bhts,bhsv->bhtv', attn, vc) + jnp.einsum('bhtd,bhdv->bhtv', qc * jnp.exp(cum), state)
        cum_end = cum[..., -1, :]
        state = jnp.exp(cum_end)[..., None] * state + jnp.einsum(
            'bhsd,bhsv->bhdv', kc * jnp.exp(cum_end[..., None, :] - cum), vc)
        return state, o

    state, o = lax.scan(step, s0, (to_chunks(q), to_chunks(k), to_chunks(v), to_chunks(log_f)))
    return state, jnp.moveaxis(o, 0, 2).reshape(b, h, n, dv)


def hgrn2_mixer(pa_x, pa_c, lb, norm_g, need_ctx):
    def prep(p):
        q, f_fw, f_bw, i, g = jnp.split(p, 5, axis=-1)
        q = hgrn_heads(jax.nn.silu(q)) * (HEAD_DIM ** -0.5)
        return q, (hgrn_heads(f_fw), hgrn_heads(f_bw)), hgrn_heads(i), g

    qx, fx, vx, gx = prep(pa_x)
    qc, fc, vc, gc = prep(pa_c)
    b = qx.shape[0]
    outs_x, outs_c = [], []
    for d in range(2):
        lbd = lb[d].astype(jnp.float32).reshape(A_HEADS, 1, HEAD_DIM)
        kx, lfx = hgrn_gates(fx[d], lbd)
        kc, lfc = hgrn_gates(fc[d], lbd)
        seq_x = [qx, kx, vx, lfx]
        seq_c = [qc, kc, vc, lfc]
        if d == 1:
            seq_x = [jnp.flip(t, axis=2) for t in seq_x]
            seq_c = [jnp.flip(t, axis=2) for t in seq_c]
        s0 = jnp.zeros((b, A_HEADS, HEAD_DIM, HEAD_DIM), jnp.float32)
        s_ctx, o_c = chunk_scan(*seq_c, s0)
        _, o_x = chunk_scan(*seq_x, s_ctx)
        if d == 1:
            o_x = jnp.flip(o_x, axis=2)
            o_c = jnp.flip(o_c, axis=2)
        outs_x.append(o_x)
        outs_c.append(o_c)

    def finish(o, g):
        o = rmsnorm(o, norm_g)
        bb, _, n, _ = o.shape
        o = o.transpose(0, 2, 1, 3).reshape(bb, n, A_WIDTH)
        return o.astype(g.dtype) * jax.nn.silu(g)

    y_x = finish(outs_x[0] + outs_x[1], gx)
    y_c = finish(outs_c[0] + outs_c[1], gc) if need_ctx else None
    return y_x, y_c


def diff_split(p):
    b, n, _ = p.shape
    q, k, v = jnp.split(p, 3, axis=-1)
    q = q.reshape(b, n, B_HEADS, 2, HEAD_DIM).transpose(0, 2, 3, 1, 4)
    k = k.reshape(b, n, B_HEADS, 2, HEAD_DIM).transpose(0, 2, 3, 1, 4)
    v = v.reshape(b, n, B_HEADS, 2 * HEAD_DIM).transpose(0, 2, 1, 3)
    return q, k, v


def diff_attend(q, k, v, lam):
    s = jnp.einsum('bhmqd,bhmkd->bhmqk', q, k).astype(jnp.float32)
    p = jax.nn.softmax(s, axis=-1)
    p = p[:, :, 0] - lam * p[:, :, 1]
    return jnp.einsum('bhqk,bhkv->bhqv', p.astype(v.dtype), v)


def diff_mixer(pb_x, pb_c, lam_p, subln_g, lambda_init, ang_row, ang_col, need_ctx):
    scale = HEAD_DIM ** -0.5
    qx, kx, vx = diff_split(pb_x)
    qc, kc, vc = diff_split(pb_c)
    qx = rope_2d(qx, ang_row, ang_col) * scale
    kx = rope_2d(kx, ang_row, ang_col)
    lp = lam_p.astype(jnp.float32)
    lam = jnp.exp(jnp.sum(lp[0] * lp[1])) - jnp.exp(jnp.sum(lp[2] * lp[3])) + lambda_init
    k_all = jnp.concatenate([kc, kx], axis=3)
    v_all = jnp.concatenate([vc, vx], axis=2)
    b, h, _, n, d = qx.shape
    nb = n // Q_BLOCK
    q_blocks = jnp.moveaxis(qx.reshape(b, h, 2, nb, Q_BLOCK, d), 3, 0)
    ox = lax.map(lambda qb: diff_attend(qb, k_all, v_all, lam), q_blocks)
    ox = jnp.moveaxis(ox, 0, 2).reshape(b, h, n, 2 * HEAD_DIM)

    def finish(o):
        o = rmsnorm(o, subln_g) * (1.0 - lambda_init)
        bb, _, m, _ = o.shape
        return o.transpose(0, 2, 1, 3).reshape(bb, m, B_WIDTH)

    y_x = finish(ox)
    y_c = finish(diff_attend(qc * scale, kc, vc, lam)) if need_ctx else None
    return y_x, y_c


def na_split(p):
    b, n, _ = p.shape
    return [t.reshape(b, n, C_HEADS, HEAD_DIM).transpose(0, 2, 1, 3) for t in jnp.split(p, 3, axis=-1)]


def na_mixer(pc_x, pc_c, rpb, need_ctx):
    scale = HEAD_DIM ** -0.5
    qx, kx, vx = na_split(pc_x)
    qc, kc, vc = na_split(pc_c)
    b, h, n, d = qx.shape
    rows = n // GRID_W
    kh = min(NA_KH, rows)
    q_rows = jnp.moveaxis((qx * scale).reshape(b, h, rows, GRID_W, d), 2, 0)
    k_grid = kx.reshape(b, h, rows, GRID_W, d)
    v_grid = vx.reshape(b, h, rows, GRID_W, d)
    col = jnp.arange(GRID_W)
    col_idx = jnp.clip(col - NA_KW // 2, 0, GRID_W - NA_KW)[:, None] + jnp.arange(NA_KW)[None, :]
    rpb_col = rpb[:, :, col_idx - col[:, None] + NA_KW - 1]
    n_loc = kh * NA_KW

    def one_row(args):
        r, q_r = args
        rs = jnp.clip(r - kh // 2, 0, rows - kh)
        k_win = lax.dynamic_slice_in_dim(k_grid, rs, kh, axis=2)[:, :, :, col_idx]
        v_win = lax.dynamic_slice_in_dim(v_grid, rs, kh, axis=2)[:, :, :, col_idx]
        bias = rpb_col[:, rs + jnp.arange(kh) - r + NA_KH - 1].transpose(0, 2, 1, 3)
        s_loc = jnp.einsum('bhqd,bhaqcd->bhqac', q_r, k_win).astype(jnp.float32) + bias.astype(jnp.float32)
        s_ctx = jnp.einsum('bhqd,bhnd->bhqn', q_r, kc).astype(jnp.float32)
        p = jax.nn.softmax(jnp.concatenate([s_loc.reshape(b, h, GRID_W, n_loc), s_ctx], axis=-1), axis=-1)
        p = p.astype(v_win.dtype)
        p_loc = p[..., :n_loc].reshape(b, h, GRID_W, kh, NA_KW)
        return (jnp.einsum('bhqac,bhaqcd->bhqd', p_loc, v_win)
                + jnp.einsum('bhqn,bhnd->bhqd', p[..., n_loc:], vc))

    ox = lax.map(one_row, (jnp.arange(rows), q_rows))
    ox = jnp.moveaxis(ox, 0, 2).reshape(b, h, n, d)

    def merge(o):
        return o.transpose(0, 2, 1, 3).reshape(o.shape[0], o.shape[2], C_WIDTH)

    y_x = merge(ox)
    y_c = None
    if need_ctx:
        p_c = jax.nn.softmax(jnp.einsum('bhqd,bhnd->bhqn', qc * scale, kc).astype(jnp.float32), axis=-1)
        y_c = merge(jnp.einsum('bhqn,bhnd->bhqd', p_c.astype(vc.dtype), vc))
    return y_x, y_c


def token_mixer(hx, hc, w_in, w_out, lb, hgrn_g, lam_p, subln_g, rpb, lambda_init, ang_row, ang_col, need_ctx):
    px = hx @ w_in
    pc = hc @ w_in
    a1 = A_COLS
    a2 = A_COLS + B_COLS
    ya_x, ya_c = hgrn2_mixer(px[..., :a1], pc[..., :a1], lb, hgrn_g, need_ctx)
    yb_x, yb_c = diff_mixer(px[..., a1:a2], pc[..., a1:a2], lam_p, subln_g, lambda_init, ang_row, ang_col, need_ctx)
    yc_x, yc_c = na_mixer(px[..., a2:], pc[..., a2:], rpb, need_ctx)
    y_x = jnp.concatenate([ya_x, yb_x, yc_x], axis=-1) @ w_out
    y_c = jnp.concatenate([ya_c, yb_c, yc_c], axis=-1) @ w_out if need_ctx else None
    return y_x, y_c


def moe_ffn(h, router_w, router_b, w_gu, b_gu, w_down, b_down):
    logits = (h @ router_w + router_b).astype(jnp.float32)
    top_val, top_idx = lax.top_k(logits, TOP_K)
    top_w = jax.nn.softmax(top_val, axis=-1)
    gates = jnp.einsum('blk,blke->ble', top_w,
                       jax.nn.one_hot(top_idx, N_EXPERTS, dtype=jnp.float32)).astype(h.dtype)
    out = jnp.zeros_like(h)
    for e in range(N_EXPERTS):
        gu = h @ w_gu[e] + b_gu[e]
        g = jnp.minimum(gu[..., ::2], SWIGLU_LIMIT)
        u = jnp.clip(gu[..., 1::2], -SWIGLU_LIMIT, SWIGLU_LIMIT)
        act = (u + 1.0) * g * jax.nn.sigmoid(g * SWIGLU_ALPHA)
        out = out + gates[..., e:e + 1] * (act @ w_down[e] + b_down[e])
    return out


def setup_inputs(seed: int = 0) -> dict:
    key = jax.random.key(seed)
    ks = jax.random.split(key, 22)
    f32 = jnp.float32
    D = D_MODEL

    def nrm(k, shape, s):
        return jax.random.normal(k, shape, f32) * s

    return {
        'x': nrm(ks[0], (BATCH, SEQ, D), 1.0),
        'c': nrm(ks[1], (BATCH, D), 1.0),
        'ctx': nrm(ks[2], (BATCH, CTX_LEN, D), 1.0),
        'c_ctx': nrm(ks[3], (D,), 1.0),
        'ada_w': nrm(ks[4], (DEPTH, D, N_MOD * D), D ** -0.5),
        'ada_b': nrm(ks[5], (DEPTH, N_MOD * D), 0.02),
        'norm_mix': 1.0 + nrm(ks[6], (DEPTH, D), 0.02),
        'norm_ffn': 1.0 + nrm(ks[7], (DEPTH, D), 0.02),
        'w_in': nrm(ks[8], (DEPTH, D, IN_COLS), D ** -0.5),
        'w_out': nrm(ks[9], (DEPTH, D_MIX, D), D_MIX ** -0.5),
        'hgrn_lb_logits': nrm(ks[10], (DEPTH, 2, A_WIDTH), 0.5),
        'hgrn_norm': 1.0 + nrm(ks[11], (DEPTH, HEAD_DIM), 0.02),
        'diff_lambda': nrm(ks[12], (DEPTH, 4, HEAD_DIM), 0.1),
        'diff_subln': 1.0 + nrm(ks[13], (DEPTH, 2 * HEAD_DIM), 0.02),
        'na_rpb': nrm(ks[14], (DEPTH, C_HEADS, 2 * NA_KH - 1, 2 * NA_KW - 1), 0.1),
        'router_w': nrm(ks[15], (DEPTH, D, N_EXPERTS), D ** -0.5),
        'router_b': nrm(ks[16], (DEPTH, N_EXPERTS), 0.01),
        'expert_w_gu': nrm(ks[17], (DEPTH, N_EXPERTS, D, 2 * D_EXPERT), D ** -0.5),
        'expert_b_gu': nrm(ks[18], (DEPTH, N_EXPERTS, 2 * D_EXPERT), 0.02),
        'expert_w_down': nrm(ks[19], (DEPTH, N_EXPERTS, D_EXPERT, D), D_EXPERT ** -0.5),
        'expert_b_down': nrm(ks[20], (DEPTH, N_EXPERTS, D), 0.02),
        'final_norm': 1.0 + nrm(ks[21], (D,), 0.02),
    }


def reference(x, c, ctx, c_ctx, ada_w, ada_b, norm_mix, norm_ffn, w_in, w_out, hgrn_lb_logits, hgrn_norm,
              diff_lambda, diff_subln, na_rpb, router_w, router_b, expert_w_gu, expert_b_gu, expert_w_down,
              expert_b_down, final_norm):
    n_tok = x.shape[1]
    ang_row, ang_col = grid_angles(n_tok)
    p_lb = jax.nn.softmax(hgrn_lb_logits.astype(jnp.float32), axis=0)
    lower_bounds = jnp.clip(jnp.cumsum(p_lb, axis=0) - p_lb[0], 0.0, 1.0)
    cond_x = jax.nn.silu(c)
    cond_c = jax.nn.silu(c_ctx)
    h_ctx = ctx
    for layer in range(DEPTH):
        last = layer == DEPTH - 1
        lambda_init = 0.8 - 0.6 * math.exp(-0.3 * layer)
        mod_x = (cond_x @ ada_w[layer] + ada_b[layer])[:, None, :]
        shift1, scale1, gate1, shift2, scale2, gate2 = jnp.split(mod_x, N_MOD, axis=-1)
        n_c = 2 if last else N_MOD
        mod_c = jnp.split(cond_c @ ada_w[layer, :, :n_c * D_MODEL] + ada_b[layer, :n_c * D_MODEL], n_c)

        hx = rmsnorm(x, norm_mix[layer]) * (1.0 + scale1) + shift1
        hc = rmsnorm(h_ctx, norm_mix[layer]) * (1.0 + mod_c[1]) + mod_c[0]
        y_x, y_c = token_mixer(hx, hc, w_in[layer], w_out[layer], lower_bounds[layer], hgrn_norm[layer],
                               diff_lambda[layer], diff_subln[layer], na_rpb[layer], lambda_init,
                               ang_row, ang_col, not last)
        x = x + gate1 * y_x
        hx = rmsnorm(x, norm_ffn[layer]) * (1.0 + scale2) + shift2
        x = x + gate2 * moe_ffn(hx, router_w[layer], router_b[layer], expert_w_gu[layer], expert_b_gu[layer],
                                expert_w_down[layer], expert_b_down[layer])
        if not last:
            h_ctx = h_ctx + mod_c[2] * y_c
            hc = rmsnorm(h_ctx, norm_ffn[layer]) * (1.0 + mod_c[4]) + mod_c[3]
            h_ctx = h_ctx + mod_c[5] * moe_ffn(hc, router_w[layer], router_b[layer], expert_w_gu[layer],
                                               expert_b_gu[layer], expert_w_down[layer], expert_b_down[layer])
    return rmsnorm(x, final_norm)
```

```python
import functools
import math

import numpy as np
import jax
import jax.numpy as jnp
from jax import lax
from jax.experimental import pallas as pl
from jax.experimental.pallas import tpu as pltpu

F32 = jnp.float32
BF16 = jnp.bfloat16

D_MODEL = 4096
HEAD_DIM = 128
GRID_W = 64
A_HEADS = 12
A_WIDTH = A_HEADS * HEAD_DIM
B_HEADS = 5
B_WIDTH = B_HEADS * 2 * HEAD_DIM
C_HEADS = 10
C_WIDTH = C_HEADS * HEAD_DIM
A_COLS = 5 * A_WIDTH
B_COLS = 3 * B_WIDTH
C_COLS = 3 * C_WIDTH
IN_COLS = A_COLS + B_COLS + C_COLS
NA_KH = 8
NA_KW = 16
ROPE_BASE = 10000.0
N_EXPERTS = 32
TOP_K = 4
D_EXPERT = 512
SWIGLU_ALPHA = 1.702
SWIGLU_LIMIT = 7.0
N_MOD = 6
NORM_EPS = 1e-6
F_TINY = 1e-30
MASK_NEG = -1e30

V7X_VMEM_LIMIT_BYTES = 56 * 1024 * 1024

SCAN_CHUNK = 64
SCAN_SUB = 16
NA_QROWS = 4
NA_QBLK = NA_QROWS * GRID_W
NA_KBLKS = 3


def _cparams(sem):
    return pltpu.CompilerParams(dimension_semantics=sem, vmem_limit_bytes=V7X_VMEM_LIMIT_BYTES)


def _dot(a, b):
    return jnp.dot(a, b, preferred_element_type=F32)


def _dot_nt(a, b):
    return lax.dot_general(a, b, (((1,), (1,)), ((), ())), preferred_element_type=F32)


def _dot_tn(a, b):
    return lax.dot_general(a, b, (((0,), (0,)), ((), ())), preferred_element_type=F32)


def _sigmoid(x):
    return 1.0 / (1.0 + jnp.exp(-x))


def _row_is_ctx(row0, rows, n_ctx):
    return (row0 + lax.broadcasted_iota(jnp.int32, (rows, 1), 0)) < n_ctx


def _ada_kernel(c_ref, w_ref, b_ref, o_ref):
    c = c_ref[...]
    a = (c * _sigmoid(c)).astype(BF16)
    o_ref[...] = _dot(a, w_ref[...].astype(BF16)) + b_ref[...]


def ada_mod(c8, ada_w, ada_b, tn=512):
    depth, d, n = ada_w.shape
    return pl.pallas_call(
        _ada_kernel,
        grid=(depth, n // tn),
        in_specs=[pl.BlockSpec((8, d), lambda l, j: (0, 0)),
                  pl.BlockSpec((None, d, tn), lambda l, j: (l, 0, j)),
                  pl.BlockSpec((None, 1, tn), lambda l, j: (l, 0, j))],
        out_specs=pl.BlockSpec((None, 8, tn), lambda l, j: (l, 0, j)),
        out_shape=jax.ShapeDtypeStruct((depth, 8, n), F32),
        compiler_params=_cparams(("arbitrary", "arbitrary")),
        name="ada_mod",
    )(c8, ada_w, ada_b.reshape(depth, 1, n))


def _norm_mod(x, g, mod_ref, si, row0, n_ctx):
    ms = jnp.mean(x * x, axis=-1, keepdims=True)
    y = x * lax.rsqrt(ms + NORM_EPS) * g
    is_ctx = _row_is_ctx(row0, x.shape[0], n_ctx)
    shift = jnp.where(is_ctx, mod_ref[1, si:si + 1, :], mod_ref[0, si:si + 1, :])
    scale = jnp.where(is_ctx, mod_ref[1, si + 1:si + 2, :], mod_ref[0, si + 1:si + 2, :])
    return y * (1.0 + scale) + shift


def _norm_kernel(x_ref, g_ref, mod_ref, o_ref, *, si, n_ctx, tr):
    row0 = pl.program_id(0) * tr
    o_ref[...] = _norm_mod(x_ref[...], g_ref[...], mod_ref, si, row0, n_ctx).astype(o_ref.dtype)


def norm_mod(x, g, mods, si, n_ctx, tr=256):
    n, d = x.shape
    return pl.pallas_call(
        functools.partial(_norm_kernel, si=si, n_ctx=n_ctx, tr=tr),
        grid=(n // tr,),
        in_specs=[pl.BlockSpec((tr, d), lambda i: (i, 0)),
                  pl.BlockSpec((1, d), lambda i: (0, 0)),
                  pl.BlockSpec((2, N_MOD, d), lambda i: (0, 0, 0))],
        out_specs=pl.BlockSpec((tr, d), lambda i: (i, 0)),
        out_shape=jax.ShapeDtypeStruct((n, d), BF16),
        compiler_params=_cparams(("parallel",)),
        name="norm_mod",
    )(x, g.reshape(1, d), mods)


def _split_bf16(x):
    hi = x.astype(BF16)
    lo = (x - hi.astype(F32)).astype(BF16)
    return hi, lo


def _norm_router_kernel(x_ref, g_ref, mod_ref, rw_ref, rb_ref, h_ref, gates_ref, *, si, n_ctx, tr):
    row0 = pl.program_id(0) * tr
    h = _norm_mod(x_ref[...], g_ref[...], mod_ref, si, row0, n_ctx)
    h_ref[...] = h.astype(BF16)
    hh, hl = _split_bf16(h)
    wh, wl = _split_bf16(rw_ref[...])
    logits = _dot(hh, wh) + _dot(hh, wl) + _dot(hl, wh) + rb_ref[...]
    iota = lax.broadcasted_iota(jnp.int32, logits.shape, 1)
    work = logits
    vals, sels = [], []
    for _ in range(TOP_K):
        m = jnp.max(work, axis=-1, keepdims=True)
        idx = jnp.min(jnp.where(work == m, iota, N_EXPERTS), axis=-1, keepdims=True)
        sel = iota == idx
        vals.append(m)
        sels.append(sel)
        work = jnp.where(sel, -jnp.inf, work)
    es = [jnp.exp(v - vals[0]) for v in vals]
    tot = es[0] + es[1] + es[2] + es[3]
    gates = jnp.zeros_like(logits)
    for k in range(TOP_K):
        gates = gates + jnp.where(sels[k], es[k] / tot, 0.0)
    gates_ref[...] = gates


def norm_router(x, g, mods, si, router_w, router_b, n_ctx, tr=256):
    n, d = x.shape
    return pl.pallas_call(
        functools.partial(_norm_router_kernel, si=si, n_ctx=n_ctx, tr=tr),
        grid=(n // tr,),
        in_specs=[pl.BlockSpec((tr, d), lambda i: (i, 0)),
                  pl.BlockSpec((1, d), lambda i: (0, 0)),
                  pl.BlockSpec((2, N_MOD, d), lambda i: (0, 0, 0)),
                  pl.BlockSpec((d, N_EXPERTS), lambda i: (0, 0)),
                  pl.BlockSpec((1, N_EXPERTS), lambda i: (0, 0))],
        out_specs=[pl.BlockSpec((tr, d), lambda i: (i, 0)),
                   pl.BlockSpec((tr, N_EXPERTS), lambda i: (i, 0))],
        out_shape=[jax.ShapeDtypeStruct((n, d), BF16),
                   jax.ShapeDtypeStruct((n, N_EXPERTS), F32)],
        compiler_params=_cparams(("parallel",)),
        name="norm_router",
    )(x, g.reshape(1, d), mods, router_w, router_b.reshape(1, N_EXPERTS))


def _final_norm_kernel(x_ref, g_ref, o_ref):
    x = x_ref[...]
    ms = jnp.mean(x * x, axis=-1, keepdims=True)
    o_ref[...] = x * lax.rsqrt(ms + NORM_EPS) * g_ref[...]


def final_norm(x, g, n_ctx, tr=256):
    n, d = x.shape
    skip = n_ctx // tr
    return pl.pallas_call(
        _final_norm_kernel,
        grid=((n - n_ctx) // tr,),
        in_specs=[pl.BlockSpec((tr, d), lambda i: (i + skip, 0)),
                  pl.BlockSpec((1, d), lambda i: (0, 0))],
        out_specs=pl.BlockSpec((tr, d), lambda i: (i, 0)),
        out_shape=jax.ShapeDtypeStruct((n - n_ctx, d), F32),
        compiler_params=_cparams(("parallel",)),
        name="final_norm",
    )(x, g.reshape(1, d))


def _mm_kernel(a_ref, b_ref, o_ref):
    o_ref[...] = _dot(a_ref[...], b_ref[...]).astype(o_ref.dtype)


def matmul(a, b, tm, tn, out_dtype):
    m, k = a.shape
    n = b.shape[1]
    return pl.pallas_call(
        _mm_kernel,
        grid=(m // tm, n // tn),
        in_specs=[pl.BlockSpec((tm, k), lambda i, j: (i, 0)),
                  pl.BlockSpec((k, tn), lambda i, j: (0, j))],
        out_specs=pl.BlockSpec((tm, tn), lambda i, j: (i, j)),
        out_shape=jax.ShapeDtypeStruct((m, n), out_dtype),
        compiler_params=_cparams(("parallel", "arbitrary")),
        name="matmul",
    )(a, b)


def _mm_res_kernel(a_ref, b_ref, x_ref, mod_ref, o_ref, *, gi, n_ctx, tm):
    row0 = pl.program_id(0) * tm
    is_ctx = _row_is_ctx(row0, tm, n_ctx)
    gate = jnp.where(is_ctx, mod_ref[1, gi:gi + 1, :], mod_ref[0, gi:gi + 1, :])
    o_ref[...] = x_ref[...] + gate * _dot(a_ref[...], b_ref[...])


def matmul_residual(a, b, x, mods, gi, n_ctx, tm, tn):
    m, k = a.shape
    n = b.shape[1]
    return pl.pallas_call(
        functools.partial(_mm_res_kernel, gi=gi, n_ctx=n_ctx, tm=tm),
        grid=(m // tm, n // tn),
        in_specs=[pl.BlockSpec((tm, k), lambda i, j: (i, 0)),
                  pl.BlockSpec((k, tn), lambda i, j: (0, j)),
                  pl.BlockSpec((tm, tn), lambda i, j: (i, j)),
                  pl.BlockSpec((2, N_MOD, tn), lambda i, j: (0, 0, j))],
        out_specs=pl.BlockSpec((tm, tn), lambda i, j: (i, j)),
        out_shape=jax.ShapeDtypeStruct((m, n), F32),
        compiler_params=_cparams(("parallel", "arbitrary")),
        name="matmul_residual",
    )(a, b, x, mods)


def _hgrn_masks(c, sub, reverse):
    r = lax.broadcasted_iota(jnp.int32, (c, c), 0)
    s = lax.broadcasted_iota(jnp.int32, (c, c), 1)
    tri = (s >= r) if reverse else (s <= r)
    same_sub = (r // sub) == (s // sub)
    return tri, same_sub


def _hgrn_head(q_raw, f_pre, v, lb, st_in, *, reverse, sel):
    c, d = q_raw.shape
    sub = SCAN_SUB
    nsub = c // sub
    q = q_raw * _sigmoid(q_raw) * (HEAD_DIM ** -0.5)
    f = lb + (1.0 - lb) * _sigmoid(f_pre)
    log_f = jnp.log(jnp.maximum(f, F_TINY))
    k = (1.0 - lb) * _sigmoid(-f_pre)

    tri, same_sub = _hgrn_masks(c, sub, reverse)
    tri_b = jnp.where(tri, 1.0, 0.0).astype(BF16)
    hi = log_f.astype(BF16)
    r1 = log_f - hi.astype(F32)
    mid = r1.astype(BF16)
    lo = (r1 - mid.astype(F32)).astype(BF16)
    cum = _dot(tri_b, hi) + _dot(tri_b, mid) + _dot(tri_b, lo)

    order = list(range(nsub))
    if reverse:
        order = order[::-1]
    zero_row = jnp.zeros((1, d), F32)
    bnd_in, bnd_out = {}, {}
    prev = zero_row
    for i in order:
        last = i * sub if reverse else i * sub + sub - 1
        bnd_in[i] = prev
        bnd_out[i] = cum[last:last + 1, :]
        prev = bnd_out[i]
    cum_end = prev

    t_in_sub = lax.broadcasted_iota(jnp.int32, (sub, 1), 0)
    q_hat, u_rows, off_rows = [], [], []
    for i in range(nsub):
        sl = slice(i * sub, (i + 1) * sub)
        q_i, k_i, cum_i = q[sl], k[sl], cum[sl]
        q_hat_i = q_i * jnp.exp(cum_i - bnd_in[i])
        k_parts = []
        for j in range(nsub):
            earlier = (j > i) if reverse else (j < i)
            sj = slice(j * sub, (j + 1) * sub)
            if earlier:
                k_parts.append(k[sj] * jnp.exp(bnd_in[i] - cum[sj]))
            else:
                k_parts.append(jnp.zeros((sub, d), F32))
        k_til = jnp.concatenate(k_parts, axis=0).astype(BF16)
        off_rows.append(_dot_nt(q_hat_i.astype(BF16), k_til))
        u_parts = []
        for s in range(sub):
            ok = (t_in_sub <= s) if reverse else (t_in_sub >= s)
            e = jnp.where(ok, cum_i - cum_i[s:s + 1, :], MASK_NEG)
            u_parts.append((q_i * k_i[s:s + 1, :] * jnp.exp(e)).astype(BF16))
        u_rows.append(jnp.concatenate(u_parts, axis=1))
    u = jnp.concatenate(u_rows, axis=0)
    diag = _dot(u, sel)
    attn = jnp.concatenate(off_rows, axis=0) + jnp.where(same_sub, diag, 0.0)
    o = _dot(attn.astype(BF16), v.astype(BF16))
    o = o + _dot_nt((q * jnp.exp(cum)).astype(BF16), st_in.astype(BF16))
    k_end = (k * jnp.exp(cum_end - cum)).astype(BF16)
    st_out = st_in * jnp.exp(cum_end) + _dot_tn(v.astype(BF16), k_end)
    return o, st_out


def _hgrn_sel(c, sub):
    rows = np.arange(sub * HEAD_DIM) // HEAD_DIM
    cols = np.arange(c) % sub
    return jnp.asarray((rows[:, None] == cols[None, :]).astype(np.float32), dtype=BF16)


def _hgrn_fw_kernel(q_ref, f_ref, v_ref, lb_ref, sel_ref, o_ref, st_ref, *, hb):
    @pl.when(pl.program_id(1) == 0)
    def _():
        st_ref[...] = jnp.zeros_like(st_ref)

    sel = sel_ref[...]
    for h in range(hb):
        cs = slice(h * HEAD_DIM, (h + 1) * HEAD_DIM)
        o, st = _hgrn_head(q_ref[:, cs].astype(F32), f_ref[:, cs].astype(F32), v_ref[:, cs].astype(F32),
                           lb_ref[0:1, cs], st_ref[h], reverse=False, sel=sel)
        st_ref[h] = st
        o_ref[:, cs] = o


def _hgrn_bw_kernel(q_ref, f_ref, v_ref, g_ref, ofw_ref, lb_ref, ng_ref, sel_ref, y_ref, st_ref, *, hb):
    @pl.when(pl.program_id(1) == 0)
    def _():
        st_ref[...] = jnp.zeros_like(st_ref)

    sel = sel_ref[...]
    for h in range(hb):
        cs = slice(h * HEAD_DIM, (h + 1) * HEAD_DIM)
        o, st = _hgrn_head(q_ref[:, cs].astype(F32), f_ref[:, cs].astype(F32), v_ref[:, cs].astype(F32),
                           lb_ref[1:2, cs], st_ref[h], reverse=True, sel=sel)
        st_ref[h] = st
        o = o + ofw_ref[:, cs]
        ms = jnp.mean(o * o, axis=-1, keepdims=True)
        y = o * lax.rsqrt(ms + NORM_EPS) * ng_ref[...]
        g = g_ref[:, cs].astype(F32)
        y_ref[:, cs] = (y * (g * _sigmoid(g))).astype(y_ref.dtype)


def hgrn_mixer(p, lb, norm_g, n_ctx, hb=4):
    n = p.shape[0]
    c = SCAN_CHUNK
    nc = n // c
    nc_ctx = n_ctx // c
    ngroups = A_HEADS // hb
    w = hb * HEAD_DIM
    per = A_WIDTH // w
    sel = _hgrn_sel(c, SCAN_SUB)

    def fw_chunk(s):
        return s

    def bw_chunk(s):
        return jnp.where(s < nc_ctx, nc_ctx - 1 - s, nc - 1 - (s - nc_ctx))

    def col(section, chunk_of):
        return pl.BlockSpec((c, w), lambda g, s: (chunk_of(s), section * per + g))

    lb_spec = pl.BlockSpec((2, w), lambda g, s: (0, g))
    sel_spec = pl.BlockSpec(sel.shape, lambda g, s: (0, 0))
    scratch = [pltpu.VMEM((hb, HEAD_DIM, HEAD_DIM), F32)]
    o_fw = pl.pallas_call(
        functools.partial(_hgrn_fw_kernel, hb=hb),
        grid=(ngroups, nc),
        in_specs=[col(0, fw_chunk), col(1, fw_chunk), col(3, fw_chunk), lb_spec, sel_spec],
        out_specs=pl.BlockSpec((c, w), lambda g, s: (s, g)),
        out_shape=jax.ShapeDtypeStruct((n, A_WIDTH), F32),
        scratch_shapes=scratch,
        compiler_params=_cparams(("parallel", "arbitrary")),
        name="hgrn_fw",
    )(p, p, p, lb, sel)
    return pl.pallas_call(
        functools.partial(_hgrn_bw_kernel, hb=hb),
        grid=(ngroups, nc),
        in_specs=[col(0, bw_chunk), col(2, bw_chunk), col(3, bw_chunk), col(4, bw_chunk),
                  pl.BlockSpec((c, w), lambda g, s: (bw_chunk(s), g)),
                  lb_spec, pl.BlockSpec((1, HEAD_DIM), lambda g, s: (0, 0)), sel_spec],
        out_specs=pl.BlockSpec((c, w), lambda g, s: (bw_chunk(s), g)),
        out_shape=jax.ShapeDtypeStruct((n, A_WIDTH), BF16),
        scratch_shapes=scratch,
        compiler_params=_cparams(("parallel", "arbitrary")),
        name="hgrn_bw",
    )(p, p, p, p, o_fw, lb, norm_g.reshape(1, HEAD_DIM), sel)


def rope_tables(n, n_ctx):
    t = jnp.arange(n - n_ctx, dtype=jnp.int32)
    half = HEAD_DIM // 4
    inv_freq = ROPE_BASE ** (-jnp.arange(half, dtype=F32) / half)
    ang_r = (t // GRID_W).astype(F32)[:, None] * inv_freq
    ang_c = (t % GRID_W).astype(F32)[:, None] * inv_freq
    cos = jnp.concatenate([jnp.cos(ang_r)] * 2 + [jnp.cos(ang_c)] * 2, axis=-1)
    sin = jnp.concatenate([-jnp.sin(ang_r), jnp.sin(ang_r), -jnp.sin(ang_c), jnp.sin(ang_c)], axis=-1)
    cos = jnp.concatenate([jnp.ones((n_ctx, HEAD_DIM), F32), cos], axis=0)
    sin = jnp.concatenate([jnp.zeros((n_ctx, HEAD_DIM), F32), sin], axis=0)
    return cos, sin


def _rope_kernel(q_ref, k_ref, cos_ref, sin_ref, qo_ref, ko_ref):
    cos = cos_ref[...]
    sin = sin_ref[...]
    lane = lax.broadcasted_iota(jnp.int32, cos.shape, 1)
    low = (lane & (HEAD_DIM // 4)) == 0
    for src, dst, scale in ((q_ref, qo_ref, HEAD_DIM ** -0.5), (k_ref, ko_ref, None)):
        for h in range(2 * B_HEADS):
            cs = slice(h * HEAD_DIM, (h + 1) * HEAD_DIM)
            x = src[:, cs].astype(F32)
            partner = jnp.where(low, pltpu.roll(x, HEAD_DIM - HEAD_DIM // 4, 1), pltpu.roll(x, HEAD_DIM // 4, 1))
            y = x * cos + partner * sin
            if scale is not None:
                y = y * scale
            dst[:, cs] = y.astype(dst.dtype)


def rope_qk(p, cos, sin, tr=256):
    n = p.shape[0]
    qblk = A_COLS // B_WIDTH
    return pl.pallas_call(
        _rope_kernel,
        grid=(n // tr,),
        in_specs=[pl.BlockSpec((tr, B_WIDTH), lambda i: (i, qblk)),
                  pl.BlockSpec((tr, B_WIDTH), lambda i: (i, qblk + 1)),
                  pl.BlockSpec((tr, HEAD_DIM), lambda i: (i, 0)),
                  pl.BlockSpec((tr, HEAD_DIM), lambda i: (i, 0))],
        out_specs=[pl.BlockSpec((tr, B_WIDTH), lambda i: (i, 0)),
                   pl.BlockSpec((tr, B_WIDTH), lambda i: (i, 0))],
        out_shape=[jax.ShapeDtypeStruct((n, B_WIDTH), BF16)] * 2,
        compiler_params=_cparams(("parallel",)),
        name="rope_qk",
    )(p, p, cos, sin)


def _diff_attn_kernel(lam_ref, g_ref, q_ref, k_ref, v_ref, o_ref, m_scr, l_scr, acc_scr, *, lambda_init, nk):
    j = pl.program_id(2)

    @pl.when(j == 0)
    def _():
        m_scr[...] = jnp.full_like(m_scr, -jnp.inf)
        l_scr[...] = jnp.zeros_like(l_scr)
        acc_scr[...] = jnp.zeros_like(acc_scr)

    q = q_ref[...]
    k = k_ref[...]
    v = v_ref[...]
    for m in range(2):
        cs = slice(m * HEAD_DIM, (m + 1) * HEAD_DIM)
        s = _dot_nt(q[:, cs], k[:, cs])
        m_prev = m_scr[m]
        m_new = jnp.maximum(m_prev, jnp.max(s, axis=-1, keepdims=True))
        alpha = jnp.exp(m_prev - m_new)
        p = jnp.exp(s - m_new)
        l_scr[m] = alpha * l_scr[m] + jnp.sum(p, axis=-1, keepdims=True)
        acc_scr[m] = alpha * acc_scr[m] + _dot(p.astype(BF16), v)
        m_scr[m] = m_new

    @pl.when(j == nk - 1)
    def _():
        lp = lam_ref[...]
        lam = (jnp.exp(jnp.sum(lp[0:1] * lp[1:2], axis=-1, keepdims=True))
               - jnp.exp(jnp.sum(lp[2:3] * lp[3:4], axis=-1, keepdims=True)) + lambda_init)
        o = acc_scr[0] / l_scr[0] - lam * (acc_scr[1] / l_scr[1])
        ms = jnp.mean(o * o, axis=-1, keepdims=True)
        y = o * lax.rsqrt(ms + NORM_EPS) * g_ref[...] * (1.0 - lambda_init)
        o_ref[...] = y.astype(o_ref.dtype)


def diff_attention(q, k, p, lam_p, subln_g, lambda_init, q_row0, n_q, n_kv, tq, tk):
    hw = 2 * HEAD_DIM
    nq, nk = n_q // tq, n_kv // tk
    qb0 = q_row0 // tq
    vblk = (A_COLS + 2 * B_WIDTH) // hw
    return pl.pallas_call(
        functools.partial(_diff_attn_kernel, lambda_init=lambda_init, nk=nk),
        grid=(B_HEADS, nq, nk),
        in_specs=[pl.BlockSpec((4, HEAD_DIM), lambda h, i, j: (0, 0)),
                  pl.BlockSpec((1, hw), lambda h, i, j: (0, 0)),
                  pl.BlockSpec((tq, hw), lambda h, i, j: (i + qb0, h)),
                  pl.BlockSpec((tk, hw), lambda h, i, j: (j, h)),
                  pl.BlockSpec((tk, hw), lambda h, i, j: (j, vblk + h))],
        out_specs=pl.BlockSpec((tq, hw), lambda h, i, j: (i, h)),
        out_shape=jax.ShapeDtypeStruct((n_q, B_WIDTH), BF16),
        scratch_shapes=[pltpu.VMEM((2, tq, 1), F32), pltpu.VMEM((2, tq, 1), F32), pltpu.VMEM((2, tq, hw), F32)],
        compiler_params=_cparams(("parallel", "parallel", "arbitrary")),
        name="diff_attention",
    )(lam_p, subln_g.reshape(1, hw), q, k, p)


def na_bias_tables(rpb, rows):
    krows = NA_KBLKS * NA_QROWS
    cases = [(0, 0), (NA_QROWS, 0), (rows - NA_QROWS, rows - krows)]
    qr = np.arange(NA_QROWS)[:, None, None, None]
    qc = np.arange(GRID_W)[None, :, None, None]
    kr = np.arange(krows)[None, None, :, None]
    kc = np.arange(GRID_W)[None, None, None, :]
    idx_r, idx_c, valid = [], [], []
    for r0, start in cases:
        r = r0 + qr
        kra = start + kr
        rs = np.clip(r - NA_KH // 2, 0, rows - NA_KH)
        cs = np.clip(qc - NA_KW // 2, 0, GRID_W - NA_KW)
        ok = (kra >= rs) & (kra < rs + NA_KH) & (kc >= cs) & (kc < cs + NA_KW)
        ir = np.clip(kra - r + NA_KH - 1, 0, 2 * NA_KH - 2) + 0 * kc
        ic = np.clip(kc - qc + NA_KW - 1, 0, 2 * NA_KW - 2) + 0 * kr
        shape = (NA_QBLK, krows * GRID_W)
        idx_r.append(np.broadcast_to(ir, ok.shape).reshape(shape))
        idx_c.append(np.broadcast_to(ic, ok.shape).reshape(shape))
        valid.append(ok.reshape(shape))
    idx_r, idx_c, valid = np.stack(idx_r), np.stack(idx_c), np.stack(valid)
    bias = rpb.astype(F32)[:, idx_r, idx_c]
    bias = jnp.where(valid[None], bias, MASK_NEG)
    return jnp.transpose(bias, (1, 0, 2, 3))


def _na_kernel(q_ref, k0_ref, k1_ref, k2_ref, kc_ref, v0_ref, v1_ref, v2_ref, vc_ref, bias_ref, o_ref):
    scale = HEAD_DIM ** -0.5
    k_refs = (k0_ref, k1_ref, k2_ref)
    v_refs = (v0_ref, v1_ref, v2_ref, vc_ref)
    for h in range(C_HEADS):
        cs = slice(h * HEAD_DIM, (h + 1) * HEAD_DIM)
        q = q_ref[:, cs]
        s_parts = [_dot_nt(q, k_refs[b][:, cs]) * scale + bias_ref[h, :, b * NA_QBLK:(b + 1) * NA_QBLK]
                   for b in range(NA_KBLKS)]
        s_parts.append(_dot_nt(q, kc_ref[:, cs]) * scale)
        m = s_parts[0].max(axis=-1, keepdims=True)
        for s in s_parts[1:]:
            m = jnp.maximum(m, s.max(axis=-1, keepdims=True))
        l = jnp.zeros_like(m)
        acc = jnp.zeros((q.shape[0], HEAD_DIM), F32)
        for s, v_ref in zip(s_parts, v_refs):
            pexp = jnp.exp(s - m)
            l = l + pexp.sum(axis=-1, keepdims=True)
            acc = acc + _dot(pexp.astype(BF16), v_ref[:, cs])
        o_ref[:, cs] = (acc / l).astype(o_ref.dtype)


def na_attention(p, bias, n_ctx):
    n = p.shape[0]
    nqb = (n - n_ctx) // NA_QBLK
    cb = n_ctx // NA_QBLK
    qblk = (A_COLS + B_COLS) // C_WIDTH

    def kv_spec(section, b):
        return pl.BlockSpec((NA_QBLK, C_WIDTH),
                            lambda i: (cb + jnp.clip(i - 1, 0, nqb - NA_KBLKS) + b, qblk + section))

    def ctx_spec(section):
        return pl.BlockSpec((n_ctx, C_WIDTH), lambda i: (0, qblk + section))

    def case(i):
        return jnp.where(i == 0, 0, jnp.where(i == nqb - 1, 2, 1))

    return pl.pallas_call(
        _na_kernel,
        grid=(nqb,),
        in_specs=[pl.BlockSpec((NA_QBLK, C_WIDTH), lambda i: (cb + i, qblk)),
                  kv_spec(1, 0), kv_spec(1, 1), kv_spec(1, 2), ctx_spec(1),
                  kv_spec(2, 0), kv_spec(2, 1), kv_spec(2, 2), ctx_spec(2),
                  pl.BlockSpec((None, C_HEADS, NA_QBLK, NA_KBLKS * NA_QBLK), lambda i: (case(i), 0, 0, 0))],
        out_specs=pl.BlockSpec((NA_QBLK, C_WIDTH), lambda i: (i, 0)),
        out_shape=jax.ShapeDtypeStruct((n - n_ctx, C_WIDTH), BF16),
        compiler_params=_cparams(("parallel",)),
        name="na_attention",
    )(p, p, p, p, p, p, p, p, p, bias)


def _ctx_attn_kernel(q_ref, k_ref, v_ref, o_ref):
    scale = HEAD_DIM ** -0.5
    for h in range(C_HEADS):
        cs = slice(h * HEAD_DIM, (h + 1) * HEAD_DIM)
        s = _dot_nt(q_ref[:, cs], k_ref[:, cs]) * scale
        pexp = jnp.exp(s - s.max(axis=-1, keepdims=True))
        o = _dot(pexp.astype(BF16), v_ref[:, cs]) / pexp.sum(axis=-1, keepdims=True)
        o_ref[:, cs] = o.astype(o_ref.dtype)


def na_ctx_attention(p, n_ctx):
    qblk = (A_COLS + B_COLS) // C_WIDTH
    return pl.pallas_call(
        _ctx_attn_kernel,
        grid=(1,),
        in_specs=[pl.BlockSpec((n_ctx, C_WIDTH), lambda i, s=s: (0, qblk + s)) for s in range(3)],
        out_specs=pl.BlockSpec((n_ctx, C_WIDTH), lambda i: (0, 0)),
        out_shape=jax.ShapeDtypeStruct((n_ctx, C_WIDTH), BF16),
        compiler_params=_cparams(("arbitrary",)),
        name="na_ctx_attention",
    )(p, p, p)


def _moe_dense_kernel(h_ref, gates_ref, wg_ref, wu_ref, bg_ref, bu_ref, wd_ref, bd_ref, x_ref, mod_ref,
                      o_ref, acc_ref, *, gi, n_ctx, tm):
    e = pl.program_id(1)

    @pl.when(e == 0)
    def _():
        acc_ref[...] = jnp.zeros_like(acc_ref)

    h = h_ref[...]
    g = jnp.minimum(_dot(h, wg_ref[...]) + bg_ref[...], SWIGLU_LIMIT)
    u = jnp.clip(_dot(h, wu_ref[...]) + bu_ref[...], -SWIGLU_LIMIT, SWIGLU_LIMIT)
    act = (u + 1.0) * g * _sigmoid(g * SWIGLU_ALPHA)
    y = _dot(act.astype(BF16), wd_ref[...]) + bd_ref[...]
    gates = gates_ref[...]
    lane = lax.broadcasted_iota(jnp.int32, gates.shape, 1)
    gate_e = jnp.sum(jnp.where(lane == e, gates, 0.0), axis=-1, keepdims=True)
    acc_ref[...] += gate_e * y

    @pl.when(e == N_EXPERTS - 1)
    def _():
        row0 = pl.program_id(0) * tm
        is_ctx = _row_is_ctx(row0, tm, n_ctx)
        gate = jnp.where(is_ctx, mod_ref[1, gi:gi + 1, :], mod_ref[0, gi:gi + 1, :])
        o_ref[...] = x_ref[...] + gate * acc_ref[...]


def moe_dense(h, gates, wg, wu, bg, bu, wd, bd, x, mods, gi, n_ctx, tm=256):
    n, d = h.shape
    de = wg.shape[-1]
    return pl.pallas_call(
        functools.partial(_moe_dense_kernel, gi=gi, n_ctx=n_ctx, tm=tm),
        grid=(n // tm, N_EXPERTS),
        in_specs=[pl.BlockSpec((tm, d), lambda i, e: (i, 0)),
                  pl.BlockSpec((tm, N_EXPERTS), lambda i, e: (i, 0)),
                  pl.BlockSpec((None, d, de), lambda i, e: (e, 0, 0)),
                  pl.BlockSpec((None, d, de), lambda i, e: (e, 0, 0)),
                  pl.BlockSpec((None, 1, de), lambda i, e: (e, 0, 0)),
                  pl.BlockSpec((None, 1, de), lambda i, e: (e, 0, 0)),
                  pl.BlockSpec((None, de, d), lambda i, e: (e, 0, 0)),
                  pl.BlockSpec((None, 1, d), lambda i, e: (e, 0, 0)),
                  pl.BlockSpec((tm, d), lambda i, e: (i, 0)),
                  pl.BlockSpec((2, N_MOD, d), lambda i, e: (0, 0, 0))],
        out_specs=pl.BlockSpec((tm, d), lambda i, e: (i, 0)),
        out_shape=jax.ShapeDtypeStruct((n, d), F32),
        scratch_shapes=[pltpu.VMEM((tm, d), F32)],
        compiler_params=_cparams(("parallel", "arbitrary")),
        name="moe_dense",
    )(h, gates, wg, wu, bg, bu, wd, bd, x, mods)


def _mm_tile(m):
    for tm in (1280, 1024, 640, 512, 256):
        if m % tm == 0:
            return tm
    raise ValueError(f"unsupported row count {m}")


def kernel(x, c, ctx, c_ctx, ada_w, ada_b, norm_mix, norm_ffn, w_in, w_out, hgrn_lb_logits, hgrn_norm,
           diff_lambda, diff_subln, na_rpb, router_w, router_b, expert_w_gu, expert_b_gu, expert_w_down,
           expert_b_down, final_norm_g):
    assert x.shape[0] == 1 and c.shape[0] == 1
    depth = ada_w.shape[0]
    n_ctx = ctx.shape[1]
    s_len = x.shape[1]
    n = n_ctx + s_len
    rows = s_len // GRID_W
    tm = _mm_tile(n)

    hall = jnp.concatenate([ctx[0], x[0]], axis=0)
    c8 = jnp.zeros((8, D_MODEL), F32).at[0].set(c[0]).at[1].set(c_ctx)
    mods_all = ada_mod(c8, ada_w, ada_b)[:, :2].reshape(depth, 2, N_MOD, D_MODEL)

    p_lb = jax.nn.softmax(hgrn_lb_logits.astype(F32), axis=0)
    lower_bounds = jnp.clip(jnp.cumsum(p_lb, axis=0) - p_lb[0], 0.0, 1.0)
    cos, sin = rope_tables(n, n_ctx)

    for layer in range(depth):
        lambda_init = 0.8 - 0.6 * math.exp(-0.3 * layer)
        mods = mods_all[layer]
        w_in_b = w_in[layer].astype(BF16)
        w_out_b = w_out[layer].astype(BF16)
        wg = expert_w_gu[layer][:, :, 0::2].astype(BF16)
        wu = expert_w_gu[layer][:, :, 1::2].astype(BF16)
        bg = expert_b_gu[layer][:, None, 0::2]
        bu = expert_b_gu[layer][:, None, 1::2]
        wd = expert_w_down[layer].astype(BF16)
        bd = expert_b_down[layer][:, None, :]

        h = norm_mod(hall, norm_mix[layer], mods, 0, n_ctx)
        p = matmul(h, w_in_b, tm, 512, BF16)

        ya = hgrn_mixer(p, lower_bounds[layer], hgrn_norm[layer], n_ctx)
        qr, kr = rope_qk(p, cos, sin)
        yb_x = diff_attention(qr, kr, p, diff_lambda[layer], diff_subln[layer], lambda_init,
                              n_ctx, s_len, n, 256, n // 13 if n % 13 == 0 else 256)
        yb_c = diff_attention(qr, kr, p, diff_lambda[layer], diff_subln[layer], lambda_init,
                              0, n_ctx, n_ctx, n_ctx, n_ctx)
        yc_x = na_attention(p, na_bias_tables(na_rpb[layer], rows), n_ctx)
        yc_c = na_ctx_attention(p, n_ctx)
        y = jnp.concatenate([ya, jnp.concatenate([yb_c, yb_x], axis=0),
                             jnp.concatenate([yc_c, yc_x], axis=0)], axis=1)
        hall = matmul_residual(y, w_out_b, hall, mods, 2, n_ctx, tm, 512)

        h2, gates = norm_router(hall, norm_ffn[layer], mods, 3, router_w[layer], router_b[layer], n_ctx)
        hall = moe_dense(h2, gates, wg, wu, bg, bu, wd, bd, hall, mods, 5, n_ctx)

    return final_norm(hall, final_norm_g, n_ctx)[None]
```

```python
import functools
import math

import numpy as np
import jax
import jax.numpy as jnp
from jax import lax
from jax.experimental import pallas as pl
from jax.experimental.pallas import tpu as pltpu

F32 = jnp.float32
BF16 = jnp.bfloat16

D_MODEL = 4096
HEAD_DIM = 128
GRID_W = 64
A_HEADS = 12
A_WIDTH = A_HEADS * HEAD_DIM
B_HEADS = 5
B_WIDTH = B_HEADS * 2 * HEAD_DIM
C_HEADS = 10
C_WIDTH = C_HEADS * HEAD_DIM
A_COLS = 5 * A_WIDTH
B_COLS = 3 * B_WIDTH
C_COLS = 3 * C_WIDTH
IN_COLS = A_COLS + B_COLS + C_COLS
NA_KH = 8
NA_KW = 16
ROPE_BASE = 10000.0
N_EXPERTS = 32
TOP_K = 4
D_EXPERT = 512
SWIGLU_ALPHA = 1.702
SWIGLU_LIMIT = 7.0
N_MOD = 6
NORM_EPS = 1e-6
F_TINY = 1e-30
MASK_NEG = -1e30

V7X_VMEM_LIMIT_BYTES = 56 * 1024 * 1024

SCAN_CHUNK = 64
SCAN_SUB = 16
NA_QROWS = 4
NA_QBLK = NA_QROWS * GRID_W
NA_KBLKS = 3


def _cparams(sem):
    return pltpu.CompilerParams(dimension_semantics=sem, vmem_limit_bytes=V7X_VMEM_LIMIT_BYTES)


def _dot(a, b):
    return jnp.dot(a, b, preferred_element_type=F32)


def _dot_nt(a, b):
    return lax.dot_general(a, b, (((1,), (1,)), ((), ())), preferred_element_type=F32)


def _dot_tn(a, b):
    return lax.dot_general(a, b, (((0,), (0,)), ((), ())), preferred_element_type=F32)


def _sigmoid(x):
    return 1.0 / (1.0 + jnp.exp(-x))


def _row_is_ctx(row0, rows, n_ctx):
    return (row0 + lax.broadcasted_iota(jnp.int32, (rows, 1), 0)) < n_ctx


def _ada_kernel(c_ref, w_ref, b_ref, o_ref):
    c = c_ref[...]
    a = (c * _sigmoid(c)).astype(BF16)
    o_ref[...] = _dot(a, w_ref[...].astype(BF16)) + b_ref[...]


def ada_mod(c8, ada_w, ada_b, tn=512):
    depth, d, n = ada_w.shape
    return pl.pallas_call(
        _ada_kernel,
        grid=(depth, n // tn),
        in_specs=[pl.BlockSpec((8, d), lambda l, j: (0, 0)),
                  pl.BlockSpec((None, d, tn), lambda l, j: (l, 0, j)),
                  pl.BlockSpec((None, 1, tn), lambda l, j: (l, 0, j))],
        out_specs=pl.BlockSpec((None, 8, tn), lambda l, j: (l, 0, j)),
        out_shape=jax.ShapeDtypeStruct((depth, 8, n), F32),
        compiler_params=_cparams(("arbitrary", "arbitrary")),
        name="ada_mod",
    )(c8, ada_w, ada_b.reshape(depth, 1, n))


def _norm_mod(x, g, mod_ref, si, row0, n_ctx):
    ms = jnp.mean(x * x, axis=-1, keepdims=True)
    y = x * lax.rsqrt(ms + NORM_EPS) * g
    is_ctx = _row_is_ctx(row0, x.shape[0], n_ctx)
    shift = jnp.where(is_ctx, mod_ref[1, si:si + 1, :], mod_ref[0, si:si + 1, :])
    scale = jnp.where(is_ctx, mod_ref[1, si + 1:si + 2, :], mod_ref[0, si + 1:si + 2, :])
    return y * (1.0 + scale) + shift


def _norm_kernel(x_ref, g_ref, mod_ref, o_ref, *, si, n_ctx, tr):
    row0 = pl.program_id(0) * tr
    o_ref[...] = _norm_mod(x_ref[...], g_ref[...], mod_ref, si, row0, n_ctx).astype(o_ref.dtype)


def norm_mod(x, g, mods, si, n_ctx, tr=256):
    n, d = x.shape
    return pl.pallas_call(
        functools.partial(_norm_kernel, si=si, n_ctx=n_ctx, tr=tr),
        grid=(n // tr,),
        in_specs=[pl.BlockSpec((tr, d), lambda i: (i, 0)),
                  pl.BlockSpec((1, d), lambda i: (0, 0)),
                  pl.BlockSpec((2, N_MOD, d), lambda i: (0, 0, 0))],
        out_specs=pl.BlockSpec((tr, d), lambda i: (i, 0)),
        out_shape=jax.ShapeDtypeStruct((n, d), BF16),
        compiler_params=_cparams(("parallel",)),
        name="norm_mod",
    )(x, g.reshape(1, d), mods)


def _split_bf16(x):
    hi = x.astype(BF16)
    lo = (x - hi.astype(F32)).astype(BF16)
    return hi, lo


def _pack_bf16_pair(lo, hi):
    lo_b = lax.bitcast_convert_type(lo.astype(BF16).astype(F32), jnp.uint32) >> 16
    hi_b = lax.bitcast_convert_type(hi.astype(BF16).astype(F32), jnp.uint32) & jnp.uint32(0xFFFF0000)
    return hi_b | lo_b


def _unpack_bf16_pair(w):
    lo = lax.bitcast_convert_type(w << 16, F32)
    hi = lax.bitcast_convert_type(w & jnp.uint32(0xFFFF0000), F32)
    return lo, hi


def _norm_router_kernel(x_ref, g_ref, mod_ref, rw_ref, rb_ref, h_ref, idx_ref, wts_ref, rank_ref, cnt_ref,
                        carry_ref, *, si, n_ctx, tr):
    @pl.when(pl.program_id(0) == 0)
    def _():
        carry_ref[...] = jnp.zeros_like(carry_ref)

    row0 = pl.program_id(0) * tr
    h = _norm_mod(x_ref[...], g_ref[...], mod_ref, si, row0, n_ctx)
    half = h.shape[1] // 2
    h_ref[...] = _pack_bf16_pair(h[:, :half], h[:, half:])
    hh, hl = _split_bf16(h)
    wh, wl = _split_bf16(rw_ref[...])
    logits = _dot(hh, wh) + _dot(hh, wl) + _dot(hl, wh) + rb_ref[...]
    iota = lax.broadcasted_iota(jnp.int32, logits.shape, 1)
    work = logits
    vals, sels = [], []
    for k in range(TOP_K):
        m = jnp.max(work, axis=-1, keepdims=True)
        idx = jnp.min(jnp.where(work == m, iota, N_EXPERTS), axis=-1, keepdims=True)
        sel = iota == idx
        vals.append(m)
        sels.append(sel)
        idx_ref[:, k:k + 1] = idx
        work = jnp.where(sel, -jnp.inf, work)
    es = [jnp.exp(v - vals[0]) for v in vals]
    tot = es[0] + es[1] + es[2] + es[3]
    for k in range(TOP_K):
        wts_ref[:, k:k + 1] = es[k] / tot
    used = jnp.zeros_like(logits)
    for k in range(TOP_K):
        used = used + jnp.where(sels[k], 1.0, 0.0)
    r = lax.broadcasted_iota(jnp.int32, (tr, tr), 0)
    c = lax.broadcasted_iota(jnp.int32, (tr, tr), 1)
    before = jnp.where(c < r, 1.0, 0.0).astype(BF16)
    prefix = _dot(before, used.astype(BF16)) + carry_ref[...]
    for k in range(TOP_K):
        rank = jnp.sum(jnp.where(sels[k], prefix, 0.0), axis=-1, keepdims=True)
        rank_ref[:, k:k + 1] = rank.astype(jnp.int32)
    carry_ref[...] += jnp.sum(used, axis=0, keepdims=True)
    cnt_ref[...] = carry_ref[...]


def norm_router(x, g, mods, si, router_w, router_b, n_ctx, tr=256):
    n, d = x.shape
    col = lambda i: (i, 0)
    fixed = lambda i: (0, 0)
    return pl.pallas_call(
        functools.partial(_norm_router_kernel, si=si, n_ctx=n_ctx, tr=tr),
        grid=(n // tr,),
        in_specs=[pl.BlockSpec((tr, d), col),
                  pl.BlockSpec((1, d), fixed),
                  pl.BlockSpec((2, N_MOD, d), lambda i: (0, 0, 0)),
                  pl.BlockSpec((d, N_EXPERTS), fixed),
                  pl.BlockSpec((1, N_EXPERTS), fixed)],
        out_specs=[pl.BlockSpec((tr, d // 2), col),
                   pl.BlockSpec((tr, TOP_K), col),
                   pl.BlockSpec((tr, TOP_K), col),
                   pl.BlockSpec((tr, TOP_K), col),
                   pl.BlockSpec((1, N_EXPERTS), fixed)],
        out_shape=[jax.ShapeDtypeStruct((n, d // 2), jnp.uint32),
                   jax.ShapeDtypeStruct((n, TOP_K), jnp.int32),
                   jax.ShapeDtypeStruct((n, TOP_K), F32),
                   jax.ShapeDtypeStruct((n, TOP_K), jnp.int32),
                   jax.ShapeDtypeStruct((1, N_EXPERTS), F32)],
        scratch_shapes=[pltpu.VMEM((1, N_EXPERTS), F32)],
        compiler_params=_cparams(("arbitrary",)),
        name="norm_router",
    )(x, g.reshape(1, d), mods, router_w, router_b.reshape(1, N_EXPERTS))


def _final_norm_kernel(x_ref, g_ref, o_ref):
    x = x_ref[...]
    ms = jnp.mean(x * x, axis=-1, keepdims=True)
    o_ref[...] = x * lax.rsqrt(ms + NORM_EPS) * g_ref[...]


def final_norm(x, g, n_ctx, tr=256):
    n, d = x.shape
    skip = n_ctx // tr
    return pl.pallas_call(
        _final_norm_kernel,
        grid=((n - n_ctx) // tr,),
        in_specs=[pl.BlockSpec((tr, d), lambda i: (i + skip, 0)),
                  pl.BlockSpec((1, d), lambda i: (0, 0))],
        out_specs=pl.BlockSpec((tr, d), lambda i: (i, 0)),
        out_shape=jax.ShapeDtypeStruct((n - n_ctx, d), F32),
        compiler_params=_cparams(("parallel",)),
        name="final_norm",
    )(x, g.reshape(1, d))


def _mm_kernel(a_ref, b_ref, o_ref):
    o_ref[...] = _dot(a_ref[...], b_ref[...]).astype(o_ref.dtype)


def matmul(a, b, tm, tn, out_dtype):
    m, k = a.shape
    n = b.shape[1]
    return pl.pallas_call(
        _mm_kernel,
        grid=(m // tm, n // tn),
        in_specs=[pl.BlockSpec((tm, k), lambda i, j: (i, 0)),
                  pl.BlockSpec((k, tn), lambda i, j: (0, j))],
        out_specs=pl.BlockSpec((tm, tn), lambda i, j: (i, j)),
        out_shape=jax.ShapeDtypeStruct((m, n), out_dtype),
        compiler_params=_cparams(("parallel", "arbitrary")),
        name="matmul",
    )(a, b)


def _mm_res_kernel(a_ref, b_ref, x_ref, mod_ref, o_ref, *, gi, n_ctx, tm):
    row0 = pl.program_id(0) * tm
    is_ctx = _row_is_ctx(row0, tm, n_ctx)
    gate = jnp.where(is_ctx, mod_ref[1, gi:gi + 1, :], mod_ref[0, gi:gi + 1, :])
    o_ref[...] = x_ref[...] + gate * _dot(a_ref[...], b_ref[...])


def matmul_residual(a, b, x, mods, gi, n_ctx, tm, tn):
    m, k = a.shape
    n = b.shape[1]
    return pl.pallas_call(
        functools.partial(_mm_res_kernel, gi=gi, n_ctx=n_ctx, tm=tm),
        grid=(m // tm, n // tn),
        in_specs=[pl.BlockSpec((tm, k), lambda i, j: (i, 0)),
                  pl.BlockSpec((k, tn), lambda i, j: (0, j)),
                  pl.BlockSpec((tm, tn), lambda i, j: (i, j)),
                  pl.BlockSpec((2, N_MOD, tn), lambda i, j: (0, 0, j))],
        out_specs=pl.BlockSpec((tm, tn), lambda i, j: (i, j)),
        out_shape=jax.ShapeDtypeStruct((m, n), F32),
        compiler_params=_cparams(("parallel", "arbitrary")),
        name="matmul_residual",
    )(a, b, x, mods)


def _hgrn_masks(c, sub, reverse):
    r = lax.broadcasted_iota(jnp.int32, (c, c), 0)
    s = lax.broadcasted_iota(jnp.int32, (c, c), 1)
    tri = (s >= r) if reverse else (s <= r)
    same_sub = (r // sub) == (s // sub)
    return tri, same_sub


def _hgrn_head(q_raw, f_pre, v, lb, st_in, *, reverse, sel):
    c, d = q_raw.shape
    sub = SCAN_SUB
    nsub = c // sub
    q = q_raw * _sigmoid(q_raw) * (HEAD_DIM ** -0.5)
    f = lb + (1.0 - lb) * _sigmoid(f_pre)
    log_f = jnp.log(jnp.maximum(f, F_TINY))
    k = (1.0 - lb) * _sigmoid(-f_pre)

    tri, same_sub = _hgrn_masks(c, sub, reverse)
    tri_b = jnp.where(tri, 1.0, 0.0).astype(BF16)
    hi = log_f.astype(BF16)
    r1 = log_f - hi.astype(F32)
    mid = r1.astype(BF16)
    lo = (r1 - mid.astype(F32)).astype(BF16)
    cum = _dot(tri_b, hi) + _dot(tri_b, mid) + _dot(tri_b, lo)

    order = list(range(nsub))
    if reverse:
        order = order[::-1]
    zero_row = jnp.zeros((1, d), F32)
    bnd_in, bnd_out = {}, {}
    prev = zero_row
    for i in order:
        last = i * sub if reverse else i * sub + sub - 1
        bnd_in[i] = prev
        bnd_out[i] = cum[last:last + 1, :]
        prev = bnd_out[i]
    cum_end = prev

    t_in_sub = lax.broadcasted_iota(jnp.int32, (sub, 1), 0)
    q_hat, u_rows, off_rows = [], [], []
    for i in range(nsub):
        sl = slice(i * sub, (i + 1) * sub)
        q_i, k_i, cum_i = q[sl], k[sl], cum[sl]
        q_hat_i = q_i * jnp.exp(cum_i - bnd_in[i])
        k_parts = []
        for j in range(nsub):
            earlier = (j > i) if reverse else (j < i)
            sj = slice(j * sub, (j + 1) * sub)
            if earlier:
                k_parts.append(k[sj] * jnp.exp(bnd_in[i] - cum[sj]))
            else:
                k_parts.append(jnp.zeros((sub, d), F32))
        k_til = jnp.concatenate(k_parts, axis=0).astype(BF16)
        off_rows.append(_dot_nt(q_hat_i.astype(BF16), k_til))
        u_parts = []
        for s in range(sub):
            r_lo, r_hi = (0, (s // 8 + 1) * 8) if reverse else (s // 8 * 8, sub)
            rows_t = t_in_sub[r_lo:r_hi]
            ok = (rows_t <= s) if reverse else (rows_t >= s)
            e = jnp.where(ok, cum_i[r_lo:r_hi] - cum_i[s:s + 1, :], MASK_NEG)
            part = [(q_i[r_lo:r_hi] * k_i[s:s + 1, :] * jnp.exp(e)).astype(BF16)]
            if r_lo > 0:
                part.insert(0, jnp.zeros((r_lo, d), BF16))
            if r_hi < sub:
                part.append(jnp.zeros((sub - r_hi, d), BF16))
            u_parts.append(part[0] if len(part) == 1 else jnp.concatenate(part, axis=0))
        u_rows.append(jnp.concatenate(u_parts, axis=1))
    u = jnp.concatenate(u_rows, axis=0)
    diag = _dot(u, sel)
    attn = jnp.concatenate(off_rows, axis=0) + jnp.where(same_sub, diag, 0.0)
    o = _dot(attn.astype(BF16), v.astype(BF16))
    o = o + _dot_nt((q * jnp.exp(cum)).astype(BF16), st_in.astype(BF16))
    k_end = (k * jnp.exp(cum_end - cum)).astype(BF16)
    st_out = st_in * jnp.exp(cum_end) + _dot_tn(v.astype(BF16), k_end)
    return o, st_out


def _hgrn_sel(c, sub):
    rows = np.arange(sub * HEAD_DIM) // HEAD_DIM
    cols = np.arange(c) % sub
    return jnp.asarray((rows[:, None] == cols[None, :]).astype(np.float32), dtype=BF16)


def _hgrn_fw_kernel(q_ref, f_ref, v_ref, lb_ref, sel_ref, o_ref, st_ref, *, hb):
    @pl.when(pl.program_id(1) == 0)
    def _():
        st_ref[...] = jnp.zeros_like(st_ref)

    sel = sel_ref[...]
    for h in range(hb):
        cs = slice(h * HEAD_DIM, (h + 1) * HEAD_DIM)
        o, st = _hgrn_head(q_ref[:, cs].astype(F32), f_ref[:, cs].astype(F32), v_ref[:, cs].astype(F32),
                           lb_ref[0:1, cs], st_ref[h], reverse=False, sel=sel)
        st_ref[h] = st
        o_ref[:, cs] = o


def _hgrn_bw_kernel(q_ref, f_ref, v_ref, g_ref, ofw_ref, lb_ref, ng_ref, sel_ref, y_ref, st_ref, *, hb):
    @pl.when(pl.program_id(1) == 0)
    def _():
        st_ref[...] = jnp.zeros_like(st_ref)

    sel = sel_ref[...]
    for h in range(hb):
        cs = slice(h * HEAD_DIM, (h + 1) * HEAD_DIM)
        o, st = _hgrn_head(q_ref[:, cs].astype(F32), f_ref[:, cs].astype(F32), v_ref[:, cs].astype(F32),
                           lb_ref[1:2, cs], st_ref[h], reverse=True, sel=sel)
        st_ref[h] = st
        o = o + ofw_ref[:, cs]
        ms = jnp.mean(o * o, axis=-1, keepdims=True)
        y = o * lax.rsqrt(ms + NORM_EPS) * ng_ref[...]
        g = g_ref[:, cs].astype(F32)
        y_ref[:, cs] = (y * (g * _sigmoid(g))).astype(y_ref.dtype)


def hgrn_mixer(p, lb, norm_g, n_ctx, hb=4):
    n = p.shape[0]
    c = SCAN_CHUNK
    nc = n // c
    nc_ctx = n_ctx // c
    ngroups = A_HEADS // hb
    w = hb * HEAD_DIM
    per = A_WIDTH // w
    sel = _hgrn_sel(c, SCAN_SUB)

    def fw_chunk(s):
        return s

    def bw_chunk(s):
        return jnp.where(s < nc_ctx, nc_ctx - 1 - s, nc - 1 - (s - nc_ctx))

    def col(section, chunk_of):
        return pl.BlockSpec((c, w), lambda g, s: (chunk_of(s), section * per + g))

    lb_spec = pl.BlockSpec((2, w), lambda g, s: (0, g))
    sel_spec = pl.BlockSpec(sel.shape, lambda g, s: (0, 0))
    scratch = [pltpu.VMEM((hb, HEAD_DIM, HEAD_DIM), F32)]
    o_fw = pl.pallas_call(
        functools.partial(_hgrn_fw_kernel, hb=hb),
        grid=(ngroups, nc),
        in_specs=[col(0, fw_chunk), col(1, fw_chunk), col(3, fw_chunk), lb_spec, sel_spec],
        out_specs=pl.BlockSpec((c, w), lambda g, s: (s, g)),
        out_shape=jax.ShapeDtypeStruct((n, A_WIDTH), F32),
        scratch_shapes=scratch,
        compiler_params=_cparams(("parallel", "arbitrary")),
        name="hgrn_fw",
    )(p, p, p, lb, sel)
    return pl.pallas_call(
        functools.partial(_hgrn_bw_kernel, hb=hb),
        grid=(ngroups, nc),
        in_specs=[col(0, bw_chunk), col(2, bw_chunk), col(3, bw_chunk), col(4, bw_chunk),
                  pl.BlockSpec((c, w), lambda g, s: (bw_chunk(s), g)),
                  lb_spec, pl.BlockSpec((1, HEAD_DIM), lambda g, s: (0, 0)), sel_spec],
        out_specs=pl.BlockSpec((c, w), lambda g, s: (bw_chunk(s), g)),
        out_shape=jax.ShapeDtypeStruct((n, A_WIDTH), BF16),
        scratch_shapes=scratch,
        compiler_params=_cparams(("parallel", "arbitrary")),
        name="hgrn_bw",
    )(p, p, p, p, o_fw, lb, norm_g.reshape(1, HEAD_DIM), sel)


def rope_tables(n, n_ctx):
    t = jnp.arange(n - n_ctx, dtype=jnp.int32)
    half = HEAD_DIM // 4
    inv_freq = ROPE_BASE ** (-jnp.arange(half, dtype=F32) / half)
    ang_r = (t // GRID_W).astype(F32)[:, None] * inv_freq
    ang_c = (t % GRID_W).astype(F32)[:, None] * inv_freq
    cos = jnp.concatenate([jnp.cos(ang_r)] * 2 + [jnp.cos(ang_c)] * 2, axis=-1)
    sin = jnp.concatenate([-jnp.sin(ang_r), jnp.sin(ang_r), -jnp.sin(ang_c), jnp.sin(ang_c)], axis=-1)
    cos = jnp.concatenate([jnp.ones((n_ctx, HEAD_DIM), F32), cos], axis=0)
    sin = jnp.concatenate([jnp.zeros((n_ctx, HEAD_DIM), F32), sin], axis=0)
    return cos, sin


def _rope_kernel(q_ref, k_ref, cos_ref, sin_ref, qo_ref, ko_ref):
    cos = cos_ref[...]
    sin = sin_ref[...]
    lane = lax.broadcasted_iota(jnp.int32, cos.shape, 1)
    low = (lane & (HEAD_DIM // 4)) == 0
    for src, dst, scale in ((q_ref, qo_ref, HEAD_DIM ** -0.5), (k_ref, ko_ref, None)):
        for h in range(2 * B_HEADS):
            cs = slice(h * HEAD_DIM, (h + 1) * HEAD_DIM)
            x = src[:, cs].astype(F32)
            partner = jnp.where(low, pltpu.roll(x, HEAD_DIM - HEAD_DIM // 4, 1), pltpu.roll(x, HEAD_DIM // 4, 1))
            y = x * cos + partner * sin
            if scale is not None:
                y = y * scale
            dst[:, cs] = y.astype(dst.dtype)


def rope_qk(p, cos, sin, tr=256):
    n = p.shape[0]
    qblk = A_COLS // B_WIDTH
    return pl.pallas_call(
        _rope_kernel,
        grid=(n // tr,),
        in_specs=[pl.BlockSpec((tr, B_WIDTH), lambda i: (i, qblk)),
                  pl.BlockSpec((tr, B_WIDTH), lambda i: (i, qblk + 1)),
                  pl.BlockSpec((tr, HEAD_DIM), lambda i: (i, 0)),
                  pl.BlockSpec((tr, HEAD_DIM), lambda i: (i, 0))],
        out_specs=[pl.BlockSpec((tr, B_WIDTH), lambda i: (i, 0)),
                   pl.BlockSpec((tr, B_WIDTH), lambda i: (i, 0))],
        out_shape=[jax.ShapeDtypeStruct((n, B_WIDTH), BF16)] * 2,
        compiler_params=_cparams(("parallel",)),
        name="rope_qk",
    )(p, p, cos, sin)


def _diff_attn_kernel(lam_ref, g_ref, q_ref, k_ref, v_ref, o_ref, m_scr, l_scr, acc_scr, *, lambda_init, nk, tk):
    tq = q_ref.shape[0]
    q = q_ref[...]
    q1 = q[:, :HEAD_DIM]
    q2 = q[:, HEAD_DIM:]
    m_scr[...] = jnp.full_like(m_scr, -jnp.inf)
    l_scr[...] = jnp.zeros_like(l_scr)
    acc_scr[...] = jnp.zeros_like(acc_scr)

    def body(j, carry):
        off = pl.multiple_of(j * tk, tk)
        k = k_ref[pl.ds(off, tk), :]
        v = v_ref[pl.ds(off, tk), :]
        s = jnp.concatenate([_dot_nt(q1, k[:, :HEAD_DIM]), _dot_nt(q2, k[:, HEAD_DIM:])], axis=0)
        m_prev = m_scr[...]
        m_new = jnp.maximum(m_prev, jnp.max(s, axis=-1, keepdims=True))
        alpha = jnp.exp(m_prev - m_new)
        p = jnp.exp(s - m_new)
        l_scr[...] = alpha * l_scr[...] + jnp.sum(p, axis=-1, keepdims=True)
        acc_scr[...] = alpha * acc_scr[...] + _dot(p.astype(BF16), v)
        m_scr[...] = m_new
        return carry

    lax.fori_loop(0, nk, body, 0)

    lp = lam_ref[...]
    lam = (jnp.exp(jnp.sum(lp[0:1] * lp[1:2], axis=-1, keepdims=True))
           - jnp.exp(jnp.sum(lp[2:3] * lp[3:4], axis=-1, keepdims=True)) + lambda_init)
    o = acc_scr[:tq] / l_scr[:tq] - lam * (acc_scr[tq:] / l_scr[tq:])
    ms = jnp.mean(o * o, axis=-1, keepdims=True)
    y = o * lax.rsqrt(ms + NORM_EPS) * g_ref[...] * (1.0 - lambda_init)
    o_ref[...] = y.astype(o_ref.dtype)


def diff_attention(q, k, p, lam_p, subln_g, lambda_init, q_row0, n_q, n_kv, tq, tk):
    hw = 2 * HEAD_DIM
    nq, nk = n_q // tq, n_kv // tk
    qb0 = q_row0 // tq
    vblk = (A_COLS + 2 * B_WIDTH) // hw
    return pl.pallas_call(
        functools.partial(_diff_attn_kernel, lambda_init=lambda_init, nk=nk, tk=tk),
        grid=(B_HEADS, nq),
        in_specs=[pl.BlockSpec((4, HEAD_DIM), lambda h, i: (0, 0)),
                  pl.BlockSpec((1, hw), lambda h, i: (0, 0)),
                  pl.BlockSpec((tq, hw), lambda h, i: (i + qb0, h)),
                  pl.BlockSpec((n_kv, hw), lambda h, i: (0, h)),
                  pl.BlockSpec((n_kv, hw), lambda h, i: (0, vblk + h))],
        out_specs=pl.BlockSpec((tq, hw), lambda h, i: (i, h)),
        out_shape=jax.ShapeDtypeStruct((n_q, B_WIDTH), BF16),
        scratch_shapes=[pltpu.VMEM((2 * tq, 1), F32), pltpu.VMEM((2 * tq, 1), F32),
                        pltpu.VMEM((2 * tq, hw), F32)],
        compiler_params=_cparams(("parallel", "arbitrary")),
        name="diff_attention",
    )(lam_p, subln_g.reshape(1, hw), q, k, p)


def na_bias_tables(rpb, rows):
    krows = NA_KBLKS * NA_QROWS
    n_dr, n_dc = 2 * NA_KH - 1, 2 * NA_KW - 1
    qc = np.arange(GRID_W)[:, None]
    kc = np.arange(GRID_W)[None, :]
    cs = np.clip(qc - NA_KW // 2, 0, GRID_W - NA_KW)
    col_ok = (kc >= cs) & (kc < cs + NA_KW)
    dc = np.clip(kc - qc + NA_KW - 1, 0, n_dc - 1)
    onehot = (np.arange(n_dc)[:, None, None] == dc[None]).astype(np.float32).reshape(n_dc, GRID_W * GRID_W)
    toep = jnp.dot(rpb.astype(F32).reshape(C_HEADS * n_dr, n_dc), jnp.asarray(onehot),
                   precision=lax.Precision.HIGHEST).reshape(C_HEADS, n_dr, GRID_W, GRID_W)
    toep = jnp.where(jnp.asarray(col_ok)[None, None], toep, MASK_NEG)
    masked = jnp.full((C_HEADS, GRID_W, GRID_W), MASK_NEG, F32)
    cases = [(0, 0), (NA_QROWS, 0), (rows - NA_QROWS, rows - krows)]
    out = []
    for r0, start in cases:
        q_blocks = []
        for qr in range(NA_QROWS):
            r = r0 + qr
            rs = min(max(r - NA_KH // 2, 0), rows - NA_KH)
            k_blocks = []
            for kr in range(krows):
                kra = start + kr
                k_blocks.append(toep[:, kra - r + NA_KH - 1] if rs <= kra < rs + NA_KH else masked)
            q_blocks.append(jnp.concatenate(k_blocks, axis=-1))
        out.append(jnp.concatenate(q_blocks, axis=1))
    return jnp.stack(out)


def _na_kernel(q_ref, k0_ref, k1_ref, k2_ref, kc_ref, v0_ref, v1_ref, v2_ref, vc_ref, bias_ref, o_ref):
    scale = HEAD_DIM ** -0.5
    k_refs = (k0_ref, k1_ref, k2_ref)
    v_refs = (v0_ref, v1_ref, v2_ref, vc_ref)
    for h in range(C_HEADS):
        cs = slice(h * HEAD_DIM, (h + 1) * HEAD_DIM)
        q = q_ref[:, cs]
        s_parts = [_dot_nt(q, k_refs[b][:, cs]) * scale + bias_ref[h, :, b * NA_QBLK:(b + 1) * NA_QBLK]
                   for b in range(NA_KBLKS)]
        s_parts.append(_dot_nt(q, kc_ref[:, cs]) * scale)
        m = s_parts[0].max(axis=-1, keepdims=True)
        for s in s_parts[1:]:
            m = jnp.maximum(m, s.max(axis=-1, keepdims=True))
        l = jnp.zeros_like(m)
        acc = jnp.zeros((q.shape[0], HEAD_DIM), F32)
        for s, v_ref in zip(s_parts, v_refs):
            pexp = jnp.exp(s - m)
            l = l + pexp.sum(axis=-1, keepdims=True)
            acc = acc + _dot(pexp.astype(BF16), v_ref[:, cs])
        o_ref[:, cs] = (acc / l).astype(o_ref.dtype)


def na_attention(p, bias, n_ctx):
    n = p.shape[0]
    nqb = (n - n_ctx) // NA_QBLK
    cb = n_ctx // NA_QBLK
    qblk = (A_COLS + B_COLS) // C_WIDTH

    def kv_spec(section, b):
        return pl.BlockSpec((NA_QBLK, C_WIDTH),
                            lambda i: (cb + jnp.clip(i - 1, 0, nqb - NA_KBLKS) + b, qblk + section))

    def ctx_spec(section):
        return pl.BlockSpec((n_ctx, C_WIDTH), lambda i: (0, qblk + section))

    def case(i):
        return jnp.where(i == 0, 0, jnp.where(i == nqb - 1, 2, 1))

    return pl.pallas_call(
        _na_kernel,
        grid=(nqb,),
        in_specs=[pl.BlockSpec((NA_QBLK, C_WIDTH), lambda i: (cb + i, qblk)),
                  kv_spec(1, 0), kv_spec(1, 1), kv_spec(1, 2), ctx_spec(1),
                  kv_spec(2, 0), kv_spec(2, 1), kv_spec(2, 2), ctx_spec(2),
                  pl.BlockSpec((None, C_HEADS, NA_QBLK, NA_KBLKS * NA_QBLK), lambda i: (case(i), 0, 0, 0))],
        out_specs=pl.BlockSpec((NA_QBLK, C_WIDTH), lambda i: (i, 0)),
        out_shape=jax.ShapeDtypeStruct((n - n_ctx, C_WIDTH), BF16),
        compiler_params=_cparams(("parallel",)),
        name="na_attention",
    )(p, p, p, p, p, p, p, p, p, bias)


def _ctx_attn_kernel(q_ref, k_ref, v_ref, o_ref):
    scale = HEAD_DIM ** -0.5
    for h in range(C_HEADS):
        cs = slice(h * HEAD_DIM, (h + 1) * HEAD_DIM)
        s = _dot_nt(q_ref[:, cs], k_ref[:, cs]) * scale
        pexp = jnp.exp(s - s.max(axis=-1, keepdims=True))
        o = _dot(pexp.astype(BF16), v_ref[:, cs]) / pexp.sum(axis=-1, keepdims=True)
        o_ref[:, cs] = o.astype(o_ref.dtype)


def na_ctx_attention(p, n_ctx):
    qblk = (A_COLS + B_COLS) // C_WIDTH
    return pl.pallas_call(
        _ctx_attn_kernel,
        grid=(1,),
        in_specs=[pl.BlockSpec((n_ctx, C_WIDTH), lambda i, s=s: (0, qblk + s)) for s in range(3)],
        out_specs=pl.BlockSpec((n_ctx, C_WIDTH), lambda i: (0, 0)),
        out_shape=jax.ShapeDtypeStruct((n_ctx, C_WIDTH), BF16),
        compiler_params=_cparams(("arbitrary",)),
        name="na_ctx_attention",
    )(p, p, p)


MOE_TM = 256
MOE_COMBINE_TR = 128


def _deint_kernel(w_ref, p_ref, o_ref):
    o_ref[...] = _dot(w_ref[...].astype(BF16), p_ref[...]).astype(o_ref.dtype)


def deinterleave_gate_up(w_gu, tk=1024):
    g, d, f2 = w_gu.shape
    src = np.concatenate([np.arange(0, f2, 2), np.arange(1, f2, 2)])
    perm = jnp.asarray((np.arange(f2)[:, None] == src[None, :]).astype(np.float32), dtype=BF16)
    return pl.pallas_call(
        _deint_kernel,
        grid=(g, d // tk),
        in_specs=[pl.BlockSpec((None, tk, f2), lambda e, i: (e, i, 0)),
                  pl.BlockSpec((f2, f2), lambda e, i: (0, 0))],
        out_specs=pl.BlockSpec((None, tk, f2), lambda e, i: (e, i, 0)),
        out_shape=jax.ShapeDtypeStruct((g, d, f2), BF16),
        compiler_params=_cparams(("parallel", "parallel")),
        name="deinterleave_gate_up",
    )(w_gu, perm)


def _row_copy(src_hbm, src_row, dst, dst_row, sem):
    return pltpu.make_async_copy(src_hbm.at[pl.ds(src_row, 1), :], dst.at[pl.ds(dst_row, 1), :], sem)


def _moe_dispatch_kernel(src_ref, prev_ref, h_hbm, xs_hbm, sem):
    t = pl.program_id(0)
    tm = src_ref.shape[1]

    def copies(idx_ref, tile, fn):
        def body(r, carry):
            fn(_row_copy(h_hbm, idx_ref[0, r], xs_hbm, tile * tm + r, sem))
            return carry
        lax.fori_loop(0, tm, body, 0)

    copies(src_ref, t, lambda cp: cp.start())

    @pl.when(t > 0)
    def _():
        copies(prev_ref, t - 1, lambda cp: cp.wait())

    @pl.when(t == pl.num_programs(0) - 1)
    def _():
        copies(src_ref, t, lambda cp: cp.wait())


def moe_dispatch(h_packed, src_rows, n_tiles):
    tm = MOE_TM
    src3 = src_rows.reshape(n_tiles, 1, tm)
    return pl.pallas_call(
        _moe_dispatch_kernel,
        grid=(n_tiles,),
        in_specs=[pl.BlockSpec((None, 1, tm), lambda t: (t, 0, 0), memory_space=pltpu.SMEM),
                  pl.BlockSpec((None, 1, tm), lambda t: (jnp.maximum(t - 1, 0), 0, 0), memory_space=pltpu.SMEM),
                  pl.BlockSpec(memory_space=pl.ANY)],
        out_specs=pl.BlockSpec(memory_space=pl.ANY),
        out_shape=jax.ShapeDtypeStruct((n_tiles * tm, h_packed.shape[1]), h_packed.dtype),
        scratch_shapes=[pltpu.SemaphoreType.DMA(())],
        compiler_params=_cparams(("arbitrary",)),
        name="moe_dispatch",
    )(src3, src3, h_packed)


def _moe_expert_kernel(te_ref, tv_ref, x_ref, wgu_ref, bgu_ref, wd_ref, bd_ref, y_ref):
    t = pl.program_id(0)

    @pl.when(tv_ref[t] != 0)
    def _():
        lo, hi = _unpack_bf16_pair(x_ref[...])
        half = lo.shape[1]
        gu = (_dot(lo.astype(BF16), wgu_ref[:half, :]) + _dot(hi.astype(BF16), wgu_ref[half:, :])
              + bgu_ref[...])
        f = gu.shape[1] // 2
        g = jnp.minimum(gu[:, :f], SWIGLU_LIMIT)
        u = jnp.clip(gu[:, f:], -SWIGLU_LIMIT, SWIGLU_LIMIT)
        act = (u + 1.0) * g * _sigmoid(g * SWIGLU_ALPHA)
        y = _dot(act.astype(BF16), wd_ref[...]) + bd_ref[...]
        y_ref[...] = _pack_bf16_pair(y[:, :half], y[:, half:])

    @pl.when(tv_ref[t] == 0)
    def _():
        y_ref[...] = jnp.zeros_like(y_ref)


def moe_experts(xs, tile_expert, tile_valid, wgu, bgu, wd, bd):
    tm = MOE_TM
    n_tiles = xs.shape[0] // tm
    half = xs.shape[1]
    d, f2 = wgu.shape[1], wgu.shape[2]
    grid_spec = pltpu.PrefetchScalarGridSpec(
        num_scalar_prefetch=2,
        grid=(n_tiles,),
        in_specs=[pl.BlockSpec((tm, half), lambda t, te, tv: (t, 0)),
                  pl.BlockSpec((None, d, f2), lambda t, te, tv: (te[t], 0, 0)),
                  pl.BlockSpec((None, 1, f2), lambda t, te, tv: (te[t], 0, 0)),
                  pl.BlockSpec((None, f2 // 2, d), lambda t, te, tv: (te[t], 0, 0)),
                  pl.BlockSpec((None, 1, d), lambda t, te, tv: (te[t], 0, 0))],
        out_specs=pl.BlockSpec((tm, half), lambda t, te, tv: (t, 0)),
    )
    return pl.pallas_call(
        _moe_expert_kernel,
        grid_spec=grid_spec,
        out_shape=jax.ShapeDtypeStruct(xs.shape, xs.dtype),
        compiler_params=_cparams(("arbitrary",)),
        name="moe_experts",
    )(tile_expert, tile_valid, xs, wgu, bgu, wd, bd)


def _moe_combine_kernel(dcur_ref, dnext_ref, wts_ref, x_ref, mod_ref, ys_hbm, o_ref, buf, sem, *, gi, n_ctx, tr):
    t = pl.program_id(0)
    slot = t % 2

    def copies(d_ref, s, fn):
        def body(i, carry):
            for k in range(TOP_K):
                fn(_row_copy(ys_hbm, d_ref[k, i], buf.at[s, k], i, sem.at[s]))
            return carry
        lax.fori_loop(0, tr, body, 0)

    @pl.when(t == 0)
    def _():
        copies(dcur_ref, 0, lambda cp: cp.start())

    @pl.when(t + 1 < pl.num_programs(0))
    def _():
        copies(dnext_ref, 1 - slot, lambda cp: cp.start())

    copies(dcur_ref, slot, lambda cp: cp.wait())

    half = buf.shape[-1]
    acc_lo = jnp.zeros((tr, half), F32)
    acc_hi = jnp.zeros((tr, half), F32)
    for k in range(TOP_K):
        lo, hi = _unpack_bf16_pair(buf[slot, k])
        w = wts_ref[:, k:k + 1]
        acc_lo = acc_lo + w * lo
        acc_hi = acc_hi + w * hi
    is_ctx = _row_is_ctx(t * tr, tr, n_ctx)
    gate = jnp.where(is_ctx, mod_ref[1, gi:gi + 1, :], mod_ref[0, gi:gi + 1, :])
    o_ref[:, :half] = x_ref[:, :half] + gate[:, :half] * acc_lo
    o_ref[:, half:] = x_ref[:, half:] + gate[:, half:] * acc_hi


def moe_combine(ys, dest_t, wts, x, mods, gi, n_ctx):
    tr = MOE_COMBINE_TR
    n, d = x.shape
    nt = n // tr
    return pl.pallas_call(
        functools.partial(_moe_combine_kernel, gi=gi, n_ctx=n_ctx, tr=tr),
        grid=(nt,),
        in_specs=[pl.BlockSpec((TOP_K, tr), lambda t: (0, t), memory_space=pltpu.SMEM),
                  pl.BlockSpec((TOP_K, tr), lambda t: (0, jnp.minimum(t + 1, nt - 1)), memory_space=pltpu.SMEM),
                  pl.BlockSpec((tr, TOP_K), lambda t: (t, 0)),
                  pl.BlockSpec((tr, d), lambda t: (t, 0)),
                  pl.BlockSpec((2, N_MOD, d), lambda t: (0, 0, 0)),
                  pl.BlockSpec(memory_space=pl.ANY)],
        out_specs=pl.BlockSpec((tr, d), lambda t: (t, 0)),
        out_shape=jax.ShapeDtypeStruct((n, d), F32),
        scratch_shapes=[pltpu.VMEM((2, TOP_K, tr, ys.shape[1]), ys.dtype), pltpu.SemaphoreType.DMA((2,))],
        compiler_params=_cparams(("arbitrary",)),
        name="moe_combine",
    )(dest_t, dest_t, wts, x, mods, ys)


def moe_routing_tables(idx, rank, counts, n_tiles):
    tm = MOE_TM
    n = idx.shape[0]
    cnt = counts[0].astype(jnp.int32)
    padded = (cnt + tm - 1) // tm * tm
    ends = jnp.cumsum(padded)
    offs = ends - padded
    onehot = idx[:, :, None] == jnp.arange(N_EXPERTS, dtype=jnp.int32)
    dest = jnp.sum(jnp.where(onehot, offs, 0), axis=-1) + rank
    token = jnp.broadcast_to(jnp.arange(n, dtype=jnp.int32)[:, None], dest.shape)
    src_rows = jnp.zeros((n_tiles * tm,), jnp.int32).at[dest.reshape(-1)].set(token.reshape(-1))
    tile_start = jnp.arange(n_tiles, dtype=jnp.int32) * tm
    tile_expert = jnp.minimum(jnp.sum(tile_start[:, None] >= ends[None, :], axis=1), N_EXPERTS - 1)
    tile_valid = (tile_start < ends[-1]).astype(jnp.int32)
    return dest.T, src_rows, tile_expert.astype(jnp.int32), tile_valid


def moe_ffn(x, norm_g, mods, router_w, router_b, wgu, bgu, wd, bd, n_ctx):
    n = x.shape[0]
    n_tiles = -(-(TOP_K * n + N_EXPERTS * (MOE_TM - 1)) // MOE_TM)
    h_packed, idx, wts, rank, counts = norm_router(x, norm_g, mods, 3, router_w, router_b, n_ctx)
    dest_t, src_rows, tile_expert, tile_valid = moe_routing_tables(idx, rank, counts, n_tiles)
    xs = moe_dispatch(h_packed, src_rows, n_tiles)
    ys = moe_experts(xs, tile_expert, tile_valid, wgu, bgu, wd, bd)
    return moe_combine(ys, dest_t, wts, x, mods, 5, n_ctx)


def _mm_tile(m):
    for tm in (1280, 1024, 640, 512, 256):
        if m % tm == 0:
            return tm
    raise ValueError(f"unsupported row count {m}")


def kernel(x, c, ctx, c_ctx, ada_w, ada_b, norm_mix, norm_ffn, w_in, w_out, hgrn_lb_logits, hgrn_norm,
           diff_lambda, diff_subln, na_rpb, router_w, router_b, expert_w_gu, expert_b_gu, expert_w_down,
           expert_b_down, final_norm_g):
    assert x.shape[0] == 1 and c.shape[0] == 1
    depth = ada_w.shape[0]
    n_ctx = ctx.shape[1]
    s_len = x.shape[1]
    n = n_ctx + s_len
    rows = s_len // GRID_W
    tm = _mm_tile(n)

    hall = jnp.concatenate([ctx[0], x[0]], axis=0)
    c8 = jnp.zeros((8, D_MODEL), F32).at[0].set(c[0]).at[1].set(c_ctx)
    mods_all = ada_mod(c8, ada_w, ada_b)[:, :2].reshape(depth, 2, N_MOD, D_MODEL)

    p_lb = jax.nn.softmax(hgrn_lb_logits.astype(F32), axis=0)
    lower_bounds = jnp.clip(jnp.cumsum(p_lb, axis=0) - p_lb[0], 0.0, 1.0)
    cos, sin = rope_tables(n, n_ctx)

    wgu_all = deinterleave_gate_up(expert_w_gu.reshape(depth * N_EXPERTS, D_MODEL, 2 * D_EXPERT))
    wgu_all = wgu_all.reshape(depth, N_EXPERTS, D_MODEL, 2 * D_EXPERT)
    bgu_all = jnp.concatenate([expert_b_gu[..., 0::2], expert_b_gu[..., 1::2]], axis=-1)
    tk = n // 26 if n % (26 * 128) == 0 else 256

    for layer in range(depth):
        lambda_init = 0.8 - 0.6 * math.exp(-0.3 * layer)
        mods = mods_all[layer]
        w_in_b = w_in[layer].astype(BF16)
        w_out_b = w_out[layer].astype(BF16)
        wd = expert_w_down[layer].astype(BF16)

        h = norm_mod(hall, norm_mix[layer], mods, 0, n_ctx)
        p = matmul(h, w_in_b, tm, 512, BF16)

        ya = hgrn_mixer(p, lower_bounds[layer], hgrn_norm[layer], n_ctx)
        qr, kr = rope_qk(p, cos, sin)
        yb_x = diff_attention(qr, kr, p, diff_lambda[layer], diff_subln[layer], lambda_init,
                              n_ctx, s_len, n, 256, tk)
        yb_c = diff_attention(qr, kr, p, diff_lambda[layer], diff_subln[layer], lambda_init,
                              0, n_ctx, n_ctx, n_ctx, n_ctx)
        yc_x = na_attention(p, na_bias_tables(na_rpb[layer], rows), n_ctx)
        yc_c = na_ctx_attention(p, n_ctx)
        y = jnp.concatenate([ya, jnp.concatenate([yb_c, yb_x], axis=0),
                             jnp.concatenate([yc_c, yc_x], axis=0)], axis=1)
        hall = matmul_residual(y, w_out_b, hall, mods, 2, n_ctx, tm, 512)
        hall = moe_ffn(hall, norm_ffn[layer], mods, router_w[layer], router_b[layer], wgu_all[layer],
                       bgu_all[layer][:, None, :], wd, expert_b_down[layer][:, None, :], n_ctx)

    return final_norm(hall, final_norm_g, n_ctx)[None]
```

```python
import functools
import math

import numpy as np
import jax
import jax.numpy as jnp
from jax import lax
from jax.experimental import pallas as pl
from jax.experimental.pallas import tpu as pltpu

F32 = jnp.float32
BF16 = jnp.bfloat16

D_MODEL = 4096
HEAD_DIM = 128
GRID_W = 64
A_HEADS = 12
A_WIDTH = A_HEADS * HEAD_DIM
B_HEADS = 5
B_WIDTH = B_HEADS * 2 * HEAD_DIM
C_HEADS = 10
C_WIDTH = C_HEADS * HEAD_DIM
A_COLS = 5 * A_WIDTH
B_COLS = 3 * B_WIDTH
C_COLS = 3 * C_WIDTH
IN_COLS = A_COLS + B_COLS + C_COLS
NA_KH = 8
NA_KW = 16
ROPE_BASE = 10000.0
N_EXPERTS = 32
TOP_K = 4
D_EXPERT = 512
SWIGLU_ALPHA = 1.702
SWIGLU_LIMIT = 7.0
N_MOD = 6
NORM_EPS = 1e-6
F_TINY = 1e-30
MASK_NEG = -1e30

V7X_VMEM_LIMIT_BYTES = 56 * 1024 * 1024

SCAN_CHUNK = 64
SCAN_SUB = 16
NA_QROWS = 4
NA_QBLK = NA_QROWS * GRID_W
NA_KBLKS = 3


def _cparams(sem):
    return pltpu.CompilerParams(dimension_semantics=sem, vmem_limit_bytes=V7X_VMEM_LIMIT_BYTES)


def _dot(a, b):
    return jnp.dot(a, b, preferred_element_type=F32)


def _dot_nt(a, b):
    return lax.dot_general(a, b, (((1,), (1,)), ((), ())), preferred_element_type=F32)


def _dot_tn(a, b):
    return lax.dot_general(a, b, (((0,), (0,)), ((), ())), preferred_element_type=F32)


def _sigmoid(x):
    return 1.0 / (1.0 + jnp.exp(-x))


def _row_is_ctx(row0, rows, n_ctx):
    return (row0 + lax.broadcasted_iota(jnp.int32, (rows, 1), 0)) < n_ctx


def _ada_kernel(c_ref, w_ref, b_ref, o_ref):
    c = c_ref[...]
    a = (c * _sigmoid(c)).astype(BF16)
    o_ref[...] = _dot(a, w_ref[...].astype(BF16)) + b_ref[...]


def ada_mod(c8, ada_w, ada_b, tn=512):
    depth, d, n = ada_w.shape
    return pl.pallas_call(
        _ada_kernel,
        grid=(depth, n // tn),
        in_specs=[pl.BlockSpec((8, d), lambda l, j: (0, 0)),
                  pl.BlockSpec((None, d, tn), lambda l, j: (l, 0, j)),
                  pl.BlockSpec((None, 1, tn), lambda l, j: (l, 0, j))],
        out_specs=pl.BlockSpec((None, 8, tn), lambda l, j: (l, 0, j)),
        out_shape=jax.ShapeDtypeStruct((depth, 8, n), F32),
        compiler_params=_cparams(("arbitrary", "arbitrary")),
        name="ada_mod",
    )(c8, ada_w, ada_b.reshape(depth, 1, n))


def _norm_mod(x, g, mod_ref, si, row0, n_ctx):
    ms = jnp.mean(x * x, axis=-1, keepdims=True)
    y = x * lax.rsqrt(ms + NORM_EPS) * g
    is_ctx = _row_is_ctx(row0, x.shape[0], n_ctx)
    shift = jnp.where(is_ctx, mod_ref[1, si:si + 1, :], mod_ref[0, si:si + 1, :])
    scale = jnp.where(is_ctx, mod_ref[1, si + 1:si + 2, :], mod_ref[0, si + 1:si + 2, :])
    return y * (1.0 + scale) + shift


def _norm_kernel(x_ref, g_ref, mod_ref, o_ref, *, si, n_ctx, tr):
    row0 = pl.program_id(0) * tr
    o_ref[...] = _norm_mod(x_ref[...], g_ref[...], mod_ref, si, row0, n_ctx).astype(o_ref.dtype)


def norm_mod(x, g, mods, si, n_ctx, tr=256):
    n, d = x.shape
    return pl.pallas_call(
        functools.partial(_norm_kernel, si=si, n_ctx=n_ctx, tr=tr),
        grid=(n // tr,),
        in_specs=[pl.BlockSpec((tr, d), lambda i: (i, 0)),
                  pl.BlockSpec((1, d), lambda i: (0, 0)),
                  pl.BlockSpec((2, N_MOD, d), lambda i: (0, 0, 0))],
        out_specs=pl.BlockSpec((tr, d), lambda i: (i, 0)),
        out_shape=jax.ShapeDtypeStruct((n, d), BF16),
        compiler_params=_cparams(("parallel",)),
        name="norm_mod",
    )(x, g.reshape(1, d), mods)


def _split_bf16(x):
    hi = x.astype(BF16)
    lo = (x - hi.astype(F32)).astype(BF16)
    return hi, lo


def _pack_bf16_pair(lo, hi):
    lo_b = lax.bitcast_convert_type(lo.astype(BF16).astype(F32), jnp.uint32) >> 16
    hi_b = lax.bitcast_convert_type(hi.astype(BF16).astype(F32), jnp.uint32) & jnp.uint32(0xFFFF0000)
    return hi_b | lo_b


def _unpack_bf16_pair(w):
    lo = lax.bitcast_convert_type(w << 16, F32)
    hi = lax.bitcast_convert_type(w & jnp.uint32(0xFFFF0000), F32)
    return lo, hi


def _norm_router_kernel(x_ref, g_ref, mod_ref, rw_ref, rb_ref, h_ref, idx_ref, wts_ref, rank_ref, cnt_ref,
                        carry_ref, *, si, n_ctx, tr):
    @pl.when(pl.program_id(0) == 0)
    def _():
        carry_ref[...] = jnp.zeros_like(carry_ref)

    row0 = pl.program_id(0) * tr
    h = _norm_mod(x_ref[...], g_ref[...], mod_ref, si, row0, n_ctx)
    half = h.shape[1] // 2
    h_ref[...] = _pack_bf16_pair(h[:, :half], h[:, half:])
    hh, hl = _split_bf16(h)
    wh, wl = _split_bf16(rw_ref[...])
    logits = _dot(hh, wh) + _dot(hh, wl) + _dot(hl, wh) + rb_ref[...]
    iota = lax.broadcasted_iota(jnp.int32, logits.shape, 1)
    work = logits
    vals, sels = [], []
    for k in range(TOP_K):
        m = jnp.max(work, axis=-1, keepdims=True)
        idx = jnp.min(jnp.where(work == m, iota, N_EXPERTS), axis=-1, keepdims=True)
        sel = iota == idx
        vals.append(m)
        sels.append(sel)
        idx_ref[:, k:k + 1] = idx
        work = jnp.where(sel, -jnp.inf, work)
    es = [jnp.exp(v - vals[0]) for v in vals]
    tot = es[0] + es[1] + es[2] + es[3]
    for k in range(TOP_K):
        wts_ref[:, k:k + 1] = es[k] / tot
    used = jnp.zeros_like(logits)
    for k in range(TOP_K):
        used = used + jnp.where(sels[k], 1.0, 0.0)
    r = lax.broadcasted_iota(jnp.int32, (tr, tr), 0)
    c = lax.broadcasted_iota(jnp.int32, (tr, tr), 1)
    before = jnp.where(c < r, 1.0, 0.0).astype(BF16)
    prefix = _dot(before, used.astype(BF16)) + carry_ref[...]
    for k in range(TOP_K):
        rank = jnp.sum(jnp.where(sels[k], prefix, 0.0), axis=-1, keepdims=True)
        rank_ref[:, k:k + 1] = rank.astype(jnp.int32)
    carry_ref[...] += jnp.sum(used, axis=0, keepdims=True)
    cnt_ref[...] = carry_ref[...]


def norm_router(x, g, mods, si, router_w, router_b, n_ctx, tr=256):
    n, d = x.shape
    col = lambda i: (i, 0)
    fixed = lambda i: (0, 0)
    return pl.pallas_call(
        functools.partial(_norm_router_kernel, si=si, n_ctx=n_ctx, tr=tr),
        grid=(n // tr,),
        in_specs=[pl.BlockSpec((tr, d), col),
                  pl.BlockSpec((1, d), fixed),
                  pl.BlockSpec((2, N_MOD, d), lambda i: (0, 0, 0)),
                  pl.BlockSpec((d, N_EXPERTS), fixed),
                  pl.BlockSpec((1, N_EXPERTS), fixed)],
        out_specs=[pl.BlockSpec((tr, d // 2), col),
                   pl.BlockSpec((tr, TOP_K), col),
                   pl.BlockSpec((tr, TOP_K), col),
                   pl.BlockSpec((tr, TOP_K), col),
                   pl.BlockSpec((1, N_EXPERTS), fixed)],
        out_shape=[jax.ShapeDtypeStruct((n, d // 2), jnp.uint32),
                   jax.ShapeDtypeStruct((n, TOP_K), jnp.int32),
                   jax.ShapeDtypeStruct((n, TOP_K), F32),
                   jax.ShapeDtypeStruct((n, TOP_K), jnp.int32),
                   jax.ShapeDtypeStruct((1, N_EXPERTS), F32)],
        scratch_shapes=[pltpu.VMEM((1, N_EXPERTS), F32)],
        compiler_params=_cparams(("arbitrary",)),
        name="norm_router",
    )(x, g.reshape(1, d), mods, router_w, router_b.reshape(1, N_EXPERTS))


def _final_norm_kernel(x_ref, g_ref, o_ref):
    x = x_ref[...]
    ms = jnp.mean(x * x, axis=-1, keepdims=True)
    o_ref[...] = x * lax.rsqrt(ms + NORM_EPS) * g_ref[...]


def final_norm(x, g, n_ctx, tr=256):
    n, d = x.shape
    skip = n_ctx // tr
    return pl.pallas_call(
        _final_norm_kernel,
        grid=((n - n_ctx) // tr,),
        in_specs=[pl.BlockSpec((tr, d), lambda i: (i + skip, 0)),
                  pl.BlockSpec((1, d), lambda i: (0, 0))],
        out_specs=pl.BlockSpec((tr, d), lambda i: (i, 0)),
        out_shape=jax.ShapeDtypeStruct((n - n_ctx, d), F32),
        compiler_params=_cparams(("parallel",)),
        name="final_norm",
    )(x, g.reshape(1, d))


def _mm_kernel(a_ref, b_ref, o_ref):
    o_ref[...] = _dot(a_ref[...], b_ref[...]).astype(o_ref.dtype)


def matmul(a, b, tm, tn, out_dtype):
    m, k = a.shape
    n = b.shape[1]
    return pl.pallas_call(
        _mm_kernel,
        grid=(m // tm, n // tn),
        in_specs=[pl.BlockSpec((tm, k), lambda i, j: (i, 0)),
                  pl.BlockSpec((k, tn), lambda i, j: (0, j))],
        out_specs=pl.BlockSpec((tm, tn), lambda i, j: (i, j)),
        out_shape=jax.ShapeDtypeStruct((m, n), out_dtype),
        compiler_params=_cparams(("parallel", "arbitrary")),
        name="matmul",
    )(a, b)


def _mm_res_kernel(*refs, gi, n_ctx, tm, widths):
    a_refs = refs[:len(widths)]
    b_ref, x_ref, mod_ref, o_ref = refs[len(widths):]
    row0 = pl.program_id(0) * tm
    is_ctx = _row_is_ctx(row0, tm, n_ctx)
    gate = jnp.where(is_ctx, mod_ref[1, gi:gi + 1, :], mod_ref[0, gi:gi + 1, :])
    acc = None
    k0 = 0
    for a_ref, w in zip(a_refs, widths):
        part = _dot(a_ref[...], b_ref[k0:k0 + w, :])
        acc = part if acc is None else acc + part
        k0 += w
    o_ref[...] = x_ref[...] + gate * acc


def matmul_residual(a_parts, b, x, mods, gi, n_ctx, tm, tn):
    m = x.shape[0]
    k, n = b.shape
    widths = tuple(a.shape[1] for a in a_parts)
    assert sum(widths) == k
    return pl.pallas_call(
        functools.partial(_mm_res_kernel, gi=gi, n_ctx=n_ctx, tm=tm, widths=widths),
        grid=(m // tm, n // tn),
        in_specs=[pl.BlockSpec((tm, w), lambda i, j: (i, 0)) for w in widths]
        + [pl.BlockSpec((k, tn), lambda i, j: (0, j)),
           pl.BlockSpec((tm, tn), lambda i, j: (i, j)),
           pl.BlockSpec((2, N_MOD, tn), lambda i, j: (0, 0, j))],
        out_specs=pl.BlockSpec((tm, tn), lambda i, j: (i, j)),
        out_shape=jax.ShapeDtypeStruct((m, n), F32),
        compiler_params=_cparams(("parallel", "arbitrary")),
        name="matmul_residual",
    )(*a_parts, b, x, mods)


def _hgrn_masks(c, sub, reverse):
    r = lax.broadcasted_iota(jnp.int32, (c, c), 0)
    s = lax.broadcasted_iota(jnp.int32, (c, c), 1)
    tri = (s >= r) if reverse else (s <= r)
    same_sub = (r // sub) == (s // sub)
    return tri, same_sub


def _hgrn_head(q_raw, f_pre, v, lb, st_in, *, reverse, sel):
    c, d = q_raw.shape
    sub = SCAN_SUB
    nsub = c // sub
    q = q_raw * _sigmoid(q_raw) * (HEAD_DIM ** -0.5)
    f = lb + (1.0 - lb) * _sigmoid(f_pre)
    log_f = jnp.log(jnp.maximum(f, F_TINY))
    k = (1.0 - lb) * _sigmoid(-f_pre)

    tri, same_sub = _hgrn_masks(c, sub, reverse)
    tri_b = jnp.where(tri, 1.0, 0.0).astype(BF16)
    hi = log_f.astype(BF16)
    r1 = log_f - hi.astype(F32)
    mid = r1.astype(BF16)
    lo = (r1 - mid.astype(F32)).astype(BF16)
    cum = _dot(tri_b, hi) + _dot(tri_b, mid) + _dot(tri_b, lo)

    order = list(range(nsub))
    if reverse:
        order = order[::-1]
    zero_row = jnp.zeros((1, d), F32)
    bnd_in, bnd_out = {}, {}
    prev = zero_row
    for i in order:
        last = i * sub if reverse else i * sub + sub - 1
        bnd_in[i] = prev
        bnd_out[i] = cum[last:last + 1, :]
        prev = bnd_out[i]
    cum_end = prev

    t_in_sub = lax.broadcasted_iota(jnp.int32, (sub, 1), 0)
    q_hat, u_rows, off_rows = [], [], []
    for i in range(nsub):
        sl = slice(i * sub, (i + 1) * sub)
        q_i, k_i, cum_i = q[sl], k[sl], cum[sl]
        q_hat_i = q_i * jnp.exp(cum_i - bnd_in[i])
        k_parts = []
        for j in range(nsub):
            earlier = (j > i) if reverse else (j < i)
            sj = slice(j * sub, (j + 1) * sub)
            if earlier:
                k_parts.append(k[sj] * jnp.exp(bnd_in[i] - cum[sj]))
            else:
                k_parts.append(jnp.zeros((sub, d), F32))
        k_til = jnp.concatenate(k_parts, axis=0).astype(BF16)
        off_rows.append(_dot_nt(q_hat_i.astype(BF16), k_til))
        u_parts = []
        for s in range(sub):
            r_lo, r_hi = (0, (s // 8 + 1) * 8) if reverse else (s // 8 * 8, sub)
            rows_t = t_in_sub[r_lo:r_hi]
            ok = (rows_t <= s) if reverse else (rows_t >= s)
            e = jnp.where(ok, cum_i[r_lo:r_hi] - cum_i[s:s + 1, :], MASK_NEG)
            part = [(q_i[r_lo:r_hi] * k_i[s:s + 1, :] * jnp.exp(e)).astype(BF16)]
            if r_lo > 0:
                part.insert(0, jnp.zeros((r_lo, d), BF16))
            if r_hi < sub:
                part.append(jnp.zeros((sub - r_hi, d), BF16))
            u_parts.append(part[0] if len(part) == 1 else jnp.concatenate(part, axis=0))
        u_rows.append(jnp.concatenate(u_parts, axis=1))
    u = jnp.concatenate(u_rows, axis=0)
    diag = _dot(u, sel)
    attn = jnp.concatenate(off_rows, axis=0) + jnp.where(same_sub, diag, 0.0)
    o = _dot(attn.astype(BF16), v.astype(BF16))
    o = o + _dot_nt((q * jnp.exp(cum)).astype(BF16), st_in.astype(BF16))
    k_end = (k * jnp.exp(cum_end - cum)).astype(BF16)
    st_out = st_in * jnp.exp(cum_end) + _dot_tn(v.astype(BF16), k_end)
    return o, st_out


def _hgrn_sel(c, sub):
    rows = np.arange(sub * HEAD_DIM) // HEAD_DIM
    cols = np.arange(c) % sub
    return jnp.asarray((rows[:, None] == cols[None, :]).astype(np.float32), dtype=BF16)


def _hgrn_fw_kernel(q_ref, f_ref, v_ref, lb_ref, sel_ref, o_ref, st_ref, *, hb):
    @pl.when(pl.program_id(1) == 0)
    def _():
        st_ref[...] = jnp.zeros_like(st_ref)

    sel = sel_ref[...]
    for h in range(hb):
        cs = slice(h * HEAD_DIM, (h + 1) * HEAD_DIM)
        o, st = _hgrn_head(q_ref[:, cs].astype(F32), f_ref[:, cs].astype(F32), v_ref[:, cs].astype(F32),
                           lb_ref[0:1, cs], st_ref[h], reverse=False, sel=sel)
        st_ref[h] = st
        o_ref[:, cs] = o


def _hgrn_bw_kernel(q_ref, f_ref, v_ref, g_ref, ofw_ref, lb_ref, ng_ref, sel_ref, y_ref, st_ref, *, hb):
    @pl.when(pl.program_id(1) == 0)
    def _():
        st_ref[...] = jnp.zeros_like(st_ref)

    sel = sel_ref[...]
    for h in range(hb):
        cs = slice(h * HEAD_DIM, (h + 1) * HEAD_DIM)
        o, st = _hgrn_head(q_ref[:, cs].astype(F32), f_ref[:, cs].astype(F32), v_ref[:, cs].astype(F32),
                           lb_ref[1:2, cs], st_ref[h], reverse=True, sel=sel)
        st_ref[h] = st
        o = o + ofw_ref[:, cs]
        ms = jnp.mean(o * o, axis=-1, keepdims=True)
        y = o * lax.rsqrt(ms + NORM_EPS) * ng_ref[...]
        g = g_ref[:, cs].astype(F32)
        y_ref[:, cs] = (y * (g * _sigmoid(g))).astype(y_ref.dtype)


def hgrn_mixer(p, lb, norm_g, n_ctx, hb=4):
    n = p.shape[0]
    c = SCAN_CHUNK
    nc = n // c
    nc_ctx = n_ctx // c
    ngroups = A_HEADS // hb
    w = hb * HEAD_DIM
    per = A_WIDTH // w
    sel = _hgrn_sel(c, SCAN_SUB)

    def fw_chunk(s):
        return s

    def bw_chunk(s):
        return jnp.where(s < nc_ctx, nc_ctx - 1 - s, nc - 1 - (s - nc_ctx))

    def col(section, chunk_of):
        return pl.BlockSpec((c, w), lambda g, s: (chunk_of(s), section * per + g))

    lb_spec = pl.BlockSpec((2, w), lambda g, s: (0, g))
    sel_spec = pl.BlockSpec(sel.shape, lambda g, s: (0, 0))
    scratch = [pltpu.VMEM((hb, HEAD_DIM, HEAD_DIM), F32)]
    o_fw = pl.pallas_call(
        functools.partial(_hgrn_fw_kernel, hb=hb),
        grid=(ngroups, nc),
        in_specs=[col(0, fw_chunk), col(1, fw_chunk), col(3, fw_chunk), lb_spec, sel_spec],
        out_specs=pl.BlockSpec((c, w), lambda g, s: (s, g)),
        out_shape=jax.ShapeDtypeStruct((n, A_WIDTH), F32),
        scratch_shapes=scratch,
        compiler_params=_cparams(("parallel", "arbitrary")),
        name="hgrn_fw",
    )(p, p, p, lb, sel)
    return pl.pallas_call(
        functools.partial(_hgrn_bw_kernel, hb=hb),
        grid=(ngroups, nc),
        in_specs=[col(0, bw_chunk), col(2, bw_chunk), col(3, bw_chunk), col(4, bw_chunk),
                  pl.BlockSpec((c, w), lambda g, s: (bw_chunk(s), g)),
                  lb_spec, pl.BlockSpec((1, HEAD_DIM), lambda g, s: (0, 0)), sel_spec],
        out_specs=pl.BlockSpec((c, w), lambda g, s: (bw_chunk(s), g)),
        out_shape=jax.ShapeDtypeStruct((n, A_WIDTH), BF16),
        scratch_shapes=scratch,
        compiler_params=_cparams(("parallel", "arbitrary")),
        name="hgrn_bw",
    )(p, p, p, p, o_fw, lb, norm_g.reshape(1, HEAD_DIM), sel)


def rope_tables(n, n_ctx):
    t = jnp.arange(n - n_ctx, dtype=jnp.int32)
    half = HEAD_DIM // 4
    inv_freq = ROPE_BASE ** (-jnp.arange(half, dtype=F32) / half)
    ang_r = (t // GRID_W).astype(F32)[:, None] * inv_freq
    ang_c = (t % GRID_W).astype(F32)[:, None] * inv_freq
    cos = jnp.concatenate([jnp.cos(ang_r)] * 2 + [jnp.cos(ang_c)] * 2, axis=-1)
    sin = jnp.concatenate([-jnp.sin(ang_r), jnp.sin(ang_r), -jnp.sin(ang_c), jnp.sin(ang_c)], axis=-1)
    cos = jnp.concatenate([jnp.ones((n_ctx, HEAD_DIM), F32), cos], axis=0)
    sin = jnp.concatenate([jnp.zeros((n_ctx, HEAD_DIM), F32), sin], axis=0)
    return cos, sin


def _rope_kernel(q_ref, k_ref, cos_ref, sin_ref, qo_ref, kt_ref):
    cos = cos_ref[...]
    sin = sin_ref[...]
    lane = lax.broadcasted_iota(jnp.int32, cos.shape, 1)
    low = (lane & (HEAD_DIM // 4)) == 0

    def rotate(x):
        partner = jnp.where(low, pltpu.roll(x, HEAD_DIM - HEAD_DIM // 4, 1), pltpu.roll(x, HEAD_DIM // 4, 1))
        return x * cos + partner * sin

    for h in range(2 * B_HEADS):
        cs = slice(h * HEAD_DIM, (h + 1) * HEAD_DIM)
        qo_ref[:, cs] = (rotate(q_ref[:, cs].astype(F32)) * (HEAD_DIM ** -0.5)).astype(qo_ref.dtype)
        kt_ref[cs, :] = rotate(k_ref[:, cs].astype(F32)).T.astype(kt_ref.dtype)


def rope_qk(p, cos, sin, tr=256):
    n = p.shape[0]
    qblk = A_COLS // B_WIDTH
    return pl.pallas_call(
        _rope_kernel,
        grid=(n // tr,),
        in_specs=[pl.BlockSpec((tr, B_WIDTH), lambda i: (i, qblk)),
                  pl.BlockSpec((tr, B_WIDTH), lambda i: (i, qblk + 1)),
                  pl.BlockSpec((tr, HEAD_DIM), lambda i: (i, 0)),
                  pl.BlockSpec((tr, HEAD_DIM), lambda i: (i, 0))],
        out_specs=[pl.BlockSpec((tr, B_WIDTH), lambda i: (i, 0)),
                   pl.BlockSpec((B_WIDTH, tr), lambda i: (0, i))],
        out_shape=[jax.ShapeDtypeStruct((n, B_WIDTH), BF16), jax.ShapeDtypeStruct((B_WIDTH, n), BF16)],
        compiler_params=_cparams(("parallel",)),
        name="rope_qk",
    )(p, p, cos, sin)


def _diff_attn_kernel(lam_ref, g_ref, q_ref, kt_ref, v_ref, *rest, lambda_init, nk, tk):
    o_ref, m_scr, l_scr, acc_scr, sa_scr, sb_scr = rest[-6:]
    tq = q_ref.shape[0]
    q = q_ref[...]
    q1 = q[:, :HEAD_DIM]
    q2 = q[:, HEAD_DIM:]
    m_scr[...] = jnp.full_like(m_scr, -jnp.inf)
    l_scr[...] = jnp.zeros_like(l_scr)
    acc_scr[...] = jnp.zeros_like(acc_scr)

    def scores(j, s_scr):
        off = pl.multiple_of(j * tk, 128)
        s_scr[:tq, :] = _dot(q1, kt_ref[:HEAD_DIM, pl.ds(off, tk)])
        s_scr[tq:, :] = _dot(q2, kt_ref[HEAD_DIM:, pl.ds(off, tk)])

    def accumulate(j, s_scr):
        off = pl.multiple_of(j * tk, tk)
        s = s_scr[...]
        m_prev = m_scr[...]
        m_new = jnp.maximum(m_prev, jnp.max(s, axis=-1, keepdims=True))
        alpha = jnp.exp(m_prev - m_new)
        p = jnp.exp(s - m_new)
        l_scr[...] = alpha * l_scr[...] + jnp.sum(p, axis=-1, keepdims=True)
        acc_scr[...] = alpha * acc_scr[...] + _dot(p.astype(BF16), v_ref[pl.ds(off, tk), :])
        m_scr[...] = m_new

    scores(0, sa_scr)

    def pair(jj, carry):
        j = 2 * jj
        scores(j + 1, sb_scr)
        accumulate(j, sa_scr)
        scores(jnp.minimum(j + 2, nk - 1), sa_scr)
        accumulate(j + 1, sb_scr)
        return carry

    lax.fori_loop(0, nk // 2, pair, 0)
    if nk % 2:
        accumulate(nk - 1, sa_scr)

    lp = lam_ref[...]
    lam = (jnp.exp(jnp.sum(lp[0:1] * lp[1:2], axis=-1, keepdims=True))
           - jnp.exp(jnp.sum(lp[2:3] * lp[3:4], axis=-1, keepdims=True)) + lambda_init)
    o = acc_scr[:tq] / l_scr[:tq] - lam * (acc_scr[tq:] / l_scr[tq:])
    ms = jnp.mean(o * o, axis=-1, keepdims=True)
    y = o * lax.rsqrt(ms + NORM_EPS) * g_ref[...] * (1.0 - lambda_init)
    o_ref[...] = y.astype(o_ref.dtype)


def diff_attention(q, kt, p, lam_p, subln_g, lambda_init, q_row0, n_q, n_kv, tq, tk, out_rows=None):
    hw = 2 * HEAD_DIM
    n = q.shape[0]
    nq, nk = n_q // tq, n_kv // tk
    qb0 = q_row0 // tq
    vblk = (A_COLS + 2 * B_WIDTH) // hw
    in_specs = [pl.BlockSpec((4, HEAD_DIM), lambda h, i: (0, 0)),
                pl.BlockSpec((1, hw), lambda h, i: (0, 0)),
                pl.BlockSpec((tq, hw), lambda h, i: (i + qb0, h)),
                pl.BlockSpec((hw, n_kv), lambda h, i: (h, 0)),
                pl.BlockSpec((n_kv, hw), lambda h, i: (0, vblk + h))]
    args = [lam_p, subln_g.reshape(1, hw), q, kt, p]
    aliases = {}
    if out_rows is not None:
        in_specs.append(pl.BlockSpec(memory_space=pl.ANY))
        args.append(out_rows)
        aliases = {len(args) - 1: 0}
    return pl.pallas_call(
        functools.partial(_diff_attn_kernel, lambda_init=lambda_init, nk=nk, tk=tk),
        grid=(B_HEADS, nq),
        in_specs=in_specs,
        out_specs=pl.BlockSpec((tq, hw), lambda h, i: (i + qb0, h)),
        out_shape=jax.ShapeDtypeStruct((n, B_WIDTH), BF16),
        scratch_shapes=[pltpu.VMEM((2 * tq, 1), F32), pltpu.VMEM((2 * tq, 1), F32),
                        pltpu.VMEM((2 * tq, hw), F32),
                        pltpu.VMEM((2 * tq, tk), F32), pltpu.VMEM((2 * tq, tk), F32)],
        input_output_aliases=aliases,
        compiler_params=_cparams(("parallel", "arbitrary")),
        name="diff_attention",
    )(*args)


def na_bias_tables(rpb, rows):
    krows = NA_KBLKS * NA_QROWS
    n_dr, n_dc = 2 * NA_KH - 1, 2 * NA_KW - 1
    qc = np.arange(GRID_W)[:, None]
    kc = np.arange(GRID_W)[None, :]
    cs = np.clip(qc - NA_KW // 2, 0, GRID_W - NA_KW)
    col_ok = (kc >= cs) & (kc < cs + NA_KW)
    dc = np.clip(kc - qc + NA_KW - 1, 0, n_dc - 1)
    onehot = (np.arange(n_dc)[:, None, None] == dc[None]).astype(np.float32).reshape(n_dc, GRID_W * GRID_W)
    toep = jnp.dot(rpb.astype(F32).reshape(C_HEADS * n_dr, n_dc), jnp.asarray(onehot),
                   precision=lax.Precision.HIGHEST).reshape(C_HEADS, n_dr, GRID_W, GRID_W)
    toep = jnp.where(jnp.asarray(col_ok)[None, None], toep, MASK_NEG)
    masked = jnp.full((C_HEADS, GRID_W, GRID_W), MASK_NEG, F32)
    cases = [(0, 0), (NA_QROWS, 0), (rows - NA_QROWS, rows - krows)]
    out = []
    for r0, start in cases:
        q_blocks = []
        for qr in range(NA_QROWS):
            r = r0 + qr
            rs = min(max(r - NA_KH // 2, 0), rows - NA_KH)
            k_blocks = []
            for kr in range(krows):
                kra = start + kr
                k_blocks.append(toep[:, kra - r + NA_KH - 1] if rs <= kra < rs + NA_KH else masked)
            q_blocks.append(jnp.concatenate(k_blocks, axis=-1))
        out.append(jnp.concatenate(q_blocks, axis=1))
    return jnp.stack(out)


def _na_kernel(q_ref, k0_ref, k1_ref, k2_ref, kc_ref, v0_ref, v1_ref, v2_ref, vc_ref, bias_ref, o_ref):
    scale = HEAD_DIM ** -0.5
    k_refs = (k0_ref, k1_ref, k2_ref)
    v_refs = (v0_ref, v1_ref, v2_ref, vc_ref)
    for h in range(C_HEADS):
        cs = slice(h * HEAD_DIM, (h + 1) * HEAD_DIM)
        q = q_ref[:, cs]
        s_parts = [_dot_nt(q, k_refs[b][:, cs]) * scale + bias_ref[h, :, b * NA_QBLK:(b + 1) * NA_QBLK]
                   for b in range(NA_KBLKS)]
        s_parts.append(_dot_nt(q, kc_ref[:, cs]) * scale)
        m = s_parts[0].max(axis=-1, keepdims=True)
        for s in s_parts[1:]:
            m = jnp.maximum(m, s.max(axis=-1, keepdims=True))
        l = jnp.zeros_like(m)
        acc = jnp.zeros((q.shape[0], HEAD_DIM), F32)
        for s, v_ref in zip(s_parts, v_refs):
            pexp = jnp.exp(s - m)
            l = l + pexp.sum(axis=-1, keepdims=True)
            acc = acc + _dot(pexp.astype(BF16), v_ref[:, cs])
        o_ref[:, cs] = (acc / l).astype(o_ref.dtype)


def na_attention(p, bias, n_ctx):
    n = p.shape[0]
    nqb = (n - n_ctx) // NA_QBLK
    cb = n_ctx // NA_QBLK
    qblk = (A_COLS + B_COLS) // C_WIDTH

    def kv_spec(section, b):
        return pl.BlockSpec((NA_QBLK, C_WIDTH),
                            lambda i: (cb + jnp.clip(i - 1, 0, nqb - NA_KBLKS) + b, qblk + section))

    def ctx_spec(section):
        return pl.BlockSpec((n_ctx, C_WIDTH), lambda i: (0, qblk + section))

    def case(i):
        return jnp.where(i == 0, 0, jnp.where(i == nqb - 1, 2, 1))

    return pl.pallas_call(
        _na_kernel,
        grid=(nqb,),
        in_specs=[pl.BlockSpec((NA_QBLK, C_WIDTH), lambda i: (cb + i, qblk)),
                  kv_spec(1, 0), kv_spec(1, 1), kv_spec(1, 2), ctx_spec(1),
                  kv_spec(2, 0), kv_spec(2, 1), kv_spec(2, 2), ctx_spec(2),
                  pl.BlockSpec((None, C_HEADS, NA_QBLK, NA_KBLKS * NA_QBLK), lambda i: (case(i), 0, 0, 0))],
        out_specs=pl.BlockSpec((NA_QBLK, C_WIDTH), lambda i: (cb + i, 0)),
        out_shape=jax.ShapeDtypeStruct((n, C_WIDTH), BF16),
        compiler_params=_cparams(("parallel",)),
        name="na_attention",
    )(p, p, p, p, p, p, p, p, p, bias)


def _ctx_attn_kernel(q_ref, k_ref, v_ref, rows_hbm, o_ref):
    del rows_hbm
    scale = HEAD_DIM ** -0.5
    for h in range(C_HEADS):
        cs = slice(h * HEAD_DIM, (h + 1) * HEAD_DIM)
        s = _dot_nt(q_ref[:, cs], k_ref[:, cs]) * scale
        pexp = jnp.exp(s - s.max(axis=-1, keepdims=True))
        o = _dot(pexp.astype(BF16), v_ref[:, cs]) / pexp.sum(axis=-1, keepdims=True)
        o_ref[:, cs] = o.astype(o_ref.dtype)


def na_ctx_attention(p, n_ctx, out_rows):
    qblk = (A_COLS + B_COLS) // C_WIDTH
    return pl.pallas_call(
        _ctx_attn_kernel,
        grid=(1,),
        in_specs=[pl.BlockSpec((n_ctx, C_WIDTH), lambda i, s=s: (0, qblk + s)) for s in range(3)]
        + [pl.BlockSpec(memory_space=pl.ANY)],
        out_specs=pl.BlockSpec((n_ctx, C_WIDTH), lambda i: (0, 0)),
        out_shape=jax.ShapeDtypeStruct(out_rows.shape, BF16),
        input_output_aliases={3: 0},
        compiler_params=_cparams(("arbitrary",)),
        name="na_ctx_attention",
    )(p, p, p, out_rows)


MOE_TM = 256
MOE_COMBINE_TR = 128


def _deint_kernel(w_ref, p_ref, o_ref):
    o_ref[...] = _dot(w_ref[...].astype(BF16), p_ref[...]).astype(o_ref.dtype)


def deinterleave_gate_up(w_gu, tk=1024):
    g, d, f2 = w_gu.shape
    src = np.concatenate([np.arange(0, f2, 2), np.arange(1, f2, 2)])
    perm = jnp.asarray((np.arange(f2)[:, None] == src[None, :]).astype(np.float32), dtype=BF16)
    return pl.pallas_call(
        _deint_kernel,
        grid=(g, d // tk),
        in_specs=[pl.BlockSpec((None, tk, f2), lambda e, i: (e, i, 0)),
                  pl.BlockSpec((f2, f2), lambda e, i: (0, 0))],
        out_specs=pl.BlockSpec((None, tk, f2), lambda e, i: (e, i, 0)),
        out_shape=jax.ShapeDtypeStruct((g, d, f2), BF16),
        compiler_params=_cparams(("parallel", "parallel")),
        name="deinterleave_gate_up",
    )(w_gu, perm)


def _row_copy(src_hbm, src_row, dst, dst_row, sem):
    return pltpu.make_async_copy(src_hbm.at[pl.ds(src_row, 1), :], dst.at[pl.ds(dst_row, 1), :], sem)


def _moe_expert_kernel(te_ref, tv_ref, src_ref, next_ref, h_hbm, wgu_ref, bgu_ref, wd_ref, bd_ref, y_ref, buf, sem):
    t = pl.program_id(0)
    slot = t % 2
    tm = src_ref.shape[1]

    def copies(idx_ref, s, fn):
        def body(r, carry):
            fn(_row_copy(h_hbm, idx_ref[0, r], buf.at[s], r, sem.at[s]))
            return carry
        lax.fori_loop(0, tm, body, 0)

    @pl.when(t == 0)
    def _():
        copies(src_ref, 0, lambda cp: cp.start())

    @pl.when(t + 1 < pl.num_programs(0))
    def _():
        copies(next_ref, 1 - slot, lambda cp: cp.start())

    copies(src_ref, slot, lambda cp: cp.wait())

    @pl.when(tv_ref[t] != 0)
    def _():
        lo, hi = _unpack_bf16_pair(buf[slot])
        half = lo.shape[1]
        gu = (_dot(lo.astype(BF16), wgu_ref[:half, :]) + _dot(hi.astype(BF16), wgu_ref[half:, :])
              + bgu_ref[...])
        f = gu.shape[1] // 2
        g = jnp.minimum(gu[:, :f], SWIGLU_LIMIT)
        u = jnp.clip(gu[:, f:], -SWIGLU_LIMIT, SWIGLU_LIMIT)
        act = (u + 1.0) * g * _sigmoid(g * SWIGLU_ALPHA)
        y = _dot(act.astype(BF16), wd_ref[...]) + bd_ref[...]
        y_ref[...] = _pack_bf16_pair(y[:, :half], y[:, half:])

    @pl.when(tv_ref[t] == 0)
    def _():
        y_ref[...] = jnp.zeros_like(y_ref)


def moe_experts(h_packed, src_rows, tile_expert, tile_valid, wgu, bgu, wd, bd):
    tm = MOE_TM
    n_tiles = tile_expert.shape[0]
    half = h_packed.shape[1]
    d, f2 = wgu.shape[1], wgu.shape[2]
    src3 = src_rows.reshape(n_tiles, 1, tm)
    grid_spec = pltpu.PrefetchScalarGridSpec(
        num_scalar_prefetch=2,
        grid=(n_tiles,),
        in_specs=[pl.BlockSpec((None, 1, tm), lambda t, te, tv: (t, 0, 0), memory_space=pltpu.SMEM),
                  pl.BlockSpec((None, 1, tm), lambda t, te, tv: (jnp.minimum(t + 1, n_tiles - 1), 0, 0),
                               memory_space=pltpu.SMEM),
                  pl.BlockSpec(memory_space=pl.ANY),
                  pl.BlockSpec((None, d, f2), lambda t, te, tv: (te[t], 0, 0)),
                  pl.BlockSpec((None, 1, f2), lambda t, te, tv: (te[t], 0, 0)),
                  pl.BlockSpec((None, f2 // 2, d), lambda t, te, tv: (te[t], 0, 0)),
                  pl.BlockSpec((None, 1, d), lambda t, te, tv: (te[t], 0, 0))],
        out_specs=pl.BlockSpec((tm, half), lambda t, te, tv: (t, 0)),
        scratch_shapes=[pltpu.VMEM((2, tm, half), h_packed.dtype), pltpu.SemaphoreType.DMA((2,))],
    )
    return pl.pallas_call(
        _moe_expert_kernel,
        grid_spec=grid_spec,
        out_shape=jax.ShapeDtypeStruct((n_tiles * tm, half), h_packed.dtype),
        compiler_params=_cparams(("arbitrary",)),
        name="moe_experts",
    )(tile_expert, tile_valid, src3, src3, h_packed, wgu, bgu, wd, bd)


def _moe_combine_kernel(dcur_ref, dnext_ref, wts_ref, x_ref, mod_ref, ys_hbm, o_ref, buf, sem, *, gi, n_ctx, tr):
    t = pl.program_id(0)
    slot = t % 2

    def copies(d_ref, s, fn):
        def body(i, carry):
            for k in range(TOP_K):
                fn(_row_copy(ys_hbm, d_ref[k, i], buf.at[s, k], i, sem.at[s]))
            return carry
        lax.fori_loop(0, tr, body, 0)

    @pl.when(t == 0)
    def _():
        copies(dcur_ref, 0, lambda cp: cp.start())

    @pl.when(t + 1 < pl.num_programs(0))
    def _():
        copies(dnext_ref, 1 - slot, lambda cp: cp.start())

    copies(dcur_ref, slot, lambda cp: cp.wait())

    half = buf.shape[-1]
    acc_lo = jnp.zeros((tr, half), F32)
    acc_hi = jnp.zeros((tr, half), F32)
    for k in range(TOP_K):
        lo, hi = _unpack_bf16_pair(buf[slot, k])
        w = wts_ref[:, k:k + 1]
        acc_lo = acc_lo + w * lo
        acc_hi = acc_hi + w * hi
    is_ctx = _row_is_ctx(t * tr, tr, n_ctx)
    gate = jnp.where(is_ctx, mod_ref[1, gi:gi + 1, :], mod_ref[0, gi:gi + 1, :])
    o_ref[:, :half] = x_ref[:, :half] + gate[:, :half] * acc_lo
    o_ref[:, half:] = x_ref[:, half:] + gate[:, half:] * acc_hi


def moe_combine(ys, dest_t, wts, x, mods, gi, n_ctx):
    tr = MOE_COMBINE_TR
    n, d = x.shape
    nt = n // tr
    return pl.pallas_call(
        functools.partial(_moe_combine_kernel, gi=gi, n_ctx=n_ctx, tr=tr),
        grid=(nt,),
        in_specs=[pl.BlockSpec((TOP_K, tr), lambda t: (0, t), memory_space=pltpu.SMEM),
                  pl.BlockSpec((TOP_K, tr), lambda t: (0, jnp.minimum(t + 1, nt - 1)), memory_space=pltpu.SMEM),
                  pl.BlockSpec((tr, TOP_K), lambda t: (t, 0)),
                  pl.BlockSpec((tr, d), lambda t: (t, 0)),
                  pl.BlockSpec((2, N_MOD, d), lambda t: (0, 0, 0)),
                  pl.BlockSpec(memory_space=pl.ANY)],
        out_specs=pl.BlockSpec((tr, d), lambda t: (t, 0)),
        out_shape=jax.ShapeDtypeStruct((n, d), F32),
        scratch_shapes=[pltpu.VMEM((2, TOP_K, tr, ys.shape[1]), ys.dtype), pltpu.SemaphoreType.DMA((2,))],
        compiler_params=_cparams(("arbitrary",)),
        name="moe_combine",
    )(dest_t, dest_t, wts, x, mods, ys)


def moe_routing_tables(idx, rank, counts, n_tiles):
    tm = MOE_TM
    n = idx.shape[0]
    cnt = counts[0].astype(jnp.int32)
    padded = (cnt + tm - 1) // tm * tm
    ends = jnp.cumsum(padded)
    offs = ends - padded
    onehot = idx[:, :, None] == jnp.arange(N_EXPERTS, dtype=jnp.int32)
    dest = jnp.sum(jnp.where(onehot, offs, 0), axis=-1) + rank
    token = jnp.broadcast_to(jnp.arange(n, dtype=jnp.int32)[:, None], dest.shape)
    src_rows = jnp.zeros((n_tiles * tm,), jnp.int32).at[dest.reshape(-1)].set(token.reshape(-1))
    tile_start = jnp.arange(n_tiles, dtype=jnp.int32) * tm
    tile_expert = jnp.minimum(jnp.sum(tile_start[:, None] >= ends[None, :], axis=1), N_EXPERTS - 1)
    tile_valid = (tile_start < ends[-1]).astype(jnp.int32)
    return dest.T, src_rows, tile_expert.astype(jnp.int32), tile_valid


def moe_ffn(x, norm_g, mods, router_w, router_b, wgu, bgu, wd, bd, n_ctx):
    n = x.shape[0]
    n_tiles = -(-(TOP_K * n + N_EXPERTS * (MOE_TM - 1)) // MOE_TM)
    h_packed, idx, wts, rank, counts = norm_router(x, norm_g, mods, 3, router_w, router_b, n_ctx)
    dest_t, src_rows, tile_expert, tile_valid = moe_routing_tables(idx, rank, counts, n_tiles)
    ys = moe_experts(h_packed, src_rows, tile_expert, tile_valid, wgu, bgu, wd, bd)
    return moe_combine(ys, dest_t, wts, x, mods, 5, n_ctx)


def _mm_tile(m):
    for tm in (1280, 1024, 640, 512, 256):
        if m % tm == 0:
            return tm
    raise ValueError(f"unsupported row count {m}")


def kernel(x, c, ctx, c_ctx, ada_w, ada_b, norm_mix, norm_ffn, w_in, w_out, hgrn_lb_logits, hgrn_norm,
           diff_lambda, diff_subln, na_rpb, router_w, router_b, expert_w_gu, expert_b_gu, expert_w_down,
           expert_b_down, final_norm_g):
    assert x.shape[0] == 1 and c.shape[0] == 1
    depth = ada_w.shape[0]
    n_ctx = ctx.shape[1]
    s_len = x.shape[1]
    n = n_ctx + s_len
    rows = s_len // GRID_W
    tm = _mm_tile(n)

    hall = jnp.concatenate([ctx[0], x[0]], axis=0)
    c8 = jnp.zeros((8, D_MODEL), F32).at[0].set(c[0]).at[1].set(c_ctx)
    mods_all = ada_mod(c8, ada_w, ada_b)[:, :2].reshape(depth, 2, N_MOD, D_MODEL)

    p_lb = jax.nn.softmax(hgrn_lb_logits.astype(F32), axis=0)
    lower_bounds = jnp.clip(jnp.cumsum(p_lb, axis=0) - p_lb[0], 0.0, 1.0)
    cos, sin = rope_tables(n, n_ctx)

    wgu_all = deinterleave_gate_up(expert_w_gu.reshape(depth * N_EXPERTS, D_MODEL, 2 * D_EXPERT))
    wgu_all = wgu_all.reshape(depth, N_EXPERTS, D_MODEL, 2 * D_EXPERT)
    bgu_all = jnp.concatenate([expert_b_gu[..., 0::2], expert_b_gu[..., 1::2]], axis=-1)
    tk = n // 26 if n % (26 * 128) == 0 else 256

    for layer in range(depth):
        lambda_init = 0.8 - 0.6 * math.exp(-0.3 * layer)
        mods = mods_all[layer]
        w_in_b = w_in[layer].astype(BF16)
        w_out_b = w_out[layer].astype(BF16)
        wd = expert_w_down[layer].astype(BF16)

        h = norm_mod(hall, norm_mix[layer], mods, 0, n_ctx)
        p = matmul(h, w_in_b, tm, 512, BF16)

        ya = hgrn_mixer(p, lower_bounds[layer], hgrn_norm[layer], n_ctx)
        qr, kt = rope_qk(p, cos, sin)
        yb = diff_attention(qr, kt, p, diff_lambda[layer], diff_subln[layer], lambda_init,
                            n_ctx, s_len, n, 256, tk)
        yb = diff_attention(qr, kt, p, diff_lambda[layer], diff_subln[layer], lambda_init,
                            0, n_ctx, n_ctx, n_ctx, n_ctx, out_rows=yb)
        yc = na_attention(p, na_bias_tables(na_rpb[layer], rows), n_ctx)
        yc = na_ctx_attention(p, n_ctx, yc)
        hall = matmul_residual([ya, yb, yc], w_out_b, hall, mods, 2, n_ctx, tm, 512)
        hall = moe_ffn(hall, norm_ffn[layer], mods, router_w[layer], router_b[layer], wgu_all[layer],
                       bgu_all[layer][:, None, :], wd, expert_b_down[layer][:, None, :], n_ctx)

    return final_norm(hall, final_norm_g, n_ctx)[None]
```

```python
import functools
import math

import numpy as np
import jax
import jax.numpy as jnp
from jax import lax
from jax.experimental import pallas as pl
from jax.experimental.pallas import tpu as pltpu

F32 = jnp.float32
BF16 = jnp.bfloat16

D_MODEL = 4096
HEAD_DIM = 128
GRID_W = 64
A_HEADS = 12
A_WIDTH = A_HEADS * HEAD_DIM
B_HEADS = 5
B_WIDTH = B_HEADS * 2 * HEAD_DIM
C_HEADS = 10
C_WIDTH = C_HEADS * HEAD_DIM
A_COLS = 5 * A_WIDTH
B_COLS = 3 * B_WIDTH
C_COLS = 3 * C_WIDTH
IN_COLS = A_COLS + B_COLS + C_COLS
NA_KH = 8
NA_KW = 16
ROPE_BASE = 10000.0
N_EXPERTS = 32
TOP_K = 4
D_EXPERT = 512
SWIGLU_ALPHA = 1.702
SWIGLU_LIMIT = 7.0
N_MOD = 6
NORM_EPS = 1e-6
F_TINY = 1e-30
MASK_NEG = -1e30
LOG2_E = 1.4426950408889634

V7X_VMEM_LIMIT_BYTES = 56 * 1024 * 1024

SCAN_CHUNK = 64
SCAN_SUB = 16
NA_QROWS = 4
NA_QBLK = NA_QROWS * GRID_W
NA_KBLKS = 3


def _cparams(sem):
    return pltpu.CompilerParams(dimension_semantics=sem, vmem_limit_bytes=V7X_VMEM_LIMIT_BYTES)


def _dot(a, b):
    return jnp.dot(a, b, preferred_element_type=F32)


def _dot_nt(a, b):
    return lax.dot_general(a, b, (((1,), (1,)), ((), ())), preferred_element_type=F32)


def _dot_tn(a, b):
    return lax.dot_general(a, b, (((0,), (0,)), ((), ())), preferred_element_type=F32)


def _sigmoid(x):
    return 1.0 / (1.0 + jnp.exp(-x))


def _row_is_ctx(row0, rows, n_ctx):
    return (row0 + lax.broadcasted_iota(jnp.int32, (rows, 1), 0)) < n_ctx


def _ada_kernel(c_ref, w_ref, b_ref, o_ref):
    c = c_ref[...]
    a = (c * _sigmoid(c)).astype(BF16)
    o_ref[...] = _dot(a, w_ref[...].astype(BF16)) + b_ref[...]


def ada_mod(c8, ada_w, ada_b, tn=512):
    depth, d, n = ada_w.shape
    return pl.pallas_call(
        _ada_kernel,
        grid=(depth, n // tn),
        in_specs=[pl.BlockSpec((8, d), lambda l, j: (0, 0)),
                  pl.BlockSpec((None, d, tn), lambda l, j: (l, 0, j)),
                  pl.BlockSpec((None, 1, tn), lambda l, j: (l, 0, j))],
        out_specs=pl.BlockSpec((None, 8, tn), lambda l, j: (l, 0, j)),
        out_shape=jax.ShapeDtypeStruct((depth, 8, n), F32),
        compiler_params=_cparams(("arbitrary", "arbitrary")),
        name="ada_mod",
    )(c8, ada_w, ada_b.reshape(depth, 1, n))


def _norm_mod(x, g, mod_ref, si, row0, n_ctx):
    ms = jnp.mean(x * x, axis=-1, keepdims=True)
    y = x * lax.rsqrt(ms + NORM_EPS) * g
    is_ctx = _row_is_ctx(row0, x.shape[0], n_ctx)
    shift = jnp.where(is_ctx, mod_ref[1, si:si + 1, :], mod_ref[0, si:si + 1, :])
    scale = jnp.where(is_ctx, mod_ref[1, si + 1:si + 2, :], mod_ref[0, si + 1:si + 2, :])
    return y * (1.0 + scale) + shift


def _norm_kernel(x_ref, g_ref, mod_ref, o_ref, *, si, n_ctx, tr):
    row0 = pl.program_id(0) * tr
    o_ref[...] = _norm_mod(x_ref[...], g_ref[...], mod_ref, si, row0, n_ctx).astype(o_ref.dtype)


def norm_mod(x, g, mods, si, n_ctx, tr=256):
    n, d = x.shape
    return pl.pallas_call(
        functools.partial(_norm_kernel, si=si, n_ctx=n_ctx, tr=tr),
        grid=(n // tr,),
        in_specs=[pl.BlockSpec((tr, d), lambda i: (i, 0)),
                  pl.BlockSpec((1, d), lambda i: (0, 0)),
                  pl.BlockSpec((2, N_MOD, d), lambda i: (0, 0, 0))],
        out_specs=pl.BlockSpec((tr, d), lambda i: (i, 0)),
        out_shape=jax.ShapeDtypeStruct((n, d), BF16),
        compiler_params=_cparams(("parallel",)),
        name="norm_mod",
    )(x, g.reshape(1, d), mods)


def _split_bf16(x):
    hi = x.astype(BF16)
    lo = (x - hi.astype(F32)).astype(BF16)
    return hi, lo


def _pack_bf16_pair(lo, hi):
    lo_b = lax.bitcast_convert_type(lo.astype(BF16).astype(F32), jnp.uint32) >> 16
    hi_b = lax.bitcast_convert_type(hi.astype(BF16).astype(F32), jnp.uint32) & jnp.uint32(0xFFFF0000)
    return hi_b | lo_b


def _unpack_bf16_pair(w):
    lo = lax.bitcast_convert_type(w << 16, F32)
    hi = lax.bitcast_convert_type(w & jnp.uint32(0xFFFF0000), F32)
    return lo, hi


def _norm_router_kernel(x_ref, g_ref, mod_ref, rw_ref, rb_ref, h_ref, idx_ref, wts_ref, rank_ref, cnt_ref,
                        carry_ref, *, si, n_ctx, tr):
    @pl.when(pl.program_id(0) == 0)
    def _():
        carry_ref[...] = jnp.zeros_like(carry_ref)

    row0 = pl.program_id(0) * tr
    h = _norm_mod(x_ref[...], g_ref[...], mod_ref, si, row0, n_ctx)
    half = h.shape[1] // 2
    h_ref[...] = _pack_bf16_pair(h[:, :half], h[:, half:])
    hh, hl = _split_bf16(h)
    wh, wl = _split_bf16(rw_ref[...])
    logits = _dot(hh, wh) + _dot(hh, wl) + _dot(hl, wh) + rb_ref[...]
    iota = lax.broadcasted_iota(jnp.int32, logits.shape, 1)
    work = logits
    vals, sels = [], []
    for k in range(TOP_K):
        m = jnp.max(work, axis=-1, keepdims=True)
        idx = jnp.min(jnp.where(work == m, iota, N_EXPERTS), axis=-1, keepdims=True)
        sel = iota == idx
        vals.append(m)
        sels.append(sel)
        idx_ref[:, k:k + 1] = idx
        work = jnp.where(sel, -jnp.inf, work)
    es = [jnp.exp(v - vals[0]) for v in vals]
    tot = es[0] + es[1] + es[2] + es[3]
    for k in range(TOP_K):
        wts_ref[:, k:k + 1] = es[k] / tot
    used = jnp.zeros_like(logits)
    for k in range(TOP_K):
        used = used + jnp.where(sels[k], 1.0, 0.0)
    r = lax.broadcasted_iota(jnp.int32, (tr, tr), 0)
    c = lax.broadcasted_iota(jnp.int32, (tr, tr), 1)
    before = jnp.where(c < r, 1.0, 0.0).astype(BF16)
    prefix = _dot(before, used.astype(BF16)) + carry_ref[...]
    for k in range(TOP_K):
        rank = jnp.sum(jnp.where(sels[k], prefix, 0.0), axis=-1, keepdims=True)
        rank_ref[:, k:k + 1] = rank.astype(jnp.int32)
    carry_ref[...] += jnp.sum(used, axis=0, keepdims=True)
    cnt_ref[...] = carry_ref[...]


def norm_router(x, g, mods, si, router_w, router_b, n_ctx, tr=256):
    n, d = x.shape
    col = lambda i: (i, 0)
    fixed = lambda i: (0, 0)
    return pl.pallas_call(
        functools.partial(_norm_router_kernel, si=si, n_ctx=n_ctx, tr=tr),
        grid=(n // tr,),
        in_specs=[pl.BlockSpec((tr, d), col),
                  pl.BlockSpec((1, d), fixed),
                  pl.BlockSpec((2, N_MOD, d), lambda i: (0, 0, 0)),
                  pl.BlockSpec((d, N_EXPERTS), fixed),
                  pl.BlockSpec((1, N_EXPERTS), fixed)],
        out_specs=[pl.BlockSpec((tr, d // 2), col),
                   pl.BlockSpec((tr, TOP_K), col),
                   pl.BlockSpec((tr, TOP_K), col),
                   pl.BlockSpec((tr, TOP_K), col),
                   pl.BlockSpec((1, N_EXPERTS), fixed)],
        out_shape=[jax.ShapeDtypeStruct((n, d // 2), jnp.uint32),
                   jax.ShapeDtypeStruct((n, TOP_K), jnp.int32),
                   jax.ShapeDtypeStruct((n, TOP_K), F32),
                   jax.ShapeDtypeStruct((n, TOP_K), jnp.int32),
                   jax.ShapeDtypeStruct((1, N_EXPERTS), F32)],
        scratch_shapes=[pltpu.VMEM((1, N_EXPERTS), F32)],
        compiler_params=_cparams(("arbitrary",)),
        name="norm_router",
    )(x, g.reshape(1, d), mods, router_w, router_b.reshape(1, N_EXPERTS))


def _final_norm_kernel(x_ref, g_ref, o_ref):
    x = x_ref[...]
    ms = jnp.mean(x * x, axis=-1, keepdims=True)
    o_ref[...] = x * lax.rsqrt(ms + NORM_EPS) * g_ref[...]


def final_norm(x, g, n_ctx, tr=256):
    n, d = x.shape
    skip = n_ctx // tr
    return pl.pallas_call(
        _final_norm_kernel,
        grid=((n - n_ctx) // tr,),
        in_specs=[pl.BlockSpec((tr, d), lambda i: (i + skip, 0)),
                  pl.BlockSpec((1, d), lambda i: (0, 0))],
        out_specs=pl.BlockSpec((tr, d), lambda i: (i, 0)),
        out_shape=jax.ShapeDtypeStruct((n - n_ctx, d), F32),
        compiler_params=_cparams(("parallel",)),
        name="final_norm",
    )(x, g.reshape(1, d))


def _mm_kernel(a_ref, b_ref, o_ref):
    o_ref[...] = _dot(a_ref[...], b_ref[...]).astype(o_ref.dtype)


def matmul(a, b, tm, tn, out_dtype):
    m, k = a.shape
    n = b.shape[1]
    return pl.pallas_call(
        _mm_kernel,
        grid=(m // tm, n // tn),
        in_specs=[pl.BlockSpec((tm, k), lambda i, j: (i, 0)),
                  pl.BlockSpec((k, tn), lambda i, j: (0, j))],
        out_specs=pl.BlockSpec((tm, tn), lambda i, j: (i, j)),
        out_shape=jax.ShapeDtypeStruct((m, n), out_dtype),
        compiler_params=_cparams(("parallel", "arbitrary")),
        name="matmul",
    )(a, b)


def _mm_res_kernel(*refs, gi, n_ctx, tm, widths):
    a_refs = refs[:len(widths)]
    b_ref, x_ref, mod_ref, o_ref = refs[len(widths):]
    row0 = pl.program_id(0) * tm
    is_ctx = _row_is_ctx(row0, tm, n_ctx)
    gate = jnp.where(is_ctx, mod_ref[1, gi:gi + 1, :], mod_ref[0, gi:gi + 1, :])
    acc = None
    k0 = 0
    for a_ref, w in zip(a_refs, widths):
        part = _dot(a_ref[...], b_ref[k0:k0 + w, :])
        acc = part if acc is None else acc + part
        k0 += w
    o_ref[...] = x_ref[...] + gate * acc


def matmul_residual(a_parts, b, x, mods, gi, n_ctx, tm, tn):
    m = x.shape[0]
    k, n = b.shape
    widths = tuple(a.shape[1] for a in a_parts)
    assert sum(widths) == k
    return pl.pallas_call(
        functools.partial(_mm_res_kernel, gi=gi, n_ctx=n_ctx, tm=tm, widths=widths),
        grid=(m // tm, n // tn),
        in_specs=[pl.BlockSpec((tm, w), lambda i, j: (i, 0)) for w in widths]
        + [pl.BlockSpec((k, tn), lambda i, j: (0, j)),
           pl.BlockSpec((tm, tn), lambda i, j: (i, j)),
           pl.BlockSpec((2, N_MOD, tn), lambda i, j: (0, 0, j))],
        out_specs=pl.BlockSpec((tm, tn), lambda i, j: (i, j)),
        out_shape=jax.ShapeDtypeStruct((m, n), F32),
        compiler_params=_cparams(("parallel", "arbitrary")),
        name="matmul_residual",
    )(*a_parts, b, x, mods)


def _hgrn_masks(c, sub, reverse):
    r = lax.broadcasted_iota(jnp.int32, (c, c), 0)
    s = lax.broadcasted_iota(jnp.int32, (c, c), 1)
    tri = (s >= r) if reverse else (s <= r)
    same_sub = (r // sub) == (s // sub)
    return tri, same_sub


def _hgrn_head(q_raw, f_pre, v, lb, st_in, *, reverse, sel, tri_b, same_sub):
    c, d = q_raw.shape
    sub = SCAN_SUB
    nsub = c // sub
    q = q_raw * _sigmoid(q_raw) * (HEAD_DIM ** -0.5)
    f = lb + (1.0 - lb) * _sigmoid(f_pre)
    log_f = jnp.log2(jnp.maximum(f, F_TINY))
    k = (1.0 - lb) * _sigmoid(-f_pre)

    hi = log_f.astype(BF16)
    r1 = log_f - hi.astype(F32)
    mid = r1.astype(BF16)
    lo = (r1 - mid.astype(F32)).astype(BF16)
    cum = _dot(tri_b, hi) + _dot(tri_b, mid) + _dot(tri_b, lo)
    cum_k = cum - jnp.log2(k)

    order = list(range(nsub))
    if reverse:
        order = order[::-1]
    zero_row = jnp.zeros((1, d), F32)
    bnd_in, bnd_out = {}, {}
    prev = zero_row
    for i in order:
        last = i * sub if reverse else i * sub + sub - 1
        bnd_in[i] = prev
        bnd_out[i] = cum[last:last + 1, :]
        prev = bnd_out[i]
    cum_end = prev

    t_in_sub = lax.broadcasted_iota(jnp.int32, (sub, 1), 0)
    q_hat, u_rows, off_rows = [], [], []
    for i in range(nsub):
        sl = slice(i * sub, (i + 1) * sub)
        q_i, cum_i, cum_k_i = q[sl], cum[sl], cum_k[sl]
        q_hat_i = q_i * jnp.exp2(cum_i - bnd_in[i])
        k_parts = []
        for j in range(nsub):
            earlier = (j > i) if reverse else (j < i)
            sj = slice(j * sub, (j + 1) * sub)
            if earlier:
                k_parts.append(jnp.exp2(bnd_in[i] - cum_k[sj]))
            else:
                k_parts.append(jnp.zeros((sub, d), F32))
        k_til = jnp.concatenate(k_parts, axis=0).astype(BF16)
        off_rows.append(_dot_nt(q_hat_i.astype(BF16), k_til))
        u_parts = []
        for s in range(sub):
            groups = []
            for g8 in range(sub // 8):
                rows = slice(g8 * 8, g8 * 8 + 8)
                if g8 == s // 8:
                    rows_t = t_in_sub[rows]
                    ok = (rows_t <= s) if reverse else (rows_t >= s)
                    e = jnp.where(ok, cum_i[rows] - cum_k_i[s:s + 1, :], MASK_NEG)
                elif (g8 < s // 8) == reverse:
                    e = cum_i[rows] - cum_k_i[s:s + 1, :]
                else:
                    groups.append(jnp.zeros((8, d), BF16))
                    continue
                groups.append((q_i[rows] * jnp.exp2(e)).astype(BF16))
            u_parts.append(jnp.concatenate(groups, axis=0))
        u_rows.append(jnp.concatenate(u_parts, axis=1))
    u = jnp.concatenate(u_rows, axis=0)
    diag = _dot(u, sel)
    attn = jnp.concatenate(off_rows, axis=0) + jnp.where(same_sub, diag, 0.0)
    v_b = v.astype(BF16)
    o = _dot(attn.astype(BF16), v_b)
    o = o + _dot_nt((q * jnp.exp2(cum)).astype(BF16), st_in.astype(BF16))
    k_end = jnp.exp2(cum_end - cum_k).astype(BF16)
    st_out = st_in * jnp.exp2(cum_end) + _dot_tn(v_b, k_end)
    return o, st_out


def _hgrn_sel(c, sub):
    rows = np.arange(sub * HEAD_DIM) // HEAD_DIM
    cols = np.arange(c) % sub
    return jnp.asarray((rows[:, None] == cols[None, :]).astype(np.float32), dtype=BF16)


def _hgrn_fw_kernel(q_ref, f_ref, v_ref, lb_ref, sel_ref, o_ref, st_ref, *, hb):
    @pl.when(pl.program_id(1) == 0)
    def _():
        st_ref[...] = jnp.zeros_like(st_ref)

    sel = sel_ref[...]
    tri, same_sub = _hgrn_masks(q_ref.shape[0], SCAN_SUB, False)
    tri_b = jnp.where(tri, 1.0, 0.0).astype(BF16)
    for h in range(hb):
        cs = slice(h * HEAD_DIM, (h + 1) * HEAD_DIM)
        o, st = _hgrn_head(q_ref[:, cs].astype(F32), f_ref[:, cs].astype(F32), v_ref[:, cs].astype(F32),
                           lb_ref[0:1, cs], st_ref[h], reverse=False, sel=sel, tri_b=tri_b, same_sub=same_sub)
        st_ref[h] = st
        o_ref[:, cs] = o


def _hgrn_bw_kernel(q_ref, f_ref, v_ref, g_ref, ofw_ref, lb_ref, ng_ref, sel_ref, y_ref, st_ref, *, hb):
    @pl.when(pl.program_id(1) == 0)
    def _():
        st_ref[...] = jnp.zeros_like(st_ref)

    sel = sel_ref[...]
    tri, same_sub = _hgrn_masks(q_ref.shape[0], SCAN_SUB, True)
    tri_b = jnp.where(tri, 1.0, 0.0).astype(BF16)
    for h in range(hb):
        cs = slice(h * HEAD_DIM, (h + 1) * HEAD_DIM)
        o, st = _hgrn_head(q_ref[:, cs].astype(F32), f_ref[:, cs].astype(F32), v_ref[:, cs].astype(F32),
                           lb_ref[1:2, cs], st_ref[h], reverse=True, sel=sel, tri_b=tri_b, same_sub=same_sub)
        st_ref[h] = st
        o = o + ofw_ref[:, cs]
        ms = jnp.mean(o * o, axis=-1, keepdims=True)
        y = o * lax.rsqrt(ms + NORM_EPS) * ng_ref[...]
        g = g_ref[:, cs].astype(F32)
        y_ref[:, cs] = (y * (g * _sigmoid(g))).astype(y_ref.dtype)


def hgrn_mixer(p, lb, norm_g, n_ctx, hb=6):
    n = p.shape[0]
    c = SCAN_CHUNK
    nc = n // c
    nc_ctx = n_ctx // c
    ngroups = A_HEADS // hb
    w = hb * HEAD_DIM
    per = A_WIDTH // w
    sel = _hgrn_sel(c, SCAN_SUB)

    def fw_chunk(s):
        return s

    def bw_chunk(s):
        return jnp.where(s < nc_ctx, nc_ctx - 1 - s, nc - 1 - (s - nc_ctx))

    def col(section, chunk_of):
        return pl.BlockSpec((c, w), lambda g, s: (chunk_of(s), section * per + g))

    lb_spec = pl.BlockSpec((2, w), lambda g, s: (0, g))
    sel_spec = pl.BlockSpec(sel.shape, lambda g, s: (0, 0))
    scratch = [pltpu.VMEM((hb, HEAD_DIM, HEAD_DIM), F32)]
    o_fw = pl.pallas_call(
        functools.partial(_hgrn_fw_kernel, hb=hb),
        grid=(ngroups, nc),
        in_specs=[col(0, fw_chunk), col(1, fw_chunk), col(3, fw_chunk), lb_spec, sel_spec],
        out_specs=pl.BlockSpec((c, w), lambda g, s: (s, g)),
        out_shape=jax.ShapeDtypeStruct((n, A_WIDTH), F32),
        scratch_shapes=scratch,
        compiler_params=_cparams(("parallel", "arbitrary")),
        name="hgrn_fw",
    )(p, p, p, lb, sel)
    return pl.pallas_call(
        functools.partial(_hgrn_bw_kernel, hb=hb),
        grid=(ngroups, nc),
        in_specs=[col(0, bw_chunk), col(2, bw_chunk), col(3, bw_chunk), col(4, bw_chunk),
                  pl.BlockSpec((c, w), lambda g, s: (bw_chunk(s), g)),
                  lb_spec, pl.BlockSpec((1, HEAD_DIM), lambda g, s: (0, 0)), sel_spec],
        out_specs=pl.BlockSpec((c, w), lambda g, s: (bw_chunk(s), g)),
        out_shape=jax.ShapeDtypeStruct((n, A_WIDTH), BF16),
        scratch_shapes=scratch,
        compiler_params=_cparams(("parallel", "arbitrary")),
        name="hgrn_bw",
    )(p, p, p, p, o_fw, lb, norm_g.reshape(1, HEAD_DIM), sel)


def rope_tables(n, n_ctx):
    t = jnp.arange(n - n_ctx, dtype=jnp.int32)
    half = HEAD_DIM // 4
    inv_freq = ROPE_BASE ** (-jnp.arange(half, dtype=F32) / half)
    ang_r = (t // GRID_W).astype(F32)[:, None] * inv_freq
    ang_c = (t % GRID_W).astype(F32)[:, None] * inv_freq
    cos = jnp.concatenate([jnp.cos(ang_r)] * 2 + [jnp.cos(ang_c)] * 2, axis=-1)
    sin = jnp.concatenate([-jnp.sin(ang_r), jnp.sin(ang_r), -jnp.sin(ang_c), jnp.sin(ang_c)], axis=-1)
    cos = jnp.concatenate([jnp.ones((n_ctx, HEAD_DIM), F32), cos], axis=0)
    sin = jnp.concatenate([jnp.zeros((n_ctx, HEAD_DIM), F32), sin], axis=0)
    return cos, sin


def _rope_kernel(q_ref, k_ref, cos_ref, sin_ref, qo_ref, kt_ref):
    cos = cos_ref[...]
    sin = sin_ref[...]
    lane = lax.broadcasted_iota(jnp.int32, cos.shape, 1)
    low = (lane & (HEAD_DIM // 4)) == 0

    def rotate(x):
        partner = jnp.where(low, pltpu.roll(x, HEAD_DIM - HEAD_DIM // 4, 1), pltpu.roll(x, HEAD_DIM // 4, 1))
        return x * cos + partner * sin

    for h in range(2 * B_HEADS):
        cs = slice(h * HEAD_DIM, (h + 1) * HEAD_DIM)
        qo_ref[:, cs] = (rotate(q_ref[:, cs].astype(F32)) * (LOG2_E * HEAD_DIM ** -0.5)).astype(qo_ref.dtype)
        kt_ref[cs, :] = rotate(k_ref[:, cs].astype(F32)).T.astype(kt_ref.dtype)


def rope_qk(p, cos, sin, tr=256):
    n = p.shape[0]
    qblk = A_COLS // B_WIDTH
    return pl.pallas_call(
        _rope_kernel,
        grid=(n // tr,),
        in_specs=[pl.BlockSpec((tr, B_WIDTH), lambda i: (i, qblk)),
                  pl.BlockSpec((tr, B_WIDTH), lambda i: (i, qblk + 1)),
                  pl.BlockSpec((tr, HEAD_DIM), lambda i: (i, 0)),
                  pl.BlockSpec((tr, HEAD_DIM), lambda i: (i, 0))],
        out_specs=[pl.BlockSpec((tr, B_WIDTH), lambda i: (i, 0)),
                   pl.BlockSpec((B_WIDTH, tr), lambda i: (0, i))],
        out_shape=[jax.ShapeDtypeStruct((n, B_WIDTH), BF16), jax.ShapeDtypeStruct((B_WIDTH, n), BF16)],
        compiler_params=_cparams(("parallel",)),
        name="rope_qk",
    )(p, p, cos, sin)


def _diff_attn_kernel(lam_ref, g_ref, q_ref, kt_ref, v_ref, *rest, lambda_init, nk, tk):
    o_ref, m_scr, l_scr, acc_scr, sa_scr, sb_scr = rest[-6:]
    tq = q_ref.shape[0]
    q = q_ref[...]
    q1 = q[:, :HEAD_DIM]
    q2 = q[:, HEAD_DIM:]
    m_scr[...] = jnp.full_like(m_scr, -jnp.inf)
    l_scr[...] = jnp.zeros_like(l_scr)
    acc_scr[...] = jnp.zeros_like(acc_scr)

    def scores(j, s_scr):
        off = pl.multiple_of(j * tk, 128)
        s_scr[:tq, :] = _dot(q1, kt_ref[:HEAD_DIM, pl.ds(off, tk)])
        s_scr[tq:, :] = _dot(q2, kt_ref[HEAD_DIM:, pl.ds(off, tk)])

    def accumulate(j, s_scr):
        off = pl.multiple_of(j * tk, tk)
        s = s_scr[...]
        m_prev = m_scr[...]
        m_new = jnp.maximum(m_prev, jnp.max(s, axis=-1, keepdims=True))
        alpha = jnp.exp2(m_prev - m_new)
        p = jnp.exp2(s - m_new)
        l_scr[...] = alpha * l_scr[...] + jnp.sum(p, axis=-1, keepdims=True)
        acc_scr[...] = alpha * acc_scr[...] + _dot(p.astype(BF16), v_ref[pl.ds(off, tk), :])
        m_scr[...] = m_new

    scores(0, sa_scr)

    def pair(jj, carry):
        j = 2 * jj
        scores(j + 1, sb_scr)
        accumulate(j, sa_scr)
        scores(jnp.minimum(j + 2, nk - 1), sa_scr)
        accumulate(j + 1, sb_scr)
        return carry

    lax.fori_loop(0, nk // 2, pair, 0)
    if nk % 2:
        accumulate(nk - 1, sa_scr)

    lp = lam_ref[...]
    lam = (jnp.exp(jnp.sum(lp[0:1] * lp[1:2], axis=-1, keepdims=True))
           - jnp.exp(jnp.sum(lp[2:3] * lp[3:4], axis=-1, keepdims=True)) + lambda_init)
    o = acc_scr[:tq] / l_scr[:tq] - lam * (acc_scr[tq:] / l_scr[tq:])
    ms = jnp.mean(o * o, axis=-1, keepdims=True)
    y = o * lax.rsqrt(ms + NORM_EPS) * g_ref[...] * (1.0 - lambda_init)
    o_ref[...] = y.astype(o_ref.dtype)


def diff_attention(q, kt, p, lam_p, subln_g, lambda_init, q_row0, n_q, n_kv, tq, tk, out_rows=None):
    hw = 2 * HEAD_DIM
    n = q.shape[0]
    nq, nk = n_q // tq, n_kv // tk
    qb0 = q_row0 // tq
    vblk = (A_COLS + 2 * B_WIDTH) // hw
    in_specs = [pl.BlockSpec((4, HEAD_DIM), lambda h, i: (0, 0)),
                pl.BlockSpec((1, hw), lambda h, i: (0, 0)),
                pl.BlockSpec((tq, hw), lambda h, i: (i + qb0, h)),
                pl.BlockSpec((hw, n_kv), lambda h, i: (h, 0)),
                pl.BlockSpec((n_kv, hw), lambda h, i: (0, vblk + h))]
    args = [lam_p, subln_g.reshape(1, hw), q, kt, p]
    aliases = {}
    if out_rows is not None:
        in_specs.append(pl.BlockSpec(memory_space=pl.ANY))
        args.append(out_rows)
        aliases = {len(args) - 1: 0}
    return pl.pallas_call(
        functools.partial(_diff_attn_kernel, lambda_init=lambda_init, nk=nk, tk=tk),
        grid=(B_HEADS, nq),
        in_specs=in_specs,
        out_specs=pl.BlockSpec((tq, hw), lambda h, i: (i + qb0, h)),
        out_shape=jax.ShapeDtypeStruct((n, B_WIDTH), BF16),
        scratch_shapes=[pltpu.VMEM((2 * tq, 1), F32), pltpu.VMEM((2 * tq, 1), F32),
                        pltpu.VMEM((2 * tq, hw), F32),
                        pltpu.VMEM((2 * tq, tk), F32), pltpu.VMEM((2 * tq, tk), F32)],
        input_output_aliases=aliases,
        compiler_params=_cparams(("parallel", "arbitrary")),
        name="diff_attention",
    )(*args)


def na_bias_tables(rpb, rows):
    krows = NA_KBLKS * NA_QROWS
    n_dr, n_dc = 2 * NA_KH - 1, 2 * NA_KW - 1
    qc = np.arange(GRID_W)[:, None]
    kc = np.arange(GRID_W)[None, :]
    cs = np.clip(qc - NA_KW // 2, 0, GRID_W - NA_KW)
    col_ok = (kc >= cs) & (kc < cs + NA_KW)
    dc = np.clip(kc - qc + NA_KW - 1, 0, n_dc - 1)
    onehot = (np.arange(n_dc)[:, None, None] == dc[None]).astype(np.float32).reshape(n_dc, GRID_W * GRID_W)
    toep = jnp.dot(rpb.astype(F32).reshape(C_HEADS * n_dr, n_dc), jnp.asarray(onehot),
                   precision=lax.Precision.HIGHEST).reshape(C_HEADS, n_dr, GRID_W, GRID_W)
    toep = jnp.where(jnp.asarray(col_ok)[None, None], toep, MASK_NEG)
    masked = jnp.full((C_HEADS, GRID_W, GRID_W), MASK_NEG, F32)
    cases = [(0, 0), (NA_QROWS, 0), (rows - NA_QROWS, rows - krows)]
    out = []
    for r0, start in cases:
        q_blocks = []
        for qr in range(NA_QROWS):
            r = r0 + qr
            rs = min(max(r - NA_KH // 2, 0), rows - NA_KH)
            k_blocks = []
            for kr in range(krows):
                kra = start + kr
                k_blocks.append(toep[:, kra - r + NA_KH - 1] if rs <= kra < rs + NA_KH else masked)
            q_blocks.append(jnp.concatenate(k_blocks, axis=-1))
        out.append(jnp.concatenate(q_blocks, axis=1))
    return jnp.stack(out)


def _na_kernel(q_ref, k0_ref, k1_ref, k2_ref, kc_ref, v0_ref, v1_ref, v2_ref, vc_ref, bias_ref, o_ref):
    scale = HEAD_DIM ** -0.5
    k_refs = (k0_ref, k1_ref, k2_ref)
    v_refs = (v0_ref, v1_ref, v2_ref, vc_ref)
    for h in range(C_HEADS):
        cs = slice(h * HEAD_DIM, (h + 1) * HEAD_DIM)
        q = q_ref[:, cs]
        s_parts = [_dot_nt(q, k_refs[b][:, cs]) * scale + bias_ref[h, :, b * NA_QBLK:(b + 1) * NA_QBLK]
                   for b in range(NA_KBLKS)]
        s_parts.append(_dot_nt(q, kc_ref[:, cs]) * scale)
        m = s_parts[0].max(axis=-1, keepdims=True)
        for s in s_parts[1:]:
            m = jnp.maximum(m, s.max(axis=-1, keepdims=True))
        l = jnp.zeros_like(m)
        acc = jnp.zeros((q.shape[0], HEAD_DIM), F32)
        for s, v_ref in zip(s_parts, v_refs):
            pexp = jnp.exp(s - m)
            l = l + pexp.sum(axis=-1, keepdims=True)
            acc = acc + _dot(pexp.astype(BF16), v_ref[:, cs])
        o_ref[:, cs] = (acc / l).astype(o_ref.dtype)


def na_attention(p, bias, n_ctx):
    n = p.shape[0]
    nqb = (n - n_ctx) // NA_QBLK
    cb = n_ctx // NA_QBLK
    qblk = (A_COLS + B_COLS) // C_WIDTH

    def kv_spec(section, b):
        return pl.BlockSpec((NA_QBLK, C_WIDTH),
                            lambda i: (cb + jnp.clip(i - 1, 0, nqb - NA_KBLKS) + b, qblk + section))

    def ctx_spec(section):
        return pl.BlockSpec((n_ctx, C_WIDTH), lambda i: (0, qblk + section))

    def case(i):
        return jnp.where(i == 0, 0, jnp.where(i == nqb - 1, 2, 1))

    return pl.pallas_call(
        _na_kernel,
        grid=(nqb,),
        in_specs=[pl.BlockSpec((NA_QBLK, C_WIDTH), lambda i: (cb + i, qblk)),
                  kv_spec(1, 0), kv_spec(1, 1), kv_spec(1, 2), ctx_spec(1),
                  kv_spec(2, 0), kv_spec(2, 1), kv_spec(2, 2), ctx_spec(2),
                  pl.BlockSpec((None, C_HEADS, NA_QBLK, NA_KBLKS * NA_QBLK), lambda i: (case(i), 0, 0, 0))],
        out_specs=pl.BlockSpec((NA_QBLK, C_WIDTH), lambda i: (cb + i, 0)),
        out_shape=jax.ShapeDtypeStruct((n, C_WIDTH), BF16),
        compiler_params=_cparams(("parallel",)),
        name="na_attention",
    )(p, p, p, p, p, p, p, p, p, bias)


def _ctx_attn_kernel(q_ref, k_ref, v_ref, rows_hbm, o_ref):
    del rows_hbm
    scale = HEAD_DIM ** -0.5
    for h in range(C_HEADS):
        cs = slice(h * HEAD_DIM, (h + 1) * HEAD_DIM)
        s = _dot_nt(q_ref[:, cs], k_ref[:, cs]) * scale
        pexp = jnp.exp(s - s.max(axis=-1, keepdims=True))
        o = _dot(pexp.astype(BF16), v_ref[:, cs]) / pexp.sum(axis=-1, keepdims=True)
        o_ref[:, cs] = o.astype(o_ref.dtype)


def na_ctx_attention(p, n_ctx, out_rows):
    qblk = (A_COLS + B_COLS) // C_WIDTH
    return pl.pallas_call(
        _ctx_attn_kernel,
        grid=(1,),
        in_specs=[pl.BlockSpec((n_ctx, C_WIDTH), lambda i, s=s: (0, qblk + s)) for s in range(3)]
        + [pl.BlockSpec(memory_space=pl.ANY)],
        out_specs=pl.BlockSpec((n_ctx, C_WIDTH), lambda i: (0, 0)),
        out_shape=jax.ShapeDtypeStruct(out_rows.shape, BF16),
        input_output_aliases={3: 0},
        compiler_params=_cparams(("arbitrary",)),
        name="na_ctx_attention",
    )(p, p, p, out_rows)


MOE_TM = 256
MOE_COMBINE_TR = 128


def _deint_kernel(w_ref, p_ref, o_ref):
    o_ref[...] = _dot(w_ref[...].astype(BF16), p_ref[...]).astype(o_ref.dtype)


def deinterleave_gate_up(w_gu, tk=1024):
    g, d, f2 = w_gu.shape
    src = np.concatenate([np.arange(0, f2, 2), np.arange(1, f2, 2)])
    perm = jnp.asarray((np.arange(f2)[:, None] == src[None, :]).astype(np.float32), dtype=BF16)
    return pl.pallas_call(
        _deint_kernel,
        grid=(g, d // tk),
        in_specs=[pl.BlockSpec((None, tk, f2), lambda e, i: (e, i, 0)),
                  pl.BlockSpec((f2, f2), lambda e, i: (0, 0))],
        out_specs=pl.BlockSpec((None, tk, f2), lambda e, i: (e, i, 0)),
        out_shape=jax.ShapeDtypeStruct((g, d, f2), BF16),
        compiler_params=_cparams(("parallel", "parallel")),
        name="deinterleave_gate_up",
    )(w_gu, perm)


def _row_copy(src_hbm, src_row, dst, dst_row, sem):
    return pltpu.make_async_copy(src_hbm.at[pl.ds(src_row, 1), :], dst.at[pl.ds(dst_row, 1), :], sem)


def _moe_expert_kernel(te_ref, tv_ref, src_ref, next_ref, h_hbm, wgu_ref, bgu_ref, wd_ref, bd_ref, y_ref, buf, sem):
    t = pl.program_id(0)
    slot = t % 2
    tm = src_ref.shape[1]

    def gather(idx_ref, s):
        def body(r, carry):
            _row_copy(h_hbm, idx_ref[0, r], buf.at[s], r, sem.at[s]).start()
            return carry
        lax.fori_loop(0, tm, body, 0, unroll=8)

    @pl.when(t == 0)
    def _():
        gather(src_ref, 0)

    @pl.when(t + 1 < pl.num_programs(0))
    def _():
        gather(next_ref, 1 - slot)

    pltpu.make_async_copy(h_hbm.at[pl.ds(0, tm), :], buf.at[slot], sem.at[slot]).wait()

    @pl.when(tv_ref[t] != 0)
    def _():
        lo, hi = _unpack_bf16_pair(buf[slot])
        half = lo.shape[1]
        gu = (_dot(lo.astype(BF16), wgu_ref[:half, :]) + _dot(hi.astype(BF16), wgu_ref[half:, :])
              + bgu_ref[...])
        f = gu.shape[1] // 2
        g = jnp.minimum(gu[:, :f], SWIGLU_LIMIT)
        u = jnp.clip(gu[:, f:], -SWIGLU_LIMIT, SWIGLU_LIMIT)
        act = (u + 1.0) * g * _sigmoid(g * SWIGLU_ALPHA)
        y = _dot(act.astype(BF16), wd_ref[...]) + bd_ref[...]
        y_ref[...] = _pack_bf16_pair(y[:, :half], y[:, half:])

    @pl.when(tv_ref[t] == 0)
    def _():
        y_ref[...] = jnp.zeros_like(y_ref)


def moe_experts(h_packed, src_rows, tile_expert, tile_valid, wgu, bgu, wd, bd):
    tm = MOE_TM
    n_tiles = tile_expert.shape[0]
    half = h_packed.shape[1]
    d, f2 = wgu.shape[1], wgu.shape[2]
    src3 = src_rows.reshape(n_tiles, 1, tm)
    grid_spec = pltpu.PrefetchScalarGridSpec(
        num_scalar_prefetch=2,
        grid=(n_tiles,),
        in_specs=[pl.BlockSpec((None, 1, tm), lambda t, te, tv: (t, 0, 0), memory_space=pltpu.SMEM),
                  pl.BlockSpec((None, 1, tm), lambda t, te, tv: (jnp.minimum(t + 1, n_tiles - 1), 0, 0),
                               memory_space=pltpu.SMEM),
                  pl.BlockSpec(memory_space=pl.ANY),
                  pl.BlockSpec((None, d, f2), lambda t, te, tv: (te[t], 0, 0)),
                  pl.BlockSpec((None, 1, f2), lambda t, te, tv: (te[t], 0, 0)),
                  pl.BlockSpec((None, f2 // 2, d), lambda t, te, tv: (te[t], 0, 0)),
                  pl.BlockSpec((None, 1, d), lambda t, te, tv: (te[t], 0, 0))],
        out_specs=pl.BlockSpec((tm, half), lambda t, te, tv: (t, 0)),
        scratch_shapes=[pltpu.VMEM((2, tm, half), h_packed.dtype), pltpu.SemaphoreType.DMA((2,))],
    )
    return pl.pallas_call(
        _moe_expert_kernel,
        grid_spec=grid_spec,
        out_shape=jax.ShapeDtypeStruct((n_tiles * tm, half), h_packed.dtype),
        compiler_params=_cparams(("arbitrary",)),
        name="moe_experts",
    )(tile_expert, tile_valid, src3, src3, h_packed, wgu, bgu, wd, bd)


def _moe_combine_kernel(dcur_ref, dnext_ref, wts_ref, x_ref, mod_ref, ys_hbm, o_ref, buf, sem, *, gi, n_ctx, tr):
    t = pl.program_id(0)
    slot = t % 2

    def gather(d_ref, s):
        def body(i, carry):
            for k in range(TOP_K):
                _row_copy(ys_hbm, d_ref[k, i], buf.at[s, k], i, sem.at[s]).start()
            return carry
        lax.fori_loop(0, tr, body, 0, unroll=4)

    @pl.when(t == 0)
    def _():
        gather(dcur_ref, 0)

    @pl.when(t + 1 < pl.num_programs(0))
    def _():
        gather(dnext_ref, 1 - slot)

    for k in range(TOP_K):
        pltpu.make_async_copy(ys_hbm.at[pl.ds(0, tr), :], buf.at[slot, k], sem.at[slot]).wait()

    half = buf.shape[-1]
    acc_lo = jnp.zeros((tr, half), F32)
    acc_hi = jnp.zeros((tr, half), F32)
    for k in range(TOP_K):
        lo, hi = _unpack_bf16_pair(buf[slot, k])
        w = wts_ref[:, k:k + 1]
        acc_lo = acc_lo + w * lo
        acc_hi = acc_hi + w * hi
    is_ctx = _row_is_ctx(t * tr, tr, n_ctx)
    gate = jnp.where(is_ctx, mod_ref[1, gi:gi + 1, :], mod_ref[0, gi:gi + 1, :])
    o_ref[:, :half] = x_ref[:, :half] + gate[:, :half] * acc_lo
    o_ref[:, half:] = x_ref[:, half:] + gate[:, half:] * acc_hi


def moe_combine(ys, dest_t, wts, x, mods, gi, n_ctx):
    tr = MOE_COMBINE_TR
    n, d = x.shape
    nt = n // tr
    return pl.pallas_call(
        functools.partial(_moe_combine_kernel, gi=gi, n_ctx=n_ctx, tr=tr),
        grid=(nt,),
        in_specs=[pl.BlockSpec((TOP_K, tr), lambda t: (0, t), memory_space=pltpu.SMEM),
                  pl.BlockSpec((TOP_K, tr), lambda t: (0, jnp.minimum(t + 1, nt - 1)), memory_space=pltpu.SMEM),
                  pl.BlockSpec((tr, TOP_K), lambda t: (t, 0)),
                  pl.BlockSpec((tr, d), lambda t: (t, 0)),
                  pl.BlockSpec((2, N_MOD, d), lambda t: (0, 0, 0)),
                  pl.BlockSpec(memory_space=pl.ANY)],
        out_specs=pl.BlockSpec((tr, d), lambda t: (t, 0)),
        out_shape=jax.ShapeDtypeStruct((n, d), F32),
        scratch_shapes=[pltpu.VMEM((2, TOP_K, tr, ys.shape[1]), ys.dtype), pltpu.SemaphoreType.DMA((2,))],
        compiler_params=_cparams(("arbitrary",)),
        name="moe_combine",
    )(dest_t, dest_t, wts, x, mods, ys)


def moe_routing_tables(idx, rank, counts, n_tiles, expert_base):
    tm = MOE_TM
    n = idx.shape[0]
    cnt = counts[0].astype(jnp.int32)
    padded = (cnt + tm - 1) // tm * tm
    ends = jnp.cumsum(padded)
    offs = ends - padded
    onehot = idx[:, :, None] == jnp.arange(N_EXPERTS, dtype=jnp.int32)
    dest = jnp.sum(jnp.where(onehot, offs, 0), axis=-1) + rank
    token = jnp.broadcast_to(jnp.arange(n, dtype=jnp.int32)[:, None], dest.shape)
    src_rows = jnp.zeros((n_tiles * tm,), jnp.int32).at[dest.reshape(-1)].set(token.reshape(-1))
    tile_start = jnp.arange(n_tiles, dtype=jnp.int32) * tm
    tile_expert = jnp.minimum(jnp.sum(tile_start[:, None] >= ends[None, :], axis=1), N_EXPERTS - 1)
    tile_valid = (tile_start < ends[-1]).astype(jnp.int32)
    return dest.T, src_rows, tile_expert.astype(jnp.int32) + expert_base, tile_valid


def moe_ffn(x, norm_g, mods, router_w, router_b, wgu, bgu, wd, bd, expert_base, n_ctx):
    n = x.shape[0]
    n_tiles = -(-(TOP_K * n + N_EXPERTS * (MOE_TM - 1)) // MOE_TM)
    h_packed, idx, wts, rank, counts = norm_router(x, norm_g, mods, 3, router_w, router_b, n_ctx)
    dest_t, src_rows, tile_expert, tile_valid = moe_routing_tables(idx, rank, counts, n_tiles, expert_base)
    ys = moe_experts(h_packed, src_rows, tile_expert, tile_valid, wgu, bgu, wd, bd)
    return moe_combine(ys, dest_t, wts, x, mods, 5, n_ctx)


def _mm_tile(m):
    for tm in (1280, 1024, 640, 512, 256):
        if m % tm == 0:
            return tm
    raise ValueError(f"unsupported row count {m}")


def kernel(x, c, ctx, c_ctx, ada_w, ada_b, norm_mix, norm_ffn, w_in, w_out, hgrn_lb_logits, hgrn_norm,
           diff_lambda, diff_subln, na_rpb, router_w, router_b, expert_w_gu, expert_b_gu, expert_w_down,
           expert_b_down, final_norm_g):
    assert x.shape[0] == 1 and c.shape[0] == 1
    depth = ada_w.shape[0]
    n_ctx = ctx.shape[1]
    s_len = x.shape[1]
    n = n_ctx + s_len
    rows = s_len // GRID_W
    tm = _mm_tile(n)

    hall = jnp.concatenate([ctx[0], x[0]], axis=0)
    c8 = jnp.zeros((8, D_MODEL), F32).at[0].set(c[0]).at[1].set(c_ctx)
    mods_all = ada_mod(c8, ada_w, ada_b)[:, :2].reshape(depth, 2, N_MOD, D_MODEL)

    p_lb = jax.nn.softmax(hgrn_lb_logits.astype(F32), axis=0)
    lower_bounds = jnp.clip(jnp.cumsum(p_lb, axis=0) - p_lb[0], 0.0, 1.0)
    cos, sin = rope_tables(n, n_ctx)

    wgu_all = deinterleave_gate_up(expert_w_gu.reshape(depth * N_EXPERTS, D_MODEL, 2 * D_EXPERT))
    bgu_all = jnp.concatenate([expert_b_gu[..., 0::2], expert_b_gu[..., 1::2]], axis=-1)
    bgu_all = bgu_all.reshape(depth * N_EXPERTS, 1, 2 * D_EXPERT)
    wd_all = expert_w_down.reshape(depth * N_EXPERTS, D_EXPERT, D_MODEL).astype(BF16)
    bd_all = expert_b_down.reshape(depth * N_EXPERTS, 1, D_MODEL)
    tk = n // 13 if n % (13 * 256) == 0 else 256

    for layer in range(depth):
        lambda_init = 0.8 - 0.6 * math.exp(-0.3 * layer)
        mods = mods_all[layer]
        w_in_b = w_in[layer].astype(BF16)
        w_out_b = w_out[layer].astype(BF16)

        h = norm_mod(hall, norm_mix[layer], mods, 0, n_ctx)
        p = matmul(h, w_in_b, tm, 512, BF16)

        ya = hgrn_mixer(p, lower_bounds[layer], hgrn_norm[layer], n_ctx)
        qr, kt = rope_qk(p, cos, sin)
        yb = diff_attention(qr, kt, p, diff_lambda[layer], diff_subln[layer], lambda_init,
                            n_ctx, s_len, n, 256, tk)
        yb = diff_attention(qr, kt, p, diff_lambda[layer], diff_subln[layer], lambda_init,
                            0, n_ctx, n_ctx, n_ctx, n_ctx, out_rows=yb)
        yc = na_attention(p, na_bias_tables(na_rpb[layer], rows), n_ctx)
        yc = na_ctx_attention(p, n_ctx, yc)
        hall = matmul_residual([ya, yb, yc], w_out_b, hall, mods, 2, n_ctx, tm, 512)
        hall = moe_ffn(hall, norm_ffn[layer], mods, router_w[layer], router_b[layer], wgu_all, bgu_all,
                       wd_all, bd_all, layer * N_EXPERTS, n_ctx)

    return final_norm(hall, final_norm_g, n_ctx)[None]
```

```python
import functools
import math

import numpy as np
import jax
import jax.numpy as jnp
from jax import lax
from jax.experimental import pallas as pl
from jax.experimental.pallas import tpu as pltpu

F32 = jnp.float32
BF16 = jnp.bfloat16

D_MODEL = 4096
HEAD_DIM = 128
GRID_W = 64
A_HEADS = 12
A_WIDTH = A_HEADS * HEAD_DIM
B_HEADS = 5
B_WIDTH = B_HEADS * 2 * HEAD_DIM
C_HEADS = 10
C_WIDTH = C_HEADS * HEAD_DIM
A_COLS = 5 * A_WIDTH
B_COLS = 3 * B_WIDTH
C_COLS = 3 * C_WIDTH
IN_COLS = A_COLS + B_COLS + C_COLS
NA_KH = 8
NA_KW = 16
ROPE_BASE = 10000.0
N_EXPERTS = 32
TOP_K = 4
D_EXPERT = 512
SWIGLU_ALPHA = 1.702
SWIGLU_LIMIT = 7.0
N_MOD = 6
NORM_EPS = 1e-6
F_TINY = 1e-30
MASK_NEG = -1e30
LOG2_E = 1.4426950408889634

V7X_VMEM_LIMIT_BYTES = 56 * 1024 * 1024

SCAN_CHUNK = 64
SCAN_SUB = 16
NA_QROWS = 4
NA_QBLK = NA_QROWS * GRID_W
NA_KBLKS = 3


def _cparams(sem):
    return pltpu.CompilerParams(dimension_semantics=sem, vmem_limit_bytes=V7X_VMEM_LIMIT_BYTES)


def _dot(a, b):
    return jnp.dot(a, b, preferred_element_type=F32)


def _dot_nt(a, b):
    return lax.dot_general(a, b, (((1,), (1,)), ((), ())), preferred_element_type=F32)


def _dot_tn(a, b):
    return lax.dot_general(a, b, (((0,), (0,)), ((), ())), preferred_element_type=F32)


def _sigmoid(x):
    return 1.0 / (1.0 + jnp.exp(-x))


def _row_is_ctx(row0, rows, n_ctx):
    return (row0 + lax.broadcasted_iota(jnp.int32, (rows, 1), 0)) < n_ctx


def _ada_kernel(c_ref, w_ref, b_ref, o_ref):
    c = c_ref[...]
    a = (c * _sigmoid(c)).astype(BF16)
    o_ref[...] = _dot(a, w_ref[...].astype(BF16)) + b_ref[...]


def ada_mod(c8, ada_w, ada_b, tn=512):
    depth, d, n = ada_w.shape
    return pl.pallas_call(
        _ada_kernel,
        grid=(depth, n // tn),
        in_specs=[pl.BlockSpec((8, d), lambda l, j: (0, 0)),
                  pl.BlockSpec((None, d, tn), lambda l, j: (l, 0, j)),
                  pl.BlockSpec((None, 1, tn), lambda l, j: (l, 0, j))],
        out_specs=pl.BlockSpec((None, 8, tn), lambda l, j: (l, 0, j)),
        out_shape=jax.ShapeDtypeStruct((depth, 8, n), F32),
        compiler_params=_cparams(("arbitrary", "arbitrary")),
        name="ada_mod",
    )(c8, ada_w, ada_b.reshape(depth, 1, n))


def _norm_mod(x, g, mod_ref, si, row0, n_ctx):
    ms = jnp.mean(x * x, axis=-1, keepdims=True)
    y = x * lax.rsqrt(ms + NORM_EPS) * g
    is_ctx = _row_is_ctx(row0, x.shape[0], n_ctx)
    shift = jnp.where(is_ctx, mod_ref[1, si:si + 1, :], mod_ref[0, si:si + 1, :])
    scale = jnp.where(is_ctx, mod_ref[1, si + 1:si + 2, :], mod_ref[0, si + 1:si + 2, :])
    return y * (1.0 + scale) + shift


def _norm_kernel(x_ref, g_ref, mod_ref, o_ref, *, si, n_ctx, tr):
    row0 = pl.program_id(0) * tr
    o_ref[...] = _norm_mod(x_ref[...], g_ref[...], mod_ref, si, row0, n_ctx).astype(o_ref.dtype)


def norm_mod(x, g, mods, si, n_ctx, tr=256):
    n, d = x.shape
    return pl.pallas_call(
        functools.partial(_norm_kernel, si=si, n_ctx=n_ctx, tr=tr),
        grid=(n // tr,),
        in_specs=[pl.BlockSpec((tr, d), lambda i: (i, 0)),
                  pl.BlockSpec((1, d), lambda i: (0, 0)),
                  pl.BlockSpec((2, N_MOD, d), lambda i: (0, 0, 0))],
        out_specs=pl.BlockSpec((tr, d), lambda i: (i, 0)),
        out_shape=jax.ShapeDtypeStruct((n, d), BF16),
        compiler_params=_cparams(("parallel",)),
        name="norm_mod",
    )(x, g.reshape(1, d), mods)


def _split_bf16(x):
    hi = x.astype(BF16)
    lo = (x - hi.astype(F32)).astype(BF16)
    return hi, lo


def _pack_bf16_pair(lo, hi):
    lo_b = lax.bitcast_convert_type(lo.astype(BF16).astype(F32), jnp.uint32) >> 16
    hi_b = lax.bitcast_convert_type(hi.astype(BF16).astype(F32), jnp.uint32) & jnp.uint32(0xFFFF0000)
    return hi_b | lo_b


def _unpack_bf16_pair(w):
    lo = lax.bitcast_convert_type(w << 16, F32)
    hi = lax.bitcast_convert_type(w & jnp.uint32(0xFFFF0000), F32)
    return lo, hi


def _norm_router_kernel(x_ref, g_ref, mod_ref, rw_ref, rb_ref, h_ref, idx_ref, wts_ref, rank_ref, cnt_ref,
                        carry_ref, *, si, n_ctx, tr):
    @pl.when(pl.program_id(0) == 0)
    def _():
        carry_ref[...] = jnp.zeros_like(carry_ref)

    row0 = pl.program_id(0) * tr
    h = _norm_mod(x_ref[...], g_ref[...], mod_ref, si, row0, n_ctx)
    half = h.shape[1] // 2
    h_ref[...] = _pack_bf16_pair(h[:, :half], h[:, half:])
    hh, hl = _split_bf16(h)
    wh, wl = _split_bf16(rw_ref[...])
    logits = _dot(hh, wh) + _dot(hh, wl) + _dot(hl, wh) + rb_ref[...]
    iota = lax.broadcasted_iota(jnp.int32, logits.shape, 1)
    work = logits
    vals, sels = [], []
    for k in range(TOP_K):
        m = jnp.max(work, axis=-1, keepdims=True)
        idx = jnp.min(jnp.where(work == m, iota, N_EXPERTS), axis=-1, keepdims=True)
        sel = iota == idx
        vals.append(m)
        sels.append(sel)
        idx_ref[:, k:k + 1] = idx
        work = jnp.where(sel, -jnp.inf, work)
    es = [jnp.exp(v - vals[0]) for v in vals]
    tot = es[0] + es[1] + es[2] + es[3]
    for k in range(TOP_K):
        wts_ref[:, k:k + 1] = es[k] / tot
    used = jnp.zeros_like(logits)
    for k in range(TOP_K):
        used = used + jnp.where(sels[k], 1.0, 0.0)
    r = lax.broadcasted_iota(jnp.int32, (tr, tr), 0)
    c = lax.broadcasted_iota(jnp.int32, (tr, tr), 1)
    before = jnp.where(c < r, 1.0, 0.0).astype(BF16)
    prefix = _dot(before, used.astype(BF16)) + carry_ref[...]
    for k in range(TOP_K):
        rank = jnp.sum(jnp.where(sels[k], prefix, 0.0), axis=-1, keepdims=True)
        rank_ref[:, k:k + 1] = rank.astype(jnp.int32)
    carry_ref[...] += jnp.sum(used, axis=0, keepdims=True)
    cnt_ref[...] = carry_ref[...]


def norm_router(x, g, mods, si, router_w, router_b, n_ctx, tr=256):
    n, d = x.shape
    col = lambda i: (i, 0)
    fixed = lambda i: (0, 0)
    return pl.pallas_call(
        functools.partial(_norm_router_kernel, si=si, n_ctx=n_ctx, tr=tr),
        grid=(n // tr,),
        in_specs=[pl.BlockSpec((tr, d), col),
                  pl.BlockSpec((1, d), fixed),
                  pl.BlockSpec((2, N_MOD, d), lambda i: (0, 0, 0)),
                  pl.BlockSpec((d, N_EXPERTS), fixed),
                  pl.BlockSpec((1, N_EXPERTS), fixed)],
        out_specs=[pl.BlockSpec((tr, d // 2), col),
                   pl.BlockSpec((tr, TOP_K), col),
                   pl.BlockSpec((tr, TOP_K), col),
                   pl.BlockSpec((tr, TOP_K), col),
                   pl.BlockSpec((1, N_EXPERTS), fixed)],
        out_shape=[jax.ShapeDtypeStruct((n, d // 2), jnp.uint32),
                   jax.ShapeDtypeStruct((n, TOP_K), jnp.int32),
                   jax.ShapeDtypeStruct((n, TOP_K), F32),
                   jax.ShapeDtypeStruct((n, TOP_K), jnp.int32),
                   jax.ShapeDtypeStruct((1, N_EXPERTS), F32)],
        scratch_shapes=[pltpu.VMEM((1, N_EXPERTS), F32)],
        compiler_params=_cparams(("arbitrary",)),
        name="norm_router",
    )(x, g.reshape(1, d), mods, router_w, router_b.reshape(1, N_EXPERTS))


def _final_norm_kernel(x_ref, g_ref, o_ref):
    x = x_ref[...]
    ms = jnp.mean(x * x, axis=-1, keepdims=True)
    o_ref[...] = x * lax.rsqrt(ms + NORM_EPS) * g_ref[...]


def final_norm(x, g, n_ctx, tr=256):
    n, d = x.shape
    skip = n_ctx // tr
    return pl.pallas_call(
        _final_norm_kernel,
        grid=((n - n_ctx) // tr,),
        in_specs=[pl.BlockSpec((tr, d), lambda i: (i + skip, 0)),
                  pl.BlockSpec((1, d), lambda i: (0, 0))],
        out_specs=pl.BlockSpec((tr, d), lambda i: (i, 0)),
        out_shape=jax.ShapeDtypeStruct((n - n_ctx, d), F32),
        compiler_params=_cparams(("parallel",)),
        name="final_norm",
    )(x, g.reshape(1, d))


def _mm_kernel(a_ref, b_ref, o_ref):
    o_ref[...] = _dot(a_ref[...], b_ref[...]).astype(o_ref.dtype)


def matmul(a, b, tm, tn, out_dtype):
    m, k = a.shape
    n = b.shape[1]
    return pl.pallas_call(
        _mm_kernel,
        grid=(m // tm, n // tn),
        in_specs=[pl.BlockSpec((tm, k), lambda i, j: (i, 0)),
                  pl.BlockSpec((k, tn), lambda i, j: (0, j))],
        out_specs=pl.BlockSpec((tm, tn), lambda i, j: (i, j)),
        out_shape=jax.ShapeDtypeStruct((m, n), out_dtype),
        compiler_params=_cparams(("parallel", "arbitrary")),
        name="matmul",
    )(a, b)


def _mm_res_kernel(*refs, gi, n_ctx, tm, widths):
    a_refs = refs[:len(widths)]
    b_ref, x_ref, mod_ref, o_ref = refs[len(widths):]
    row0 = pl.program_id(0) * tm
    is_ctx = _row_is_ctx(row0, tm, n_ctx)
    gate = jnp.where(is_ctx, mod_ref[1, gi:gi + 1, :], mod_ref[0, gi:gi + 1, :])
    acc = None
    k0 = 0
    for a_ref, w in zip(a_refs, widths):
        part = _dot(a_ref[...], b_ref[k0:k0 + w, :])
        acc = part if acc is None else acc + part
        k0 += w
    o_ref[...] = x_ref[...] + gate * acc


def matmul_residual(a_parts, b, x, mods, gi, n_ctx, tm, tn):
    m = x.shape[0]
    k, n = b.shape
    widths = tuple(a.shape[1] for a in a_parts)
    assert sum(widths) == k
    return pl.pallas_call(
        functools.partial(_mm_res_kernel, gi=gi, n_ctx=n_ctx, tm=tm, widths=widths),
        grid=(m // tm, n // tn),
        in_specs=[pl.BlockSpec((tm, w), lambda i, j: (i, 0)) for w in widths]
        + [pl.BlockSpec((k, tn), lambda i, j: (0, j)),
           pl.BlockSpec((tm, tn), lambda i, j: (i, j)),
           pl.BlockSpec((2, N_MOD, tn), lambda i, j: (0, 0, j))],
        out_specs=pl.BlockSpec((tm, tn), lambda i, j: (i, j)),
        out_shape=jax.ShapeDtypeStruct((m, n), F32),
        compiler_params=_cparams(("parallel", "arbitrary")),
        name="matmul_residual",
    )(*a_parts, b, x, mods)


def _hgrn_masks(c, sub, reverse):
    r = lax.broadcasted_iota(jnp.int32, (c, c), 0)
    s = lax.broadcasted_iota(jnp.int32, (c, c), 1)
    tri = (s >= r) if reverse else (s <= r)
    same_sub = (r // sub) == (s // sub)
    return tri, same_sub


def _hgrn_head(q_raw, f_pre, v, lb, st_in, *, reverse, sel, tri_b, same_sub):
    c, d = q_raw.shape
    sub = SCAN_SUB
    nsub = c // sub
    q = q_raw * _sigmoid(q_raw) * (HEAD_DIM ** -0.5)
    f = lb + (1.0 - lb) * _sigmoid(f_pre)
    log_f = jnp.log2(jnp.maximum(f, F_TINY))
    k = (1.0 - lb) * _sigmoid(-f_pre)

    hi = log_f.astype(BF16)
    r1 = log_f - hi.astype(F32)
    mid = r1.astype(BF16)
    lo = (r1 - mid.astype(F32)).astype(BF16)
    yield
    cum = _dot(tri_b, hi) + _dot(tri_b, mid) + _dot(tri_b, lo)
    yield
    cum_k = cum - jnp.log2(k)

    order = list(range(nsub))
    if reverse:
        order = order[::-1]
    zero_row = jnp.zeros((1, d), F32)
    bnd_in, bnd_out = {}, {}
    prev = zero_row
    for i in order:
        last = i * sub if reverse else i * sub + sub - 1
        bnd_in[i] = prev
        bnd_out[i] = cum[last:last + 1, :]
        prev = bnd_out[i]
    cum_end = prev

    t_in_sub = lax.broadcasted_iota(jnp.int32, (sub, 1), 0)
    q_hat, u_rows, off_rows = [], [], []
    for i in range(nsub):
        sl = slice(i * sub, (i + 1) * sub)
        q_i, cum_i, cum_k_i = q[sl], cum[sl], cum_k[sl]
        q_hat_i = q_i * jnp.exp2(cum_i - bnd_in[i])
        k_parts = []
        for j in range(nsub):
            earlier = (j > i) if reverse else (j < i)
            sj = slice(j * sub, (j + 1) * sub)
            if earlier:
                k_parts.append(jnp.exp2(bnd_in[i] - cum_k[sj]))
            else:
                k_parts.append(jnp.zeros((sub, d), F32))
        k_til = jnp.concatenate(k_parts, axis=0).astype(BF16)
        off_rows.append(_dot_nt(q_hat_i.astype(BF16), k_til))
        yield
        u_parts = []
        for s in range(sub):
            groups = []
            for g8 in range(sub // 8):
                rows = slice(g8 * 8, g8 * 8 + 8)
                if g8 == s // 8:
                    rows_t = t_in_sub[rows]
                    ok = (rows_t <= s) if reverse else (rows_t >= s)
                    e = jnp.where(ok, cum_i[rows] - cum_k_i[s:s + 1, :], MASK_NEG)
                elif (g8 < s // 8) == reverse:
                    e = cum_i[rows] - cum_k_i[s:s + 1, :]
                else:
                    groups.append(jnp.zeros((8, d), BF16))
                    continue
                groups.append((q_i[rows] * jnp.exp2(e)).astype(BF16))
            u_parts.append(jnp.concatenate(groups, axis=0))
        u_rows.append(jnp.concatenate(u_parts, axis=1))
        yield
    u = jnp.concatenate(u_rows, axis=0)
    diag = _dot(u, sel)
    v_b = v.astype(BF16)
    k_end = jnp.exp2(cum_end - cum_k).astype(BF16)
    st_out = st_in * jnp.exp2(cum_end) + _dot_tn(v_b, k_end)
    o_state = _dot_nt((q * jnp.exp2(cum)).astype(BF16), st_in.astype(BF16))
    yield
    attn = jnp.concatenate(off_rows, axis=0) + jnp.where(same_sub, diag, 0.0)
    o = _dot(attn.astype(BF16), v_b) + o_state
    return o, st_out


def _run_interleaved(gens):
    results = [None] * len(gens)
    live = list(range(len(gens)))
    while live:
        for idx in list(live):
            try:
                next(gens[idx])
            except StopIteration as done:
                results[idx] = done.value
                live.remove(idx)
    return results


def _hgrn_sel(c, sub):
    rows = np.arange(sub * HEAD_DIM) // HEAD_DIM
    cols = np.arange(c) % sub
    return jnp.asarray((rows[:, None] == cols[None, :]).astype(np.float32), dtype=BF16)


def _hgrn_fw_kernel(q_ref, f_ref, v_ref, lb_ref, sel_ref, o_ref, st_ref, *, hb):
    @pl.when(pl.program_id(1) == 0)
    def _():
        st_ref[...] = jnp.zeros_like(st_ref)

    sel = sel_ref[...]
    tri, same_sub = _hgrn_masks(q_ref.shape[0], SCAN_SUB, False)
    tri_b = jnp.where(tri, 1.0, 0.0).astype(BF16)
    cols = [slice(h * HEAD_DIM, (h + 1) * HEAD_DIM) for h in range(hb)]
    heads = _run_interleaved([
        _hgrn_head(q_ref[:, cs].astype(F32), f_ref[:, cs].astype(F32), v_ref[:, cs].astype(F32),
                   lb_ref[0:1, cs], st_ref[h], reverse=False, sel=sel, tri_b=tri_b, same_sub=same_sub)
        for h, cs in enumerate(cols)])
    for h, (cs, (o, st)) in enumerate(zip(cols, heads)):
        st_ref[h] = st
        o_ref[:, cs] = o


def _hgrn_bw_kernel(q_ref, f_ref, v_ref, g_ref, ofw_ref, lb_ref, ng_ref, sel_ref, y_ref, st_ref, *, hb):
    @pl.when(pl.program_id(1) == 0)
    def _():
        st_ref[...] = jnp.zeros_like(st_ref)

    sel = sel_ref[...]
    tri, same_sub = _hgrn_masks(q_ref.shape[0], SCAN_SUB, True)
    tri_b = jnp.where(tri, 1.0, 0.0).astype(BF16)
    cols = [slice(h * HEAD_DIM, (h + 1) * HEAD_DIM) for h in range(hb)]
    heads = _run_interleaved([
        _hgrn_head(q_ref[:, cs].astype(F32), f_ref[:, cs].astype(F32), v_ref[:, cs].astype(F32),
                   lb_ref[1:2, cs], st_ref[h], reverse=True, sel=sel, tri_b=tri_b, same_sub=same_sub)
        for h, cs in enumerate(cols)])
    for h, (cs, (o, st)) in enumerate(zip(cols, heads)):
        st_ref[h] = st
        o = o + ofw_ref[:, cs]
        ms = jnp.mean(o * o, axis=-1, keepdims=True)
        y = o * lax.rsqrt(ms + NORM_EPS) * ng_ref[...]
        g = g_ref[:, cs].astype(F32)
        y_ref[:, cs] = (y * (g * _sigmoid(g))).astype(y_ref.dtype)


def hgrn_mixer(p, lb, norm_g, n_ctx, hb=6):
    n = p.shape[0]
    c = SCAN_CHUNK
    nc = n // c
    nc_ctx = n_ctx // c
    ngroups = A_HEADS // hb
    w = hb * HEAD_DIM
    per = A_WIDTH // w
    sel = _hgrn_sel(c, SCAN_SUB)

    def fw_chunk(s):
        return s

    def bw_chunk(s):
        return jnp.where(s < nc_ctx, nc_ctx - 1 - s, nc - 1 - (s - nc_ctx))

    def col(section, chunk_of):
        return pl.BlockSpec((c, w), lambda g, s: (chunk_of(s), section * per + g))

    lb_spec = pl.BlockSpec((2, w), lambda g, s: (0, g))
    sel_spec = pl.BlockSpec(sel.shape, lambda g, s: (0, 0))
    scratch = [pltpu.VMEM((hb, HEAD_DIM, HEAD_DIM), F32)]
    o_fw = pl.pallas_call(
        functools.partial(_hgrn_fw_kernel, hb=hb),
        grid=(ngroups, nc),
        in_specs=[col(0, fw_chunk), col(1, fw_chunk), col(3, fw_chunk), lb_spec, sel_spec],
        out_specs=pl.BlockSpec((c, w), lambda g, s: (s, g)),
        out_shape=jax.ShapeDtypeStruct((n, A_WIDTH), F32),
        scratch_shapes=scratch,
        compiler_params=_cparams(("parallel", "arbitrary")),
        name="hgrn_fw",
    )(p, p, p, lb, sel)
    return pl.pallas_call(
        functools.partial(_hgrn_bw_kernel, hb=hb),
        grid=(ngroups, nc),
        in_specs=[col(0, bw_chunk), col(2, bw_chunk), col(3, bw_chunk), col(4, bw_chunk),
                  pl.BlockSpec((c, w), lambda g, s: (bw_chunk(s), g)),
                  lb_spec, pl.BlockSpec((1, HEAD_DIM), lambda g, s: (0, 0)), sel_spec],
        out_specs=pl.BlockSpec((c, w), lambda g, s: (bw_chunk(s), g)),
        out_shape=jax.ShapeDtypeStruct((n, A_WIDTH), BF16),
        scratch_shapes=scratch,
        compiler_params=_cparams(("parallel", "arbitrary")),
        name="hgrn_bw",
    )(p, p, p, p, o_fw, lb, norm_g.reshape(1, HEAD_DIM), sel)


def rope_tables(n, n_ctx):
    t = jnp.arange(n - n_ctx, dtype=jnp.int32)
    half = HEAD_DIM // 4
    inv_freq = ROPE_BASE ** (-jnp.arange(half, dtype=F32) / half)
    ang_r = (t // GRID_W).astype(F32)[:, None] * inv_freq
    ang_c = (t % GRID_W).astype(F32)[:, None] * inv_freq
    cos = jnp.concatenate([jnp.cos(ang_r)] * 2 + [jnp.cos(ang_c)] * 2, axis=-1)
    sin = jnp.concatenate([-jnp.sin(ang_r), jnp.sin(ang_r), -jnp.sin(ang_c), jnp.sin(ang_c)], axis=-1)
    cos = jnp.concatenate([jnp.ones((n_ctx, HEAD_DIM), F32), cos], axis=0)
    sin = jnp.concatenate([jnp.zeros((n_ctx, HEAD_DIM), F32), sin], axis=0)
    return cos, sin


def _rope_kernel(q_ref, k_ref, cos_ref, sin_ref, qo_ref, kt_ref):
    cos = cos_ref[...]
    sin = sin_ref[...]
    lane = lax.broadcasted_iota(jnp.int32, cos.shape, 1)
    low = (lane & (HEAD_DIM // 4)) == 0

    def rotate(x):
        partner = jnp.where(low, pltpu.roll(x, HEAD_DIM - HEAD_DIM // 4, 1), pltpu.roll(x, HEAD_DIM // 4, 1))
        return x * cos + partner * sin

    for h in range(2 * B_HEADS):
        cs = slice(h * HEAD_DIM, (h + 1) * HEAD_DIM)
        qo_ref[:, cs] = (rotate(q_ref[:, cs].astype(F32)) * (LOG2_E * HEAD_DIM ** -0.5)).astype(qo_ref.dtype)
        kt_ref[cs, :] = rotate(k_ref[:, cs].astype(F32)).T.astype(kt_ref.dtype)


def rope_qk(p, cos, sin, tr=256):
    n = p.shape[0]
    qblk = A_COLS // B_WIDTH
    return pl.pallas_call(
        _rope_kernel,
        grid=(n // tr,),
        in_specs=[pl.BlockSpec((tr, B_WIDTH), lambda i: (i, qblk)),
                  pl.BlockSpec((tr, B_WIDTH), lambda i: (i, qblk + 1)),
                  pl.BlockSpec((tr, HEAD_DIM), lambda i: (i, 0)),
                  pl.BlockSpec((tr, HEAD_DIM), lambda i: (i, 0))],
        out_specs=[pl.BlockSpec((tr, B_WIDTH), lambda i: (i, 0)),
                   pl.BlockSpec((B_WIDTH, tr), lambda i: (0, i))],
        out_shape=[jax.ShapeDtypeStruct((n, B_WIDTH), BF16), jax.ShapeDtypeStruct((B_WIDTH, n), BF16)],
        compiler_params=_cparams(("parallel",)),
        name="rope_qk",
    )(p, p, cos, sin)


def _diff_attn_kernel(lam_ref, g_ref, q_ref, kt_ref, v_ref, *rest, lambda_init, nk, tk):
    o_ref, m_scr, l_scr, acc_scr, sa_scr, sb_scr = rest[-6:]
    tq = q_ref.shape[0]
    q = q_ref[...]
    q1 = q[:, :HEAD_DIM]
    q2 = q[:, HEAD_DIM:]
    m_scr[...] = jnp.full_like(m_scr, -jnp.inf)
    l_scr[...] = jnp.zeros_like(l_scr)
    acc_scr[...] = jnp.zeros_like(acc_scr)

    def scores(j, s_scr):
        off = pl.multiple_of(j * tk, 128)
        s_scr[:tq, :] = _dot(q1, kt_ref[:HEAD_DIM, pl.ds(off, tk)])
        s_scr[tq:, :] = _dot(q2, kt_ref[HEAD_DIM:, pl.ds(off, tk)])

    def accumulate(j, s_scr):
        off = pl.multiple_of(j * tk, tk)
        s = s_scr[...]
        m_prev = m_scr[...]
        m_new = jnp.maximum(m_prev, jnp.max(s, axis=-1, keepdims=True))
        alpha = jnp.exp2(m_prev - m_new)
        p = jnp.exp2(s - m_new)
        l_scr[...] = alpha * l_scr[...] + jnp.sum(p, axis=-1, keepdims=True)
        acc_scr[...] = alpha * acc_scr[...] + _dot(p.astype(BF16), v_ref[pl.ds(off, tk), :])
        m_scr[...] = m_new

    scores(0, sa_scr)

    def pair(jj, carry):
        j = 2 * jj
        scores(j + 1, sb_scr)
        accumulate(j, sa_scr)
        scores(jnp.minimum(j + 2, nk - 1), sa_scr)
        accumulate(j + 1, sb_scr)
        return carry

    lax.fori_loop(0, nk // 2, pair, 0)
    if nk % 2:
        accumulate(nk - 1, sa_scr)

    lp = lam_ref[...]
    lam = (jnp.exp(jnp.sum(lp[0:1] * lp[1:2], axis=-1, keepdims=True))
           - jnp.exp(jnp.sum(lp[2:3] * lp[3:4], axis=-1, keepdims=True)) + lambda_init)
    o = acc_scr[:tq] / l_scr[:tq] - lam * (acc_scr[tq:] / l_scr[tq:])
    ms = jnp.mean(o * o, axis=-1, keepdims=True)
    y = o * lax.rsqrt(ms + NORM_EPS) * g_ref[...] * (1.0 - lambda_init)
    o_ref[...] = y.astype(o_ref.dtype)


def diff_attention(q, kt, p, lam_p, subln_g, lambda_init, q_row0, n_q, n_kv, tq, tk, out_rows=None):
    hw = 2 * HEAD_DIM
    n = q.shape[0]
    nq, nk = n_q // tq, n_kv // tk
    qb0 = q_row0 // tq
    vblk = (A_COLS + 2 * B_WIDTH) // hw
    in_specs = [pl.BlockSpec((4, HEAD_DIM), lambda h, i: (0, 0)),
                pl.BlockSpec((1, hw), lambda h, i: (0, 0)),
                pl.BlockSpec((tq, hw), lambda h, i: (i + qb0, h)),
                pl.BlockSpec((hw, n_kv), lambda h, i: (h, 0)),
                pl.BlockSpec((n_kv, hw), lambda h, i: (0, vblk + h))]
    args = [lam_p, subln_g.reshape(1, hw), q, kt, p]
    aliases = {}
    if out_rows is not None:
        in_specs.append(pl.BlockSpec(memory_space=pl.ANY))
        args.append(out_rows)
        aliases = {len(args) - 1: 0}
    return pl.pallas_call(
        functools.partial(_diff_attn_kernel, lambda_init=lambda_init, nk=nk, tk=tk),
        grid=(B_HEADS, nq),
        in_specs=in_specs,
        out_specs=pl.BlockSpec((tq, hw), lambda h, i: (i + qb0, h)),
        out_shape=jax.ShapeDtypeStruct((n, B_WIDTH), BF16),
        scratch_shapes=[pltpu.VMEM((2 * tq, 1), F32), pltpu.VMEM((2 * tq, 1), F32),
                        pltpu.VMEM((2 * tq, hw), F32),
                        pltpu.VMEM((2 * tq, tk), F32), pltpu.VMEM((2 * tq, tk), F32)],
        input_output_aliases=aliases,
        compiler_params=_cparams(("parallel", "arbitrary")),
        name="diff_attention",
    )(*args)


def na_bias_tables(rpb, rows):
    krows = NA_KBLKS * NA_QROWS
    n_dr, n_dc = 2 * NA_KH - 1, 2 * NA_KW - 1
    qc = np.arange(GRID_W)[:, None]
    kc = np.arange(GRID_W)[None, :]
    cs = np.clip(qc - NA_KW // 2, 0, GRID_W - NA_KW)
    col_ok = (kc >= cs) & (kc < cs + NA_KW)
    dc = np.clip(kc - qc + NA_KW - 1, 0, n_dc - 1)
    onehot = (np.arange(n_dc)[:, None, None] == dc[None]).astype(np.float32).reshape(n_dc, GRID_W * GRID_W)
    toep = jnp.dot(rpb.astype(F32).reshape(C_HEADS * n_dr, n_dc), jnp.asarray(onehot),
                   precision=lax.Precision.HIGHEST).reshape(C_HEADS, n_dr, GRID_W, GRID_W)
    toep = jnp.where(jnp.asarray(col_ok)[None, None], toep, MASK_NEG)
    masked = jnp.full((C_HEADS, GRID_W, GRID_W), MASK_NEG, F32)
    cases = [(0, 0), (NA_QROWS, 0), (rows - NA_QROWS, rows - krows)]
    out = []
    for r0, start in cases:
        q_blocks = []
        for qr in range(NA_QROWS):
            r = r0 + qr
            rs = min(max(r - NA_KH // 2, 0), rows - NA_KH)
            k_blocks = []
            for kr in range(krows):
                kra = start + kr
                k_blocks.append(toep[:, kra - r + NA_KH - 1] if rs <= kra < rs + NA_KH else masked)
            q_blocks.append(jnp.concatenate(k_blocks, axis=-1))
        out.append(jnp.concatenate(q_blocks, axis=1))
    return jnp.stack(out)


def _na_kernel(q_ref, k0_ref, k1_ref, k2_ref, kc_ref, v0_ref, v1_ref, v2_ref, vc_ref, bias_ref, o_ref):
    scale = HEAD_DIM ** -0.5
    k_refs = (k0_ref, k1_ref, k2_ref)
    v_refs = (v0_ref, v1_ref, v2_ref, vc_ref)
    for h in range(C_HEADS):
        cs = slice(h * HEAD_DIM, (h + 1) * HEAD_DIM)
        q = q_ref[:, cs]
        s_parts = [_dot_nt(q, k_refs[b][:, cs]) * scale + bias_ref[h, :, b * NA_QBLK:(b + 1) * NA_QBLK]
                   for b in range(NA_KBLKS)]
        s_parts.append(_dot_nt(q, kc_ref[:, cs]) * scale)
        m = s_parts[0].max(axis=-1, keepdims=True)
        for s in s_parts[1:]:
            m = jnp.maximum(m, s.max(axis=-1, keepdims=True))
        l = jnp.zeros_like(m)
        acc = jnp.zeros((q.shape[0], HEAD_DIM), F32)
        for s, v_ref in zip(s_parts, v_refs):
            pexp = jnp.exp(s - m)
            l = l + pexp.sum(axis=-1, keepdims=True)
            acc = acc + _dot(pexp.astype(BF16), v_ref[:, cs])
        o_ref[:, cs] = (acc / l).astype(o_ref.dtype)


def na_attention(p, bias, n_ctx):
    n = p.shape[0]
    nqb = (n - n_ctx) // NA_QBLK
    cb = n_ctx // NA_QBLK
    qblk = (A_COLS + B_COLS) // C_WIDTH

    def kv_spec(section, b):
        return pl.BlockSpec((NA_QBLK, C_WIDTH),
                            lambda i: (cb + jnp.clip(i - 1, 0, nqb - NA_KBLKS) + b, qblk + section))

    def ctx_spec(section):
        return pl.BlockSpec((n_ctx, C_WIDTH), lambda i: (0, qblk + section))

    def case(i):
        return jnp.where(i == 0, 0, jnp.where(i == nqb - 1, 2, 1))

    return pl.pallas_call(
        _na_kernel,
        grid=(nqb,),
        in_specs=[pl.BlockSpec((NA_QBLK, C_WIDTH), lambda i: (cb + i, qblk)),
                  kv_spec(1, 0), kv_spec(1, 1), kv_spec(1, 2), ctx_spec(1),
                  kv_spec(2, 0), kv_spec(2, 1), kv_spec(2, 2), ctx_spec(2),
                  pl.BlockSpec((None, C_HEADS, NA_QBLK, NA_KBLKS * NA_QBLK), lambda i: (case(i), 0, 0, 0))],
        out_specs=pl.BlockSpec((NA_QBLK, C_WIDTH), lambda i: (cb + i, 0)),
        out_shape=jax.ShapeDtypeStruct((n, C_WIDTH), BF16),
        compiler_params=_cparams(("parallel",)),
        name="na_attention",
    )(p, p, p, p, p, p, p, p, p, bias)


def _ctx_attn_kernel(q_ref, k_ref, v_ref, rows_hbm, o_ref):
    del rows_hbm
    scale = HEAD_DIM ** -0.5
    for h in range(C_HEADS):
        cs = slice(h * HEAD_DIM, (h + 1) * HEAD_DIM)
        s = _dot_nt(q_ref[:, cs], k_ref[:, cs]) * scale
        pexp = jnp.exp(s - s.max(axis=-1, keepdims=True))
        o = _dot(pexp.astype(BF16), v_ref[:, cs]) / pexp.sum(axis=-1, keepdims=True)
        o_ref[:, cs] = o.astype(o_ref.dtype)


def na_ctx_attention(p, n_ctx, out_rows):
    qblk = (A_COLS + B_COLS) // C_WIDTH
    return pl.pallas_call(
        _ctx_attn_kernel,
        grid=(1,),
        in_specs=[pl.BlockSpec((n_ctx, C_WIDTH), lambda i, s=s: (0, qblk + s)) for s in range(3)]
        + [pl.BlockSpec(memory_space=pl.ANY)],
        out_specs=pl.BlockSpec((n_ctx, C_WIDTH), lambda i: (0, 0)),
        out_shape=jax.ShapeDtypeStruct(out_rows.shape, BF16),
        input_output_aliases={3: 0},
        compiler_params=_cparams(("arbitrary",)),
        name="na_ctx_attention",
    )(p, p, p, out_rows)


MOE_TM = 256
MOE_COMBINE_TR = 128


def _deint_kernel(w_ref, p_ref, o_ref):
    o_ref[...] = _dot(w_ref[...].astype(BF16), p_ref[...]).astype(o_ref.dtype)


def deinterleave_gate_up(w_gu, tk=1024):
    g, d, f2 = w_gu.shape
    src = np.concatenate([np.arange(0, f2, 2), np.arange(1, f2, 2)])
    perm = jnp.asarray((np.arange(f2)[:, None] == src[None, :]).astype(np.float32), dtype=BF16)
    return pl.pallas_call(
        _deint_kernel,
        grid=(g, d // tk),
        in_specs=[pl.BlockSpec((None, tk, f2), lambda e, i: (e, i, 0)),
                  pl.BlockSpec((f2, f2), lambda e, i: (0, 0))],
        out_specs=pl.BlockSpec((None, tk, f2), lambda e, i: (e, i, 0)),
        out_shape=jax.ShapeDtypeStruct((g, d, f2), BF16),
        compiler_params=_cparams(("parallel", "parallel")),
        name="deinterleave_gate_up",
    )(w_gu, perm)


def _row_copy(src_hbm, src_row, dst, dst_row, sem):
    return pltpu.make_async_copy(src_hbm.at[pl.ds(src_row, 1), :], dst.at[pl.ds(dst_row, 1), :], sem)


def _moe_expert_kernel(te_ref, src_ref, next_ref, h_hbm, wgu_ref, bgu_ref, wd_ref, bd_ref, y_ref,
                       buf_a, buf_b, sem):
    t = pl.program_id(0)
    last = pl.num_programs(0) - 1
    tm = src_ref.shape[1]

    def wait_rows(buf, s):
        pltpu.make_async_copy(h_hbm.at[pl.ds(0, tm), :], buf, sem.at[s]).wait()

    @pl.when(t == 0)
    def _():
        def body(r, carry):
            _row_copy(h_hbm, src_ref[0, r], buf_a, r, sem.at[0]).start()
            return carry
        lax.fori_loop(0, tm, body, 0, unroll=8)

    def step(cur, s_cur, nxt, s_nxt):
        wait_rows(cur, s_cur)
        lo, hi = _unpack_bf16_pair(cur[...])
        half = lo.shape[1]
        gu = (_dot(lo.astype(BF16), wgu_ref[:half, :]) + _dot(hi.astype(BF16), wgu_ref[half:, :])
              + bgu_ref[...])
        f = gu.shape[1] // 2
        g = jnp.minimum(gu[:, :f], SWIGLU_LIMIT)
        u = jnp.clip(gu[:, f:], -SWIGLU_LIMIT, SWIGLU_LIMIT)
        act = (u + 1.0) * g * _sigmoid(g * SWIGLU_ALPHA)
        y = _dot(act.astype(BF16), wd_ref[...]) + bd_ref[...]
        y_ref[...] = _pack_bf16_pair(y[:, :half], y[:, half:])
        for r in range(tm):
            _row_copy(h_hbm, next_ref[0, r], nxt, r, sem.at[s_nxt]).start()

        @pl.when(t == last)
        def _():
            wait_rows(nxt, s_nxt)

    @pl.when(t % 2 == 0)
    def _():
        step(buf_a, 0, buf_b, 1)

    @pl.when(t % 2 == 1)
    def _():
        step(buf_b, 1, buf_a, 0)


def moe_experts(h_packed, src_rows, tile_expert, wgu, bgu, wd, bd):
    tm = MOE_TM
    n_tiles = tile_expert.shape[0]
    half = h_packed.shape[1]
    d, f2 = wgu.shape[1], wgu.shape[2]
    src3 = src_rows.reshape(n_tiles, 1, tm)
    grid_spec = pltpu.PrefetchScalarGridSpec(
        num_scalar_prefetch=1,
        grid=(n_tiles,),
        in_specs=[pl.BlockSpec((None, 1, tm), lambda t, te: (t, 0, 0), memory_space=pltpu.SMEM),
                  pl.BlockSpec((None, 1, tm), lambda t, te: (jnp.minimum(t + 1, n_tiles - 1), 0, 0),
                               memory_space=pltpu.SMEM),
                  pl.BlockSpec(memory_space=pl.ANY),
                  pl.BlockSpec((None, d, f2), lambda t, te: (te[t], 0, 0)),
                  pl.BlockSpec((None, 1, f2), lambda t, te: (te[t], 0, 0)),
                  pl.BlockSpec((None, f2 // 2, d), lambda t, te: (te[t], 0, 0)),
                  pl.BlockSpec((None, 1, d), lambda t, te: (te[t], 0, 0))],
        out_specs=pl.BlockSpec((tm, half), lambda t, te: (t, 0)),
        scratch_shapes=[pltpu.VMEM((tm, half), h_packed.dtype), pltpu.VMEM((tm, half), h_packed.dtype),
                        pltpu.SemaphoreType.DMA((2,))],
    )
    return pl.pallas_call(
        _moe_expert_kernel,
        grid_spec=grid_spec,
        out_shape=jax.ShapeDtypeStruct((n_tiles * tm, half), h_packed.dtype),
        compiler_params=_cparams(("arbitrary",)),
        name="moe_experts",
    )(tile_expert, src3, src3, h_packed, wgu, bgu, wd, bd)


def _moe_combine_kernel(dcur_ref, dnext_ref, wts_ref, x_ref, mod_ref, ys_hbm, o_ref, buf, sem, *, gi, n_ctx, tr):
    t = pl.program_id(0)
    slot = t % 2

    def gather(d_ref, s):
        def body(i, carry):
            for k in range(TOP_K):
                _row_copy(ys_hbm, d_ref[k, i], buf.at[s, k], i, sem.at[s]).start()
            return carry
        lax.fori_loop(0, tr, body, 0, unroll=4)

    @pl.when(t == 0)
    def _():
        gather(dcur_ref, 0)

    @pl.when(t + 1 < pl.num_programs(0))
    def _():
        gather(dnext_ref, 1 - slot)

    for k in range(TOP_K):
        pltpu.make_async_copy(ys_hbm.at[pl.ds(0, tr), :], buf.at[slot, k], sem.at[slot]).wait()

    half = buf.shape[-1]
    acc_lo = jnp.zeros((tr, half), F32)
    acc_hi = jnp.zeros((tr, half), F32)
    for k in range(TOP_K):
        lo, hi = _unpack_bf16_pair(buf[slot, k])
        w = wts_ref[:, k:k + 1]
        acc_lo = acc_lo + w * lo
        acc_hi = acc_hi + w * hi
    is_ctx = _row_is_ctx(t * tr, tr, n_ctx)
    gate = jnp.where(is_ctx, mod_ref[1, gi:gi + 1, :], mod_ref[0, gi:gi + 1, :])
    o_ref[:, :half] = x_ref[:, :half] + gate[:, :half] * acc_lo
    o_ref[:, half:] = x_ref[:, half:] + gate[:, half:] * acc_hi


def moe_combine(ys, dest_t, wts, x, mods, gi, n_ctx):
    tr = MOE_COMBINE_TR
    n, d = x.shape
    nt = n // tr
    return pl.pallas_call(
        functools.partial(_moe_combine_kernel, gi=gi, n_ctx=n_ctx, tr=tr),
        grid=(nt,),
        in_specs=[pl.BlockSpec((TOP_K, tr), lambda t: (0, t), memory_space=pltpu.SMEM),
                  pl.BlockSpec((TOP_K, tr), lambda t: (0, jnp.minimum(t + 1, nt - 1)), memory_space=pltpu.SMEM),
                  pl.BlockSpec((tr, TOP_K), lambda t: (t, 0)),
                  pl.BlockSpec((tr, d), lambda t: (t, 0)),
                  pl.BlockSpec((2, N_MOD, d), lambda t: (0, 0, 0)),
                  pl.BlockSpec(memory_space=pl.ANY)],
        out_specs=pl.BlockSpec((tr, d), lambda t: (t, 0)),
        out_shape=jax.ShapeDtypeStruct((n, d), F32),
        scratch_shapes=[pltpu.VMEM((2, TOP_K, tr, ys.shape[1]), ys.dtype), pltpu.SemaphoreType.DMA((2,))],
        compiler_params=_cparams(("arbitrary",)),
        name="moe_combine",
    )(dest_t, dest_t, wts, x, mods, ys)


def moe_routing_tables(idx, rank, counts, n_tiles, expert_base):
    tm = MOE_TM
    n = idx.shape[0]
    cnt = counts[0].astype(jnp.int32)
    padded = (cnt + tm - 1) // tm * tm
    ends = jnp.cumsum(padded)
    offs = ends - padded
    onehot = idx[:, :, None] == jnp.arange(N_EXPERTS, dtype=jnp.int32)
    dest = jnp.sum(jnp.where(onehot, offs, 0), axis=-1) + rank
    token = jnp.broadcast_to(jnp.arange(n, dtype=jnp.int32)[:, None], dest.shape)
    src_rows = jnp.zeros((n_tiles * tm,), jnp.int32).at[dest.reshape(-1)].set(token.reshape(-1))
    tile_start = jnp.arange(n_tiles, dtype=jnp.int32) * tm
    tile_expert = jnp.minimum(jnp.sum(tile_start[:, None] >= ends[None, :], axis=1), N_EXPERTS - 1)
    return dest.T, src_rows, tile_expert.astype(jnp.int32) + expert_base


def moe_ffn(x, norm_g, mods, router_w, router_b, wgu, bgu, wd, bd, expert_base, n_ctx):
    n = x.shape[0]
    n_tiles = -(-(TOP_K * n + N_EXPERTS * (MOE_TM - 1)) // MOE_TM)
    h_packed, idx, wts, rank, counts = norm_router(x, norm_g, mods, 3, router_w, router_b, n_ctx)
    dest_t, src_rows, tile_expert = moe_routing_tables(idx, rank, counts, n_tiles, expert_base)
    ys = moe_experts(h_packed, src_rows, tile_expert, wgu, bgu, wd, bd)
    return moe_combine(ys, dest_t, wts, x, mods, 5, n_ctx)


def _mm_tile(m):
    for tm in (1280, 1024, 640, 512, 256):
        if m % tm == 0:
            return tm
    raise ValueError(f"unsupported row count {m}")


def kernel(x, c, ctx, c_ctx, ada_w, ada_b, norm_mix, norm_ffn, w_in, w_out, hgrn_lb_logits, hgrn_norm,
           diff_lambda, diff_subln, na_rpb, router_w, router_b, expert_w_gu, expert_b_gu, expert_w_down,
           expert_b_down, final_norm_g):
    assert x.shape[0] == 1 and c.shape[0] == 1
    depth = ada_w.shape[0]
    n_ctx = ctx.shape[1]
    s_len = x.shape[1]
    n = n_ctx + s_len
    rows = s_len // GRID_W
    tm = _mm_tile(n)

    hall = jnp.concatenate([ctx[0], x[0]], axis=0)
    c8 = jnp.zeros((8, D_MODEL), F32).at[0].set(c[0]).at[1].set(c_ctx)
    mods_all = ada_mod(c8, ada_w, ada_b)[:, :2].reshape(depth, 2, N_MOD, D_MODEL)

    p_lb = jax.nn.softmax(hgrn_lb_logits.astype(F32), axis=0)
    lower_bounds = jnp.clip(jnp.cumsum(p_lb, axis=0) - p_lb[0], 0.0, 1.0)
    cos, sin = rope_tables(n, n_ctx)

    wgu_all = deinterleave_gate_up(expert_w_gu.reshape(depth * N_EXPERTS, D_MODEL, 2 * D_EXPERT))
    bgu_all = jnp.concatenate([expert_b_gu[..., 0::2], expert_b_gu[..., 1::2]], axis=-1)
    bgu_all = bgu_all.reshape(depth * N_EXPERTS, 1, 2 * D_EXPERT)
    wd_all = expert_w_down.reshape(depth * N_EXPERTS, D_EXPERT, D_MODEL).astype(BF16)
    bd_all = expert_b_down.reshape(depth * N_EXPERTS, 1, D_MODEL)
    tk = n // 13 if n % (13 * 256) == 0 else 256

    for layer in range(depth):
        lambda_init = 0.8 - 0.6 * math.exp(-0.3 * layer)
        mods = mods_all[layer]
        w_in_b = w_in[layer].astype(BF16)
        w_out_b = w_out[layer].astype(BF16)

        h = norm_mod(hall, norm_mix[layer], mods, 0, n_ctx)
        p = matmul(h, w_in_b, tm, 512, BF16)

        ya = hgrn_mixer(p, lower_bounds[layer], hgrn_norm[layer], n_ctx)
        qr, kt = rope_qk(p, cos, sin)
        yb = diff_attention(qr, kt, p, diff_lambda[layer], diff_subln[layer], lambda_init,
                            n_ctx, s_len, n, 256, tk)
        yb = diff_attention(qr, kt, p, diff_lambda[layer], diff_subln[layer], lambda_init,
                            0, n_ctx, n_ctx, n_ctx, n_ctx, out_rows=yb)
        yc = na_attention(p, na_bias_tables(na_rpb[layer], rows), n_ctx)
        yc = na_ctx_attention(p, n_ctx, yc)
        hall = matmul_residual([ya, yb, yc], w_out_b, hall, mods, 2, n_ctx, tm, 512)
        hall = moe_ffn(hall, norm_ffn[layer], mods, router_w[layer], router_b[layer], wgu_all, bgu_all,
                       wd_all, bd_all, layer * N_EXPERTS, n_ctx)

    return final_norm(hall, final_norm_g, n_ctx)[None]
```

```python
import functools
import math

import numpy as np
import jax
import jax.numpy as jnp
from jax import lax
from jax.experimental import pallas as pl
from jax.experimental.pallas import tpu as pltpu

F32 = jnp.float32
BF16 = jnp.bfloat16

D_MODEL = 4096
HEAD_DIM = 128
GRID_W = 64
A_HEADS = 12
A_WIDTH = A_HEADS * HEAD_DIM
B_HEADS = 5
B_WIDTH = B_HEADS * 2 * HEAD_DIM
C_HEADS = 10
C_WIDTH = C_HEADS * HEAD_DIM
A_COLS = 5 * A_WIDTH
B_COLS = 3 * B_WIDTH
C_COLS = 3 * C_WIDTH
IN_COLS = A_COLS + B_COLS + C_COLS
NA_KH = 8
NA_KW = 16
ROPE_BASE = 10000.0
N_EXPERTS = 32
TOP_K = 4
D_EXPERT = 512
SWIGLU_ALPHA = 1.702
SWIGLU_LIMIT = 7.0
N_MOD = 6
NORM_EPS = 1e-6
F_TINY = 1e-30
MASK_NEG = -1e30
LOG2_E = 1.4426950408889634

V7X_VMEM_LIMIT_BYTES = 56 * 1024 * 1024

SCAN_CHUNK = 64
SCAN_SUB = 16
NA_QROWS = 4
NA_QBLK = NA_QROWS * GRID_W
NA_KBLKS = 3


def _cparams(sem):
    return pltpu.CompilerParams(dimension_semantics=sem, vmem_limit_bytes=V7X_VMEM_LIMIT_BYTES)


def _dot(a, b):
    return jnp.dot(a, b, preferred_element_type=F32)


def _dot_nt(a, b):
    return lax.dot_general(a, b, (((1,), (1,)), ((), ())), preferred_element_type=F32)


def _dot_tn(a, b):
    return lax.dot_general(a, b, (((0,), (0,)), ((), ())), preferred_element_type=F32)


def _sigmoid(x):
    return 1.0 / (1.0 + jnp.exp(-x))


def _row_is_ctx(row0, rows, n_ctx):
    return (row0 + lax.broadcasted_iota(jnp.int32, (rows, 1), 0)) < n_ctx


def _ada_kernel(c_ref, w_ref, b_ref, o_ref):
    c = c_ref[...]
    a = (c * _sigmoid(c)).astype(BF16)
    o_ref[...] = _dot(a, w_ref[...].astype(BF16)) + b_ref[...]


def ada_mod(c8, ada_w, ada_b, tn=512):
    depth, d, n = ada_w.shape
    return pl.pallas_call(
        _ada_kernel,
        grid=(depth, n // tn),
        in_specs=[pl.BlockSpec((8, d), lambda l, j: (0, 0)),
                  pl.BlockSpec((None, d, tn), lambda l, j: (l, 0, j)),
                  pl.BlockSpec((None, 1, tn), lambda l, j: (l, 0, j))],
        out_specs=pl.BlockSpec((None, 8, tn), lambda l, j: (l, 0, j)),
        out_shape=jax.ShapeDtypeStruct((depth, 8, n), F32),
        compiler_params=_cparams(("arbitrary", "arbitrary")),
        name="ada_mod",
    )(c8, ada_w, ada_b.reshape(depth, 1, n))


def _norm_mod(x, g, mod_ref, si, row0, n_ctx):
    ms = jnp.mean(x * x, axis=-1, keepdims=True)
    y = x * lax.rsqrt(ms + NORM_EPS) * g
    is_ctx = _row_is_ctx(row0, x.shape[0], n_ctx)
    shift = jnp.where(is_ctx, mod_ref[1, si:si + 1, :], mod_ref[0, si:si + 1, :])
    scale = jnp.where(is_ctx, mod_ref[1, si + 1:si + 2, :], mod_ref[0, si + 1:si + 2, :])
    return y * (1.0 + scale) + shift


def _norm_kernel(x_ref, g_ref, mod_ref, o_ref, *, si, n_ctx, tr):
    row0 = pl.program_id(0) * tr
    o_ref[...] = _norm_mod(x_ref[...], g_ref[...], mod_ref, si, row0, n_ctx).astype(o_ref.dtype)


def norm_mod(x, g, mods, si, n_ctx, tr=256):
    n, d = x.shape
    return pl.pallas_call(
        functools.partial(_norm_kernel, si=si, n_ctx=n_ctx, tr=tr),
        grid=(n // tr,),
        in_specs=[pl.BlockSpec((tr, d), lambda i: (i, 0)),
                  pl.BlockSpec((1, d), lambda i: (0, 0)),
                  pl.BlockSpec((2, N_MOD, d), lambda i: (0, 0, 0))],
        out_specs=pl.BlockSpec((tr, d), lambda i: (i, 0)),
        out_shape=jax.ShapeDtypeStruct((n, d), BF16),
        compiler_params=_cparams(("parallel",)),
        name="norm_mod",
    )(x, g.reshape(1, d), mods)


def _split_bf16(x):
    hi = x.astype(BF16)
    lo = (x - hi.astype(F32)).astype(BF16)
    return hi, lo


def _pack_bf16_pair(lo, hi):
    lo_b = lax.bitcast_convert_type(lo.astype(BF16).astype(F32), jnp.uint32) >> 16
    hi_b = lax.bitcast_convert_type(hi.astype(BF16).astype(F32), jnp.uint32) & jnp.uint32(0xFFFF0000)
    return hi_b | lo_b


def _unpack_bf16_pair(w):
    lo = lax.bitcast_convert_type(w << 16, F32)
    hi = lax.bitcast_convert_type(w & jnp.uint32(0xFFFF0000), F32)
    return lo, hi


def _norm_router_kernel(x_ref, g_ref, mod_ref, rw_ref, rb_ref, h_ref, idx_ref, wts_ref, rank_ref, cnt_ref,
                        carry_ref, *, si, n_ctx, tr):
    @pl.when(pl.program_id(0) == 0)
    def _():
        carry_ref[...] = jnp.zeros_like(carry_ref)

    row0 = pl.program_id(0) * tr
    h = _norm_mod(x_ref[...], g_ref[...], mod_ref, si, row0, n_ctx)
    half = h.shape[1] // 2
    h_ref[...] = _pack_bf16_pair(h[:, :half], h[:, half:])
    hh, hl = _split_bf16(h)
    wh, wl = _split_bf16(rw_ref[...])
    logits = _dot(hh, wh) + _dot(hh, wl) + _dot(hl, wh) + rb_ref[...]
    iota = lax.broadcasted_iota(jnp.int32, logits.shape, 1)
    work = logits
    vals, sels = [], []
    for k in range(TOP_K):
        m = jnp.max(work, axis=-1, keepdims=True)
        idx = jnp.min(jnp.where(work == m, iota, N_EXPERTS), axis=-1, keepdims=True)
        sel = iota == idx
        vals.append(m)
        sels.append(sel)
        idx_ref[:, k:k + 1] = idx
        work = jnp.where(sel, -jnp.inf, work)
    es = [jnp.exp(v - vals[0]) for v in vals]
    tot = es[0] + es[1] + es[2] + es[3]
    for k in range(TOP_K):
        wts_ref[:, k:k + 1] = es[k] / tot
    used = jnp.zeros_like(logits)
    for k in range(TOP_K):
        used = used + jnp.where(sels[k], 1.0, 0.0)
    r = lax.broadcasted_iota(jnp.int32, (tr, tr), 0)
    c = lax.broadcasted_iota(jnp.int32, (tr, tr), 1)
    before = jnp.where(c < r, 1.0, 0.0).astype(BF16)
    prefix = _dot(before, used.astype(BF16)) + carry_ref[...]
    for k in range(TOP_K):
        rank = jnp.sum(jnp.where(sels[k], prefix, 0.0), axis=-1, keepdims=True)
        rank_ref[:, k:k + 1] = rank.astype(jnp.int32)
    carry_ref[...] += jnp.sum(used, axis=0, keepdims=True)
    cnt_ref[...] = carry_ref[...]


def norm_router(x, g, mods, si, router_w, router_b, n_ctx, tr=256):
    n, d = x.shape
    col = lambda i: (i, 0)
    fixed = lambda i: (0, 0)
    return pl.pallas_call(
        functools.partial(_norm_router_kernel, si=si, n_ctx=n_ctx, tr=tr),
        grid=(n // tr,),
        in_specs=[pl.BlockSpec((tr, d), col),
                  pl.BlockSpec((1, d), fixed),
                  pl.BlockSpec((2, N_MOD, d), lambda i: (0, 0, 0)),
                  pl.BlockSpec((d, N_EXPERTS), fixed),
                  pl.BlockSpec((1, N_EXPERTS), fixed)],
        out_specs=[pl.BlockSpec((tr, d // 2), col),
                   pl.BlockSpec((tr, TOP_K), col),
                   pl.BlockSpec((tr, TOP_K), col),
                   pl.BlockSpec((tr, TOP_K), col),
                   pl.BlockSpec((1, N_EXPERTS), fixed)],
        out_shape=[jax.ShapeDtypeStruct((n, d // 2), jnp.uint32),
                   jax.ShapeDtypeStruct((n, TOP_K), jnp.int32),
                   jax.ShapeDtypeStruct((n, TOP_K), F32),
                   jax.ShapeDtypeStruct((n, TOP_K), jnp.int32),
                   jax.ShapeDtypeStruct((1, N_EXPERTS), F32)],
        scratch_shapes=[pltpu.VMEM((1, N_EXPERTS), F32)],
        compiler_params=_cparams(("arbitrary",)),
        name="norm_router",
    )(x, g.reshape(1, d), mods, router_w, router_b.reshape(1, N_EXPERTS))


def _final_norm_kernel(x_ref, g_ref, o_ref):
    x = x_ref[...]
    ms = jnp.mean(x * x, axis=-1, keepdims=True)
    o_ref[...] = x * lax.rsqrt(ms + NORM_EPS) * g_ref[...]


def final_norm(x, g, n_ctx, tr=256):
    n, d = x.shape
    skip = n_ctx // tr
    return pl.pallas_call(
        _final_norm_kernel,
        grid=((n - n_ctx) // tr,),
        in_specs=[pl.BlockSpec((tr, d), lambda i: (i + skip, 0)),
                  pl.BlockSpec((1, d), lambda i: (0, 0))],
        out_specs=pl.BlockSpec((tr, d), lambda i: (i, 0)),
        out_shape=jax.ShapeDtypeStruct((n - n_ctx, d), F32),
        compiler_params=_cparams(("parallel",)),
        name="final_norm",
    )(x, g.reshape(1, d))


def _mm_kernel(a_ref, b_ref, o_ref):
    o_ref[...] = _dot(a_ref[...], b_ref[...]).astype(o_ref.dtype)


def matmul(a, b, tm, tn, out_dtype):
    m, k = a.shape
    n = b.shape[1]
    return pl.pallas_call(
        _mm_kernel,
        grid=(m // tm, n // tn),
        in_specs=[pl.BlockSpec((tm, k), lambda i, j: (i, 0)),
                  pl.BlockSpec((k, tn), lambda i, j: (0, j))],
        out_specs=pl.BlockSpec((tm, tn), lambda i, j: (i, j)),
        out_shape=jax.ShapeDtypeStruct((m, n), out_dtype),
        compiler_params=_cparams(("parallel", "arbitrary")),
        name="matmul",
    )(a, b)


def _mm_res_kernel(*refs, gi, n_ctx, tm, widths):
    a_refs = refs[:len(widths)]
    b_ref, x_ref, mod_ref, o_ref = refs[len(widths):]
    row0 = pl.program_id(0) * tm
    is_ctx = _row_is_ctx(row0, tm, n_ctx)
    gate = jnp.where(is_ctx, mod_ref[1, gi:gi + 1, :], mod_ref[0, gi:gi + 1, :])
    acc = None
    k0 = 0
    for a_ref, w in zip(a_refs, widths):
        part = _dot(a_ref[...], b_ref[k0:k0 + w, :])
        acc = part if acc is None else acc + part
        k0 += w
    o_ref[...] = x_ref[...] + gate * acc


def matmul_residual(a_parts, b, x, mods, gi, n_ctx, tm, tn):
    m = x.shape[0]
    k, n = b.shape
    widths = tuple(a.shape[1] for a in a_parts)
    assert sum(widths) == k
    return pl.pallas_call(
        functools.partial(_mm_res_kernel, gi=gi, n_ctx=n_ctx, tm=tm, widths=widths),
        grid=(m // tm, n // tn),
        in_specs=[pl.BlockSpec((tm, w), lambda i, j: (i, 0)) for w in widths]
        + [pl.BlockSpec((k, tn), lambda i, j: (0, j)),
           pl.BlockSpec((tm, tn), lambda i, j: (i, j)),
           pl.BlockSpec((2, N_MOD, tn), lambda i, j: (0, 0, j))],
        out_specs=pl.BlockSpec((tm, tn), lambda i, j: (i, j)),
        out_shape=jax.ShapeDtypeStruct((m, n), F32),
        compiler_params=_cparams(("parallel", "arbitrary")),
        name="matmul_residual",
    )(*a_parts, b, x, mods)


def _hgrn_masks(c, sub, reverse):
    r = lax.broadcasted_iota(jnp.int32, (c, c), 0)
    s = lax.broadcasted_iota(jnp.int32, (c, c), 1)
    tri = (s >= r) if reverse else (s <= r)
    same_sub = (r // sub) == (s // sub)
    return tri, same_sub


def _hgrn_head(q_raw, f_pre, v, lb, st_in, ck_ref, *, reverse, sel, tri_b, same_sub):
    c, d = q_raw.shape
    sub = SCAN_SUB
    nsub = c // sub
    q = q_raw * _sigmoid(q_raw) * (HEAD_DIM ** -0.5)
    f = lb + (1.0 - lb) * _sigmoid(f_pre)
    log_f = jnp.log2(jnp.maximum(f, F_TINY))
    k = (1.0 - lb) * _sigmoid(-f_pre)

    hi = log_f.astype(BF16)
    r1 = log_f - hi.astype(F32)
    mid = r1.astype(BF16)
    lo = (r1 - mid.astype(F32)).astype(BF16)
    yield
    cum = _dot(tri_b, hi) + _dot(tri_b, mid) + _dot(tri_b, lo)
    yield
    cum_k = cum - jnp.log2(k)
    ck_ref[...] = cum_k

    order = list(range(nsub))
    if reverse:
        order = order[::-1]
    zero_row = jnp.zeros((1, d), F32)
    bnd_in, bnd_out = {}, {}
    prev = zero_row
    for i in order:
        last = i * sub if reverse else i * sub + sub - 1
        bnd_in[i] = prev
        bnd_out[i] = cum[last:last + 1, :]
        prev = bnd_out[i]
    cum_end = prev

    t_in_sub = lax.broadcasted_iota(jnp.int32, (sub, 1), 0)
    q_hat, u_rows, off_rows = [], [], []
    for i in range(nsub):
        sl = slice(i * sub, (i + 1) * sub)
        q_i, cum_i = q[sl], cum[sl]
        q_hat_i = q_i * jnp.exp2(cum_i - bnd_in[i])
        k_parts = []
        for j in range(nsub):
            earlier = (j > i) if reverse else (j < i)
            sj = slice(j * sub, (j + 1) * sub)
            if earlier:
                k_parts.append(jnp.exp2(bnd_in[i] - cum_k[sj]))
            else:
                k_parts.append(jnp.zeros((sub, d), F32))
        k_til = jnp.concatenate(k_parts, axis=0).astype(BF16)
        off_rows.append(_dot_nt(q_hat_i.astype(BF16), k_til))
        yield
        u_parts = []
        for s in range(sub):
            groups = []
            for g8 in range(sub // 8):
                rows = slice(g8 * 8, g8 * 8 + 8)
                if g8 == s // 8:
                    rows_t = t_in_sub[rows]
                    ok = (rows_t <= s) if reverse else (rows_t >= s)
                    e = jnp.where(ok, cum_i[rows] - ck_ref[i * sub + s:i * sub + s + 1, :], MASK_NEG)
                elif (g8 < s // 8) == reverse:
                    e = cum_i[rows] - ck_ref[i * sub + s:i * sub + s + 1, :]
                else:
                    groups.append(jnp.zeros((8, d), BF16))
                    continue
                groups.append((q_i[rows] * jnp.exp2(e)).astype(BF16))
            u_parts.append(jnp.concatenate(groups, axis=0))
        u_rows.append(jnp.concatenate(u_parts, axis=1))
        yield
    u = jnp.concatenate(u_rows, axis=0)
    diag = _dot(u, sel)
    v_b = v.astype(BF16)
    k_end = jnp.exp2(cum_end - cum_k).astype(BF16)
    st_out = st_in * jnp.exp2(cum_end) + _dot_tn(v_b, k_end)
    o_state = _dot_nt((q * jnp.exp2(cum)).astype(BF16), st_in.astype(BF16))
    yield
    attn = jnp.concatenate(off_rows, axis=0) + jnp.where(same_sub, diag, 0.0)
    o = _dot(attn.astype(BF16), v_b) + o_state
    return o, st_out


def _run_interleaved(gens):
    results = [None] * len(gens)
    live = list(range(len(gens)))
    while live:
        for idx in list(live):
            try:
                next(gens[idx])
            except StopIteration as done:
                results[idx] = done.value
                live.remove(idx)
    return results


def _hgrn_sel(c, sub):
    rows = np.arange(sub * HEAD_DIM) // HEAD_DIM
    cols = np.arange(c) % sub
    return jnp.asarray((rows[:, None] == cols[None, :]).astype(np.float32), dtype=BF16)


def _hgrn_fw_kernel(q_ref, f_ref, v_ref, lb_ref, sel_ref, o_ref, st_ref, ck_ref, *, hb):
    @pl.when(pl.program_id(1) == 0)
    def _():
        st_ref[...] = jnp.zeros_like(st_ref)

    sel = sel_ref[...]
    tri, same_sub = _hgrn_masks(q_ref.shape[0], SCAN_SUB, False)
    tri_b = jnp.where(tri, 1.0, 0.0).astype(BF16)
    cols = [slice(h * HEAD_DIM, (h + 1) * HEAD_DIM) for h in range(hb)]
    heads = _run_interleaved([
        _hgrn_head(q_ref[:, cs].astype(F32), f_ref[:, cs].astype(F32), v_ref[:, cs].astype(F32),
                   lb_ref[0:1, cs], st_ref[h], ck_ref.at[h], reverse=False, sel=sel, tri_b=tri_b,
                   same_sub=same_sub)
        for h, cs in enumerate(cols)])
    for h, (cs, (o, st)) in enumerate(zip(cols, heads)):
        st_ref[h] = st
        o_ref[:, cs] = o


def _hgrn_bw_kernel(q_ref, f_ref, v_ref, g_ref, ofw_ref, lb_ref, ng_ref, sel_ref, y_ref, st_ref, ck_ref, *, hb):
    @pl.when(pl.program_id(1) == 0)
    def _():
        st_ref[...] = jnp.zeros_like(st_ref)

    sel = sel_ref[...]
    tri, same_sub = _hgrn_masks(q_ref.shape[0], SCAN_SUB, True)
    tri_b = jnp.where(tri, 1.0, 0.0).astype(BF16)
    cols = [slice(h * HEAD_DIM, (h + 1) * HEAD_DIM) for h in range(hb)]
    heads = _run_interleaved([
        _hgrn_head(q_ref[:, cs].astype(F32), f_ref[:, cs].astype(F32), v_ref[:, cs].astype(F32),
                   lb_ref[1:2, cs], st_ref[h], ck_ref.at[h], reverse=True, sel=sel, tri_b=tri_b,
                   same_sub=same_sub)
        for h, cs in enumerate(cols)])
    for h, (cs, (o, st)) in enumerate(zip(cols, heads)):
        st_ref[h] = st
        o = o + ofw_ref[:, cs]
        ms = jnp.mean(o * o, axis=-1, keepdims=True)
        y = o * lax.rsqrt(ms + NORM_EPS) * ng_ref[...]
        g = g_ref[:, cs].astype(F32)
        y_ref[:, cs] = (y * (g * _sigmoid(g))).astype(y_ref.dtype)


def hgrn_mixer(p, lb, norm_g, n_ctx, hb=6):
    n = p.shape[0]
    c = SCAN_CHUNK
    nc = n // c
    nc_ctx = n_ctx // c
    ngroups = A_HEADS // hb
    w = hb * HEAD_DIM
    per = A_WIDTH // w
    sel = _hgrn_sel(c, SCAN_SUB)

    def fw_chunk(s):
        return s

    def bw_chunk(s):
        return jnp.where(s < nc_ctx, nc_ctx - 1 - s, nc - 1 - (s - nc_ctx))

    def col(section, chunk_of):
        return pl.BlockSpec((c, w), lambda g, s: (chunk_of(s), section * per + g))

    lb_spec = pl.BlockSpec((2, w), lambda g, s: (0, g))
    sel_spec = pl.BlockSpec(sel.shape, lambda g, s: (0, 0))
    scratch = [pltpu.VMEM((hb, HEAD_DIM, HEAD_DIM), F32), pltpu.VMEM((hb, c, HEAD_DIM), F32)]
    o_fw = pl.pallas_call(
        functools.partial(_hgrn_fw_kernel, hb=hb),
        grid=(ngroups, nc),
        in_specs=[col(0, fw_chunk), col(1, fw_chunk), col(3, fw_chunk), lb_spec, sel_spec],
        out_specs=pl.BlockSpec((c, w), lambda g, s: (s, g)),
        out_shape=jax.ShapeDtypeStruct((n, A_WIDTH), F32),
        scratch_shapes=scratch,
        compiler_params=_cparams(("parallel", "arbitrary")),
        name="hgrn_fw",
    )(p, p, p, lb, sel)
    return pl.pallas_call(
        functools.partial(_hgrn_bw_kernel, hb=hb),
        grid=(ngroups, nc),
        in_specs=[col(0, bw_chunk), col(2, bw_chunk), col(3, bw_chunk), col(4, bw_chunk),
                  pl.BlockSpec((c, w), lambda g, s: (bw_chunk(s), g)),
                  lb_spec, pl.BlockSpec((1, HEAD_DIM), lambda g, s: (0, 0)), sel_spec],
        out_specs=pl.BlockSpec((c, w), lambda g, s: (bw_chunk(s), g)),
        out_shape=jax.ShapeDtypeStruct((n, A_WIDTH), BF16),
        scratch_shapes=scratch,
        compiler_params=_cparams(("parallel", "arbitrary")),
        name="hgrn_bw",
    )(p, p, p, p, o_fw, lb, norm_g.reshape(1, HEAD_DIM), sel)


def rope_tables(n, n_ctx):
    t = jnp.arange(n - n_ctx, dtype=jnp.int32)
    half = HEAD_DIM // 4
    inv_freq = ROPE_BASE ** (-jnp.arange(half, dtype=F32) / half)
    ang_r = (t // GRID_W).astype(F32)[:, None] * inv_freq
    ang_c = (t % GRID_W).astype(F32)[:, None] * inv_freq
    cos = jnp.concatenate([jnp.cos(ang_r)] * 2 + [jnp.cos(ang_c)] * 2, axis=-1)
    sin = jnp.concatenate([-jnp.sin(ang_r), jnp.sin(ang_r), -jnp.sin(ang_c), jnp.sin(ang_c)], axis=-1)
    cos = jnp.concatenate([jnp.ones((n_ctx, HEAD_DIM), F32), cos], axis=0)
    sin = jnp.concatenate([jnp.zeros((n_ctx, HEAD_DIM), F32), sin], axis=0)
    return cos, sin


def _rope_kernel(q_ref, k_ref, cos_ref, sin_ref, qo_ref, kt_ref):
    cos = cos_ref[...]
    sin = sin_ref[...]
    lane = lax.broadcasted_iota(jnp.int32, cos.shape, 1)
    low = (lane & (HEAD_DIM // 4)) == 0

    def rotate(x):
        partner = jnp.where(low, pltpu.roll(x, HEAD_DIM - HEAD_DIM // 4, 1), pltpu.roll(x, HEAD_DIM // 4, 1))
        return x * cos + partner * sin

    for h in range(2 * B_HEADS):
        cs = slice(h * HEAD_DIM, (h + 1) * HEAD_DIM)
        qo_ref[:, cs] = (rotate(q_ref[:, cs].astype(F32)) * (LOG2_E * HEAD_DIM ** -0.5)).astype(qo_ref.dtype)
        kt_ref[cs, :] = rotate(k_ref[:, cs].astype(F32)).T.astype(kt_ref.dtype)


def rope_qk(p, cos, sin, tr=256):
    n = p.shape[0]
    qblk = A_COLS // B_WIDTH
    return pl.pallas_call(
        _rope_kernel,
        grid=(n // tr,),
        in_specs=[pl.BlockSpec((tr, B_WIDTH), lambda i: (i, qblk)),
                  pl.BlockSpec((tr, B_WIDTH), lambda i: (i, qblk + 1)),
                  pl.BlockSpec((tr, HEAD_DIM), lambda i: (i, 0)),
                  pl.BlockSpec((tr, HEAD_DIM), lambda i: (i, 0))],
        out_specs=[pl.BlockSpec((tr, B_WIDTH), lambda i: (i, 0)),
                   pl.BlockSpec((B_WIDTH, tr), lambda i: (0, i))],
        out_shape=[jax.ShapeDtypeStruct((n, B_WIDTH), BF16), jax.ShapeDtypeStruct((B_WIDTH, n), BF16)],
        compiler_params=_cparams(("parallel",)),
        name="rope_qk",
    )(p, p, cos, sin)


def _diff_attn_kernel(lam_ref, g_ref, q_ref, kt_ref, v_ref, *rest, lambda_init, nk, tk):
    o_ref, m_scr, l_scr, acc_scr, sa_scr, sb_scr = rest[-6:]
    tq = q_ref.shape[0]
    q = q_ref[...]
    q1 = q[:, :HEAD_DIM]
    q2 = q[:, HEAD_DIM:]
    m_scr[...] = jnp.full_like(m_scr, -jnp.inf)
    l_scr[...] = jnp.zeros_like(l_scr)
    acc_scr[...] = jnp.zeros_like(acc_scr)

    def scores(j, s_scr):
        off = pl.multiple_of(j * tk, 128)
        s_scr[:tq, :] = _dot(q1, kt_ref[:HEAD_DIM, pl.ds(off, tk)])
        s_scr[tq:, :] = _dot(q2, kt_ref[HEAD_DIM:, pl.ds(off, tk)])

    def accumulate(j, s_scr):
        off = pl.multiple_of(j * tk, tk)
        s = s_scr[...]
        m_prev = m_scr[...]
        m_new = jnp.maximum(m_prev, jnp.max(s, axis=-1, keepdims=True))
        alpha = jnp.exp2(m_prev - m_new)
        p = jnp.exp2(s - m_new)
        l_scr[...] = alpha * l_scr[...] + jnp.sum(p, axis=-1, keepdims=True)
        acc_scr[...] = alpha * acc_scr[...] + _dot(p.astype(BF16), v_ref[pl.ds(off, tk), :])
        m_scr[...] = m_new

    scores(0, sa_scr)

    def pair(jj, carry):
        j = 2 * jj
        scores(j + 1, sb_scr)
        accumulate(j, sa_scr)
        scores(jnp.minimum(j + 2, nk - 1), sa_scr)
        accumulate(j + 1, sb_scr)
        return carry

    lax.fori_loop(0, nk // 2, pair, 0)
    if nk % 2:
        accumulate(nk - 1, sa_scr)

    lp = lam_ref[...]
    lam = (jnp.exp(jnp.sum(lp[0:1] * lp[1:2], axis=-1, keepdims=True))
           - jnp.exp(jnp.sum(lp[2:3] * lp[3:4], axis=-1, keepdims=True)) + lambda_init)
    o = acc_scr[:tq] / l_scr[:tq] - lam * (acc_scr[tq:] / l_scr[tq:])
    ms = jnp.mean(o * o, axis=-1, keepdims=True)
    y = o * lax.rsqrt(ms + NORM_EPS) * g_ref[...] * (1.0 - lambda_init)
    o_ref[...] = y.astype(o_ref.dtype)


def diff_attention(q, kt, p, lam_p, subln_g, lambda_init, q_row0, n_q, n_kv, tq, tk, out_rows=None):
    hw = 2 * HEAD_DIM
    n = q.shape[0]
    nq, nk = n_q // tq, n_kv // tk
    vblk = (A_COLS + 2 * B_WIDTH) // hw
    qb0 = q_row0 // tq
    q_spec = pl.BlockSpec((tq, hw), lambda h, i: (i + qb0, h))
    in_specs = [pl.BlockSpec((4, HEAD_DIM), lambda h, i: (0, 0)),
                pl.BlockSpec((1, hw), lambda h, i: (0, 0)),
                q_spec,
                pl.BlockSpec((hw, n_kv), lambda h, i: (h, 0)),
                pl.BlockSpec((n_kv, hw), lambda h, i: (0, vblk + h))]
    args = [lam_p, subln_g.reshape(1, hw), q, kt, p]
    aliases = {}
    if out_rows is not None:
        in_specs.append(pl.BlockSpec(memory_space=pl.ANY))
        args.append(out_rows)
        aliases = {len(args) - 1: 0}
    return pl.pallas_call(
        functools.partial(_diff_attn_kernel, lambda_init=lambda_init, nk=nk, tk=tk),
        grid=(B_HEADS, nq),
        in_specs=in_specs,
        out_specs=q_spec,
        out_shape=jax.ShapeDtypeStruct((n, B_WIDTH), BF16),
        scratch_shapes=[pltpu.VMEM((2 * tq, 1), F32), pltpu.VMEM((2 * tq, 1), F32),
                        pltpu.VMEM((2 * tq, hw), F32),
                        pltpu.VMEM((2 * tq, tk), F32), pltpu.VMEM((2 * tq, tk), F32)],
        input_output_aliases=aliases,
        compiler_params=_cparams(("parallel", "arbitrary")),
        name="diff_attention",
    )(*args)


def na_bias_tables(rpb, rows):
    krows = NA_KBLKS * NA_QROWS
    n_dr, n_dc = 2 * NA_KH - 1, 2 * NA_KW - 1
    qc = np.arange(GRID_W)[:, None]
    kc = np.arange(GRID_W)[None, :]
    cs = np.clip(qc - NA_KW // 2, 0, GRID_W - NA_KW)
    col_ok = (kc >= cs) & (kc < cs + NA_KW)
    dc = np.clip(kc - qc + NA_KW - 1, 0, n_dc - 1)
    onehot = (np.arange(n_dc)[:, None, None] == dc[None]).astype(np.float32).reshape(n_dc, GRID_W * GRID_W)
    toep = jnp.dot(rpb.astype(F32).reshape(C_HEADS * n_dr, n_dc), jnp.asarray(onehot),
                   precision=lax.Precision.HIGHEST).reshape(C_HEADS, n_dr, GRID_W, GRID_W)
    toep = jnp.where(jnp.asarray(col_ok)[None, None], toep, MASK_NEG)
    masked = jnp.full((C_HEADS, GRID_W, GRID_W), MASK_NEG, F32)
    cases = [(0, 0), (NA_QROWS, 0), (rows - NA_QROWS, rows - krows)]
    out = []
    for r0, start in cases:
        q_blocks = []
        for qr in range(NA_QROWS):
            r = r0 + qr
            rs = min(max(r - NA_KH // 2, 0), rows - NA_KH)
            k_blocks = []
            for kr in range(krows):
                kra = start + kr
                k_blocks.append(toep[:, kra - r + NA_KH - 1] if rs <= kra < rs + NA_KH else masked)
            q_blocks.append(jnp.concatenate(k_blocks, axis=-1))
        out.append(jnp.concatenate(q_blocks, axis=1))
    return jnp.stack(out)


def _na_kernel(q_ref, k0_ref, k1_ref, k2_ref, kc_ref, v0_ref, v1_ref, v2_ref, vc_ref, bias_ref, o_ref):
    scale = HEAD_DIM ** -0.5
    k_refs = (k0_ref, k1_ref, k2_ref)
    v_refs = (v0_ref, v1_ref, v2_ref, vc_ref)
    for h in range(C_HEADS):
        cs = slice(h * HEAD_DIM, (h + 1) * HEAD_DIM)
        q = q_ref[:, cs]
        s_parts = [_dot_nt(q, k_refs[b][:, cs]) * scale + bias_ref[h, :, b * NA_QBLK:(b + 1) * NA_QBLK]
                   for b in range(NA_KBLKS)]
        s_parts.append(_dot_nt(q, kc_ref[:, cs]) * scale)
        m = s_parts[0].max(axis=-1, keepdims=True)
        for s in s_parts[1:]:
            m = jnp.maximum(m, s.max(axis=-1, keepdims=True))
        l = jnp.zeros_like(m)
        acc = jnp.zeros((q.shape[0], HEAD_DIM), F32)
        for s, v_ref in zip(s_parts, v_refs):
            pexp = jnp.exp(s - m)
            l = l + pexp.sum(axis=-1, keepdims=True)
            acc = acc + _dot(pexp.astype(BF16), v_ref[:, cs])
        o_ref[:, cs] = (acc / l).astype(o_ref.dtype)


def na_attention(p, bias, n_ctx):
    n = p.shape[0]
    nqb = (n - n_ctx) // NA_QBLK
    cb = n_ctx // NA_QBLK
    qblk = (A_COLS + B_COLS) // C_WIDTH

    def kv_spec(section, b):
        return pl.BlockSpec((NA_QBLK, C_WIDTH),
                            lambda i: (cb + jnp.clip(i - 1, 0, nqb - NA_KBLKS) + b, qblk + section))

    def ctx_spec(section):
        return pl.BlockSpec((n_ctx, C_WIDTH), lambda i: (0, qblk + section))

    def case(i):
        return jnp.where(i == 0, 0, jnp.where(i == nqb - 1, 2, 1))

    return pl.pallas_call(
        _na_kernel,
        grid=(nqb,),
        in_specs=[pl.BlockSpec((NA_QBLK, C_WIDTH), lambda i: (cb + i, qblk)),
                  kv_spec(1, 0), kv_spec(1, 1), kv_spec(1, 2), ctx_spec(1),
                  kv_spec(2, 0), kv_spec(2, 1), kv_spec(2, 2), ctx_spec(2),
                  pl.BlockSpec((None, C_HEADS, NA_QBLK, NA_KBLKS * NA_QBLK), lambda i: (case(i), 0, 0, 0))],
        out_specs=pl.BlockSpec((NA_QBLK, C_WIDTH), lambda i: (cb + i, 0)),
        out_shape=jax.ShapeDtypeStruct((n, C_WIDTH), BF16),
        compiler_params=_cparams(("parallel",)),
        name="na_attention",
    )(p, p, p, p, p, p, p, p, p, bias)


def _ctx_attn_kernel(q_ref, k_ref, v_ref, rows_hbm, o_ref):
    del rows_hbm
    scale = HEAD_DIM ** -0.5
    for h in range(C_HEADS):
        cs = slice(h * HEAD_DIM, (h + 1) * HEAD_DIM)
        s = _dot_nt(q_ref[:, cs], k_ref[:, cs]) * scale
        pexp = jnp.exp(s - s.max(axis=-1, keepdims=True))
        o = _dot(pexp.astype(BF16), v_ref[:, cs]) / pexp.sum(axis=-1, keepdims=True)
        o_ref[:, cs] = o.astype(o_ref.dtype)


def na_ctx_attention(p, n_ctx, out_rows):
    qblk = (A_COLS + B_COLS) // C_WIDTH
    return pl.pallas_call(
        _ctx_attn_kernel,
        grid=(1,),
        in_specs=[pl.BlockSpec((n_ctx, C_WIDTH), lambda i, s=s: (0, qblk + s)) for s in range(3)]
        + [pl.BlockSpec(memory_space=pl.ANY)],
        out_specs=pl.BlockSpec((n_ctx, C_WIDTH), lambda i: (0, 0)),
        out_shape=jax.ShapeDtypeStruct(out_rows.shape, BF16),
        input_output_aliases={3: 0},
        compiler_params=_cparams(("arbitrary",)),
        name="na_ctx_attention",
    )(p, p, p, out_rows)


MOE_TM = 256
MOE_COMBINE_TR = 128


DEINT_BLOCK = 256


def _deint_kernel(w_ref, p_ref, o_ref):
    f2 = w_ref.shape[1]
    hb = DEINT_BLOCK // 2
    perm = p_ref[...]
    for j in range(f2 // DEINT_BLOCK):
        part = _dot(w_ref[:, j * DEINT_BLOCK:(j + 1) * DEINT_BLOCK].astype(BF16), perm).astype(o_ref.dtype)
        o_ref[:, j * hb:(j + 1) * hb] = part[:, :hb]
        o_ref[:, f2 // 2 + j * hb:f2 // 2 + (j + 1) * hb] = part[:, hb:]


def deinterleave_gate_up(w_gu, tk=1024):
    g, d, f2 = w_gu.shape
    src = np.concatenate([np.arange(0, DEINT_BLOCK, 2), np.arange(1, DEINT_BLOCK, 2)])
    perm = jnp.asarray((np.arange(DEINT_BLOCK)[:, None] == src[None, :]).astype(np.float32), dtype=BF16)
    return pl.pallas_call(
        _deint_kernel,
        grid=(g, d // tk),
        in_specs=[pl.BlockSpec((None, tk, f2), lambda e, i: (e, i, 0)),
                  pl.BlockSpec((DEINT_BLOCK, DEINT_BLOCK), lambda e, i: (0, 0))],
        out_specs=pl.BlockSpec((None, tk, f2), lambda e, i: (e, i, 0)),
        out_shape=jax.ShapeDtypeStruct((g, d, f2), BF16),
        compiler_params=_cparams(("parallel", "parallel")),
        name="deinterleave_gate_up",
    )(w_gu, perm)


def _row_copy(src_hbm, src_row, dst, dst_row, sem):
    return pltpu.make_async_copy(src_hbm.at[pl.ds(src_row, 1), :], dst.at[pl.ds(dst_row, 1), :], sem)


def _moe_expert_kernel(te_ref, src_ref, next_ref, h_hbm, wgu_ref, bgu_ref, wd_ref, bd_ref, y_ref, buf, sem):
    t = pl.program_id(0)
    slot = t % 2
    tm = src_ref.shape[1]

    def gather(idx_ref, s):
        def body(r, carry):
            _row_copy(h_hbm, idx_ref[0, r], buf.at[s], r, sem.at[s]).start()
            return carry
        lax.fori_loop(0, tm, body, 0, unroll=8)

    @pl.when(t == 0)
    def _():
        gather(src_ref, 0)

    @pl.when(t + 1 < pl.num_programs(0))
    def _():
        gather(next_ref, 1 - slot)

    pltpu.make_async_copy(h_hbm.at[pl.ds(0, tm), :], buf.at[slot], sem.at[slot]).wait()

    lo, hi = _unpack_bf16_pair(buf[slot])
    half = lo.shape[1]
    gu = (_dot(lo.astype(BF16), wgu_ref[:half, :]) + _dot(hi.astype(BF16), wgu_ref[half:, :])
          + bgu_ref[...])
    f = gu.shape[1] // 2
    g = jnp.minimum(gu[:, :f], SWIGLU_LIMIT)
    u = jnp.clip(gu[:, f:], -SWIGLU_LIMIT, SWIGLU_LIMIT)
    act = (u + 1.0) * g * _sigmoid(g * SWIGLU_ALPHA)
    y = _dot(act.astype(BF16), wd_ref[...]) + bd_ref[...]
    y_ref[...] = _pack_bf16_pair(y[:, :half], y[:, half:])


def moe_experts(h_packed, src_rows, tile_expert, wgu, bgu, wd, bd):
    tm = MOE_TM
    n_tiles = tile_expert.shape[0]
    half = h_packed.shape[1]
    d, f2 = wgu.shape[1], wgu.shape[2]
    src3 = src_rows.reshape(n_tiles, 1, tm)
    grid_spec = pltpu.PrefetchScalarGridSpec(
        num_scalar_prefetch=1,
        grid=(n_tiles,),
        in_specs=[pl.BlockSpec((None, 1, tm), lambda t, te: (t, 0, 0), memory_space=pltpu.SMEM),
                  pl.BlockSpec((None, 1, tm), lambda t, te: (jnp.minimum(t + 1, n_tiles - 1), 0, 0),
                               memory_space=pltpu.SMEM),
                  pl.BlockSpec(memory_space=pl.ANY),
                  pl.BlockSpec((None, d, f2), lambda t, te: (te[t], 0, 0)),
                  pl.BlockSpec((None, 1, f2), lambda t, te: (te[t], 0, 0)),
                  pl.BlockSpec((None, f2 // 2, d), lambda t, te: (te[t], 0, 0)),
                  pl.BlockSpec((None, 1, d), lambda t, te: (te[t], 0, 0))],
        out_specs=pl.BlockSpec((tm, half), lambda t, te: (t, 0)),
        scratch_shapes=[pltpu.VMEM((2, tm, half), h_packed.dtype), pltpu.SemaphoreType.DMA((2,))],
    )
    return pl.pallas_call(
        _moe_expert_kernel,
        grid_spec=grid_spec,
        out_shape=jax.ShapeDtypeStruct((n_tiles * tm, half), h_packed.dtype),
        compiler_params=_cparams(("arbitrary",)),
        name="moe_experts",
    )(tile_expert, src3, src3, h_packed, wgu, bgu, wd, bd)


def _moe_combine_kernel(dcur_ref, dnext_ref, wts_ref, x_ref, mod_ref, ys_hbm, o_ref, buf, sem, *, gi, n_ctx, tr):
    t = pl.program_id(0)
    slot = t % 2

    def gather(d_ref, s):
        def body(i, carry):
            for k in range(TOP_K):
                _row_copy(ys_hbm, d_ref[k, i], buf.at[s, k], i, sem.at[s]).start()
            return carry
        lax.fori_loop(0, tr, body, 0, unroll=4)

    @pl.when(t == 0)
    def _():
        gather(dcur_ref, 0)

    @pl.when(t + 1 < pl.num_programs(0))
    def _():
        gather(dnext_ref, 1 - slot)

    for k in range(TOP_K):
        pltpu.make_async_copy(ys_hbm.at[pl.ds(0, tr), :], buf.at[slot, k], sem.at[slot]).wait()

    half = buf.shape[-1]
    acc_lo = jnp.zeros((tr, half), F32)
    acc_hi = jnp.zeros((tr, half), F32)
    for k in range(TOP_K):
        lo, hi = _unpack_bf16_pair(buf[slot, k])
        w = wts_ref[:, k:k + 1]
        acc_lo = acc_lo + w * lo
        acc_hi = acc_hi + w * hi
    is_ctx = _row_is_ctx(t * tr, tr, n_ctx)
    gate = jnp.where(is_ctx, mod_ref[1, gi:gi + 1, :], mod_ref[0, gi:gi + 1, :])
    o_ref[:, :half] = x_ref[:, :half] + gate[:, :half] * acc_lo
    o_ref[:, half:] = x_ref[:, half:] + gate[:, half:] * acc_hi


def moe_combine(ys, dest_t, wts, x, mods, gi, n_ctx):
    tr = MOE_COMBINE_TR
    n, d = x.shape
    nt = n // tr
    return pl.pallas_call(
        functools.partial(_moe_combine_kernel, gi=gi, n_ctx=n_ctx, tr=tr),
        grid=(nt,),
        in_specs=[pl.BlockSpec((TOP_K, tr), lambda t: (0, t), memory_space=pltpu.SMEM),
                  pl.BlockSpec((TOP_K, tr), lambda t: (0, jnp.minimum(t + 1, nt - 1)), memory_space=pltpu.SMEM),
                  pl.BlockSpec((tr, TOP_K), lambda t: (t, 0)),
                  pl.BlockSpec((tr, d), lambda t: (t, 0)),
                  pl.BlockSpec((2, N_MOD, d), lambda t: (0, 0, 0)),
                  pl.BlockSpec(memory_space=pl.ANY)],
        out_specs=pl.BlockSpec((tr, d), lambda t: (t, 0)),
        out_shape=jax.ShapeDtypeStruct((n, d), F32),
        scratch_shapes=[pltpu.VMEM((2, TOP_K, tr, ys.shape[1]), ys.dtype), pltpu.SemaphoreType.DMA((2,))],
        compiler_params=_cparams(("arbitrary",)),
        name="moe_combine",
    )(dest_t, dest_t, wts, x, mods, ys)


def moe_routing_tables(idx, rank, counts, n_tiles, expert_base):
    tm = MOE_TM
    n = idx.shape[0]
    cnt = counts[0].astype(jnp.int32)
    padded = (cnt + tm - 1) // tm * tm
    ends = jnp.cumsum(padded)
    offs = ends - padded
    onehot = idx[:, :, None] == jnp.arange(N_EXPERTS, dtype=jnp.int32)
    dest = jnp.sum(jnp.where(onehot, offs, 0), axis=-1) + rank
    token = jnp.broadcast_to(jnp.arange(n, dtype=jnp.int32)[:, None], dest.shape)
    src_rows = jnp.zeros((n_tiles * tm,), jnp.int32).at[dest.reshape(-1)].set(token.reshape(-1))
    tile_start = jnp.arange(n_tiles, dtype=jnp.int32) * tm
    tile_expert = jnp.minimum(jnp.sum(tile_start[:, None] >= ends[None, :], axis=1), N_EXPERTS - 1)
    return dest.T, src_rows, tile_expert.astype(jnp.int32) + expert_base


def moe_ffn(x, norm_g, mods, router_w, router_b, wgu, bgu, wd, bd, expert_base, n_ctx):
    n = x.shape[0]
    n_tiles = -(-(TOP_K * n + N_EXPERTS * (MOE_TM - 1)) // MOE_TM)
    h_packed, idx, wts, rank, counts = norm_router(x, norm_g, mods, 3, router_w, router_b, n_ctx)
    dest_t, src_rows, tile_expert = moe_routing_tables(idx, rank, counts, n_tiles, expert_base)
    ys = moe_experts(h_packed, src_rows, tile_expert, wgu, bgu, wd, bd)
    return moe_combine(ys, dest_t, wts, x, mods, 5, n_ctx)


def _mm_tile(m):
    for tm in (1280, 1024, 640, 512, 256):
        if m % tm == 0:
            return tm
    raise ValueError(f"unsupported row count {m}")


def kernel(x, c, ctx, c_ctx, ada_w, ada_b, norm_mix, norm_ffn, w_in, w_out, hgrn_lb_logits, hgrn_norm,
           diff_lambda, diff_subln, na_rpb, router_w, router_b, expert_w_gu, expert_b_gu, expert_w_down,
           expert_b_down, final_norm_g):
    assert x.shape[0] == 1 and c.shape[0] == 1
    depth = ada_w.shape[0]
    n_ctx = ctx.shape[1]
    s_len = x.shape[1]
    n = n_ctx + s_len
    rows = s_len // GRID_W
    tm = _mm_tile(n)

    hall = jnp.concatenate([ctx[0], x[0]], axis=0)
    c8 = jnp.zeros((8, D_MODEL), F32).at[0].set(c[0]).at[1].set(c_ctx)
    mods_all = ada_mod(c8, ada_w, ada_b)[:, :2].reshape(depth, 2, N_MOD, D_MODEL)

    p_lb = jax.nn.softmax(hgrn_lb_logits.astype(F32), axis=0)
    lower_bounds = jnp.clip(jnp.cumsum(p_lb, axis=0) - p_lb[0], 0.0, 1.0)
    cos, sin = rope_tables(n, n_ctx)

    wgu_all = deinterleave_gate_up(expert_w_gu.reshape(depth * N_EXPERTS, D_MODEL, 2 * D_EXPERT))
    bgu_all = jnp.concatenate([expert_b_gu[..., 0::2], expert_b_gu[..., 1::2]], axis=-1)
    bgu_all = bgu_all.reshape(depth * N_EXPERTS, 1, 2 * D_EXPERT)
    wd_all = expert_w_down.reshape(depth * N_EXPERTS, D_EXPERT, D_MODEL).astype(BF16)
    bd_all = expert_b_down.reshape(depth * N_EXPERTS, 1, D_MODEL)
    tk = n // 13 if n % (13 * 256) == 0 else 256

    for layer in range(depth):
        lambda_init = 0.8 - 0.6 * math.exp(-0.3 * layer)
        mods = mods_all[layer]
        w_in_b = w_in[layer].astype(BF16)
        w_out_b = w_out[layer].astype(BF16)

        h = norm_mod(hall, norm_mix[layer], mods, 0, n_ctx)
        p = matmul(h, w_in_b, tm, 1024, BF16)

        ya = hgrn_mixer(p, lower_bounds[layer], hgrn_norm[layer], n_ctx)
        qr, kt = rope_qk(p, cos, sin)
        yb = diff_attention(qr, kt, p, diff_lambda[layer], diff_subln[layer], lambda_init,
                            n_ctx, s_len, n, 256, tk)
        yb = diff_attention(qr, kt, p, diff_lambda[layer], diff_subln[layer], lambda_init,
                            0, n_ctx, n_ctx, n_ctx, n_ctx, out_rows=yb)
        yc = na_attention(p, na_bias_tables(na_rpb[layer], rows), n_ctx)
        yc = na_ctx_attention(p, n_ctx, yc)
        hall = matmul_residual([ya, yb, yc], w_out_b, hall, mods, 2, n_ctx, tm, 512)
        hall = moe_ffn(hall, norm_ffn[layer], mods, router_w[layer], router_b[layer], wgu_all, bgu_all,
                       wd_all, bd_all, layer * N_EXPERTS, n_ctx)

    return final_norm(hall, final_norm_g, n_ctx)[None]
```

```python
import functools
import math

import numpy as np
import jax
import jax.numpy as jnp
from jax import lax
from jax.experimental import pallas as pl
from jax.experimental.pallas import tpu as pltpu

F32 = jnp.float32
BF16 = jnp.bfloat16

D_MODEL = 4096
HEAD_DIM = 128
GRID_W = 64
A_HEADS = 12
A_WIDTH = A_HEADS * HEAD_DIM
B_HEADS = 5
B_WIDTH = B_HEADS * 2 * HEAD_DIM
C_HEADS = 10
C_WIDTH = C_HEADS * HEAD_DIM
A_COLS = 5 * A_WIDTH
B_COLS = 3 * B_WIDTH
C_COLS = 3 * C_WIDTH
IN_COLS = A_COLS + B_COLS + C_COLS
NA_KH = 8
NA_KW = 16
ROPE_BASE = 10000.0
N_EXPERTS = 32
TOP_K = 4
D_EXPERT = 512
SWIGLU_ALPHA = 1.702
SWIGLU_LIMIT = 7.0
N_MOD = 6
NORM_EPS = 1e-6
F_TINY = 1e-30
MASK_NEG = -1e30
LOG2_E = 1.4426950408889634

V7X_VMEM_LIMIT_BYTES = 56 * 1024 * 1024

SCAN_CHUNK = 64
SCAN_SUB = 16
NA_QROWS = 4
NA_QBLK = NA_QROWS * GRID_W
NA_KBLKS = 3


def _cparams(sem):
    return pltpu.CompilerParams(dimension_semantics=sem, vmem_limit_bytes=V7X_VMEM_LIMIT_BYTES)


def _dot(a, b):
    return jnp.dot(a, b, preferred_element_type=F32)


def _dot_nt(a, b):
    return lax.dot_general(a, b, (((1,), (1,)), ((), ())), preferred_element_type=F32)


def _dot_tn(a, b):
    return lax.dot_general(a, b, (((0,), (0,)), ((), ())), preferred_element_type=F32)


def _sigmoid(x):
    return 1.0 / (1.0 + jnp.exp(-x))


def _row_is_ctx(row0, rows, n_ctx):
    return (row0 + lax.broadcasted_iota(jnp.int32, (rows, 1), 0)) < n_ctx


def _ada_kernel(c_ref, w_ref, b_ref, o_ref):
    c = c_ref[...]
    a = (c * _sigmoid(c)).astype(BF16)
    o_ref[...] = _dot(a, w_ref[...].astype(BF16)) + b_ref[...]


def ada_mod(c8, ada_w, ada_b, tn=512):
    depth, d, n = ada_w.shape
    return pl.pallas_call(
        _ada_kernel,
        grid=(depth, n // tn),
        in_specs=[pl.BlockSpec((8, d), lambda l, j: (0, 0)),
                  pl.BlockSpec((None, d, tn), lambda l, j: (l, 0, j)),
                  pl.BlockSpec((None, 1, tn), lambda l, j: (l, 0, j))],
        out_specs=pl.BlockSpec((None, 8, tn), lambda l, j: (l, 0, j)),
        out_shape=jax.ShapeDtypeStruct((depth, 8, n), F32),
        compiler_params=_cparams(("arbitrary", "arbitrary")),
        name="ada_mod",
    )(c8, ada_w, ada_b.reshape(depth, 1, n))


def _norm_mod(x, g, mod_ref, si, row0, n_ctx):
    ms = jnp.mean(x * x, axis=-1, keepdims=True)
    y = x * lax.rsqrt(ms + NORM_EPS) * g
    is_ctx = _row_is_ctx(row0, x.shape[0], n_ctx)
    shift = jnp.where(is_ctx, mod_ref[1, si:si + 1, :], mod_ref[0, si:si + 1, :])
    scale = jnp.where(is_ctx, mod_ref[1, si + 1:si + 2, :], mod_ref[0, si + 1:si + 2, :])
    return y * (1.0 + scale) + shift


def _norm_kernel(x_ref, g_ref, mod_ref, o_ref, *, si, n_ctx, tr):
    row0 = pl.program_id(0) * tr
    o_ref[...] = _norm_mod(x_ref[...], g_ref[...], mod_ref, si, row0, n_ctx).astype(o_ref.dtype)


def norm_mod(x, g, mods, si, n_ctx, tr=256):
    n, d = x.shape
    return pl.pallas_call(
        functools.partial(_norm_kernel, si=si, n_ctx=n_ctx, tr=tr),
        grid=(n // tr,),
        in_specs=[pl.BlockSpec((tr, d), lambda i: (i, 0)),
                  pl.BlockSpec((1, d), lambda i: (0, 0)),
                  pl.BlockSpec((2, N_MOD, d), lambda i: (0, 0, 0))],
        out_specs=pl.BlockSpec((tr, d), lambda i: (i, 0)),
        out_shape=jax.ShapeDtypeStruct((n, d), BF16),
        compiler_params=_cparams(("parallel",)),
        name="norm_mod",
    )(x, g.reshape(1, d), mods)


def _split_bf16(x):
    hi = x.astype(BF16)
    lo = (x - hi.astype(F32)).astype(BF16)
    return hi, lo


def _pack_bf16_pair(lo, hi):
    lo_b = lax.bitcast_convert_type(lo.astype(BF16).astype(F32), jnp.uint32) >> 16
    hi_b = lax.bitcast_convert_type(hi.astype(BF16).astype(F32), jnp.uint32) & jnp.uint32(0xFFFF0000)
    return hi_b | lo_b


def _unpack_bf16_pair(w):
    lo = lax.bitcast_convert_type(w << 16, F32)
    hi = lax.bitcast_convert_type(w & jnp.uint32(0xFFFF0000), F32)
    return lo, hi


def _norm_router_kernel(x_ref, g_ref, mod_ref, rw_ref, rb_ref, h_ref, idx_ref, wts_ref, rank_ref, cnt_ref,
                        carry_ref, *, si, n_ctx, tr):
    @pl.when(pl.program_id(0) == 0)
    def _():
        carry_ref[...] = jnp.zeros_like(carry_ref)

    row0 = pl.program_id(0) * tr
    h = _norm_mod(x_ref[...], g_ref[...], mod_ref, si, row0, n_ctx)
    half = h.shape[1] // 2
    h_ref[...] = _pack_bf16_pair(h[:, :half], h[:, half:])
    hh, hl = _split_bf16(h)
    wh, wl = _split_bf16(rw_ref[...])
    logits = _dot(hh, wh) + _dot(hh, wl) + _dot(hl, wh) + rb_ref[...]
    iota = lax.broadcasted_iota(jnp.int32, logits.shape, 1)
    work = logits
    vals, sels = [], []
    for k in range(TOP_K):
        m = jnp.max(work, axis=-1, keepdims=True)
        idx = jnp.min(jnp.where(work == m, iota, N_EXPERTS), axis=-1, keepdims=True)
        sel = iota == idx
        vals.append(m)
        sels.append(sel)
        idx_ref[:, k:k + 1] = idx
        work = jnp.where(sel, -jnp.inf, work)
    es = [jnp.exp(v - vals[0]) for v in vals]
    tot = es[0] + es[1] + es[2] + es[3]
    for k in range(TOP_K):
        wts_ref[:, k:k + 1] = es[k] / tot
    used = jnp.zeros_like(logits)
    for k in range(TOP_K):
        used = used + jnp.where(sels[k], 1.0, 0.0)
    r = lax.broadcasted_iota(jnp.int32, (tr, tr), 0)
    c = lax.broadcasted_iota(jnp.int32, (tr, tr), 1)
    before = jnp.where(c < r, 1.0, 0.0).astype(BF16)
    prefix = _dot(before, used.astype(BF16)) + carry_ref[...]
    for k in range(TOP_K):
        rank = jnp.sum(jnp.where(sels[k], prefix, 0.0), axis=-1, keepdims=True)
        rank_ref[:, k:k + 1] = rank.astype(jnp.int32)
    carry_ref[...] += jnp.sum(used, axis=0, keepdims=True)
    cnt_ref[...] = carry_ref[...]


def norm_router(x, g, mods, si, router_w, router_b, n_ctx, tr=256):
    n, d = x.shape
    col = lambda i: (i, 0)
    fixed = lambda i: (0, 0)
    return pl.pallas_call(
        functools.partial(_norm_router_kernel, si=si, n_ctx=n_ctx, tr=tr),
        grid=(n // tr,),
        in_specs=[pl.BlockSpec((tr, d), col),
                  pl.BlockSpec((1, d), fixed),
                  pl.BlockSpec((2, N_MOD, d), lambda i: (0, 0, 0)),
                  pl.BlockSpec((d, N_EXPERTS), fixed),
                  pl.BlockSpec((1, N_EXPERTS), fixed)],
        out_specs=[pl.BlockSpec((tr, d // 2), col),
                   pl.BlockSpec((tr, TOP_K), col),
                   pl.BlockSpec((tr, TOP_K), col),
                   pl.BlockSpec((tr, TOP_K), col),
                   pl.BlockSpec((1, N_EXPERTS), fixed)],
        out_shape=[jax.ShapeDtypeStruct((n, d // 2), jnp.uint32),
                   jax.ShapeDtypeStruct((n, TOP_K), jnp.int32),
                   jax.ShapeDtypeStruct((n, TOP_K), F32),
                   jax.ShapeDtypeStruct((n, TOP_K), jnp.int32),
                   jax.ShapeDtypeStruct((1, N_EXPERTS), F32)],
        scratch_shapes=[pltpu.VMEM((1, N_EXPERTS), F32)],
        compiler_params=_cparams(("arbitrary",)),
        name="norm_router",
    )(x, g.reshape(1, d), mods, router_w, router_b.reshape(1, N_EXPERTS))


def _final_norm_kernel(x_ref, g_ref, o_ref):
    x = x_ref[...]
    ms = jnp.mean(x * x, axis=-1, keepdims=True)
    o_ref[...] = x * lax.rsqrt(ms + NORM_EPS) * g_ref[...]


def final_norm(x, g, n_ctx, tr=256):
    n, d = x.shape
    skip = n_ctx // tr
    return pl.pallas_call(
        _final_norm_kernel,
        grid=((n - n_ctx) // tr,),
        in_specs=[pl.BlockSpec((tr, d), lambda i: (i + skip, 0)),
                  pl.BlockSpec((1, d), lambda i: (0, 0))],
        out_specs=pl.BlockSpec((tr, d), lambda i: (i, 0)),
        out_shape=jax.ShapeDtypeStruct((n - n_ctx, d), F32),
        compiler_params=_cparams(("parallel",)),
        name="final_norm",
    )(x, g.reshape(1, d))


def _mm_kernel(a_ref, b_ref, o_ref):
    o_ref[...] = _dot(a_ref[...], b_ref[...]).astype(o_ref.dtype)


def matmul(a, b, tm, tn, out_dtype):
    m, k = a.shape
    n = b.shape[1]
    return pl.pallas_call(
        _mm_kernel,
        grid=(m // tm, n // tn),
        in_specs=[pl.BlockSpec((tm, k), lambda i, j: (i, 0)),
                  pl.BlockSpec((k, tn), lambda i, j: (0, j))],
        out_specs=pl.BlockSpec((tm, tn), lambda i, j: (i, j)),
        out_shape=jax.ShapeDtypeStruct((m, n), out_dtype),
        compiler_params=_cparams(("parallel", "arbitrary")),
        name="matmul",
    )(a, b)


def _mm_res_kernel(*refs, gi, n_ctx, tm, widths):
    a_refs = refs[:len(widths)]
    b_ref, x_ref, mod_ref, o_ref = refs[len(widths):]
    row0 = pl.program_id(0) * tm
    is_ctx = _row_is_ctx(row0, tm, n_ctx)
    gate = jnp.where(is_ctx, mod_ref[1, gi:gi + 1, :], mod_ref[0, gi:gi + 1, :])
    acc = None
    k0 = 0
    for a_ref, w in zip(a_refs, widths):
        part = _dot(a_ref[...], b_ref[k0:k0 + w, :])
        acc = part if acc is None else acc + part
        k0 += w
    o_ref[...] = x_ref[...] + gate * acc


def matmul_residual(a_parts, b, x, mods, gi, n_ctx, tm, tn):
    m = x.shape[0]
    k, n = b.shape
    widths = tuple(a.shape[1] for a in a_parts)
    assert sum(widths) == k
    return pl.pallas_call(
        functools.partial(_mm_res_kernel, gi=gi, n_ctx=n_ctx, tm=tm, widths=widths),
        grid=(m // tm, n // tn),
        in_specs=[pl.BlockSpec((tm, w), lambda i, j: (i, 0)) for w in widths]
        + [pl.BlockSpec((k, tn), lambda i, j: (0, j)),
           pl.BlockSpec((tm, tn), lambda i, j: (i, j)),
           pl.BlockSpec((2, N_MOD, tn), lambda i, j: (0, 0, j))],
        out_specs=pl.BlockSpec((tm, tn), lambda i, j: (i, j)),
        out_shape=jax.ShapeDtypeStruct((m, n), F32),
        compiler_params=_cparams(("parallel", "arbitrary")),
        name="matmul_residual",
    )(*a_parts, b, x, mods)


def _hgrn_masks(c, sub, reverse):
    r = lax.broadcasted_iota(jnp.int32, (c, c), 0)
    s = lax.broadcasted_iota(jnp.int32, (c, c), 1)
    tri = (s >= r) if reverse else (s <= r)
    same_sub = (r // sub) == (s // sub)
    return tri, same_sub


def _hgrn_head(q_raw, f_pre, v, lb, st_in, ck_ref, *, reverse, sel, tri_b, same_sub):
    c, d = q_raw.shape
    sub = SCAN_SUB
    nsub = c // sub
    q = q_raw * _sigmoid(q_raw) * (HEAD_DIM ** -0.5)
    f = lb + (1.0 - lb) * _sigmoid(f_pre)
    log_f = jnp.log2(jnp.maximum(f, F_TINY))
    k = (1.0 - lb) * _sigmoid(-f_pre)

    hi = log_f.astype(BF16)
    r1 = log_f - hi.astype(F32)
    mid = r1.astype(BF16)
    lo = (r1 - mid.astype(F32)).astype(BF16)
    yield
    cum = _dot(tri_b, hi) + _dot(tri_b, mid) + _dot(tri_b, lo)
    yield
    cum_k = cum - jnp.log2(k)
    ck_ref[...] = cum_k

    order = list(range(nsub))
    if reverse:
        order = order[::-1]
    zero_row = jnp.zeros((1, d), F32)
    bnd_in, bnd_out = {}, {}
    prev = zero_row
    for i in order:
        last = i * sub if reverse else i * sub + sub - 1
        bnd_in[i] = prev
        bnd_out[i] = cum[last:last + 1, :]
        prev = bnd_out[i]
    cum_end = prev

    t_in_sub = lax.broadcasted_iota(jnp.int32, (sub, 1), 0)
    q_hat, u_rows, off_rows = [], [], []
    for i in range(nsub):
        sl = slice(i * sub, (i + 1) * sub)
        q_i, cum_i = q[sl], cum[sl]
        q_hat_i = q_i * jnp.exp2(cum_i - bnd_in[i])
        k_parts = []
        for j in range(nsub):
            earlier = (j > i) if reverse else (j < i)
            sj = slice(j * sub, (j + 1) * sub)
            if earlier:
                k_parts.append(jnp.exp2(bnd_in[i] - cum_k[sj]))
            else:
                k_parts.append(jnp.zeros((sub, d), F32))
        k_til = jnp.concatenate(k_parts, axis=0).astype(BF16)
        off_rows.append(_dot_nt(q_hat_i.astype(BF16), k_til))
        yield
        u_parts = []
        for s in range(sub):
            groups = []
            for g8 in range(sub // 8):
                rows = slice(g8 * 8, g8 * 8 + 8)
                if g8 == s // 8:
                    rows_t = t_in_sub[rows]
                    ok = (rows_t <= s) if reverse else (rows_t >= s)
                    e = jnp.where(ok, cum_i[rows] - ck_ref[i * sub + s:i * sub + s + 1, :], MASK_NEG)
                elif (g8 < s // 8) == reverse:
                    e = cum_i[rows] - ck_ref[i * sub + s:i * sub + s + 1, :]
                else:
                    groups.append(jnp.zeros((8, d), BF16))
                    continue
                groups.append((q_i[rows] * jnp.exp2(e)).astype(BF16))
            u_parts.append(jnp.concatenate(groups, axis=0))
        u_rows.append(jnp.concatenate(u_parts, axis=1))
        yield
    u = jnp.concatenate(u_rows, axis=0)
    diag = _dot(u, sel)
    v_b = v.astype(BF16)
    k_end = jnp.exp2(cum_end - cum_k).astype(BF16)
    st_out = st_in * jnp.exp2(cum_end) + _dot_tn(v_b, k_end)
    o_state = _dot_nt((q * jnp.exp2(cum)).astype(BF16), st_in.astype(BF16))
    yield
    attn = jnp.concatenate(off_rows, axis=0) + jnp.where(same_sub, diag, 0.0)
    o = _dot(attn.astype(BF16), v_b) + o_state
    return o, st_out


def _run_interleaved(gens):
    results = [None] * len(gens)
    live = list(range(len(gens)))
    while live:
        for idx in list(live):
            try:
                next(gens[idx])
            except StopIteration as done:
                results[idx] = done.value
                live.remove(idx)
    return results


def _hgrn_sel(c, sub):
    rows = np.arange(sub * HEAD_DIM) // HEAD_DIM
    cols = np.arange(c) % sub
    return jnp.asarray((rows[:, None] == cols[None, :]).astype(np.float32), dtype=BF16)


def _hgrn_fw_kernel(q_ref, f_ref, v_ref, lb_ref, sel_ref, o_ref, st_ref, ck_ref, *, hb):
    @pl.when(pl.program_id(1) == 0)
    def _():
        st_ref[...] = jnp.zeros_like(st_ref)

    sel = sel_ref[...]
    tri, same_sub = _hgrn_masks(q_ref.shape[0], SCAN_SUB, False)
    tri_b = jnp.where(tri, 1.0, 0.0).astype(BF16)
    cols = [slice(h * HEAD_DIM, (h + 1) * HEAD_DIM) for h in range(hb)]
    heads = _run_interleaved([
        _hgrn_head(q_ref[:, cs].astype(F32), f_ref[:, cs].astype(F32), v_ref[:, cs].astype(F32),
                   lb_ref[0:1, cs], st_ref[h], ck_ref.at[h], reverse=False, sel=sel, tri_b=tri_b,
                   same_sub=same_sub)
        for h, cs in enumerate(cols)])
    for h, (cs, (o, st)) in enumerate(zip(cols, heads)):
        st_ref[h] = st
        o_ref[:, cs] = o


def _hgrn_bw_kernel(q_ref, f_ref, v_ref, g_ref, ofw_ref, lb_ref, ng_ref, sel_ref, y_ref, st_ref, ck_ref, *, hb):
    @pl.when(pl.program_id(1) == 0)
    def _():
        st_ref[...] = jnp.zeros_like(st_ref)

    sel = sel_ref[...]
    tri, same_sub = _hgrn_masks(q_ref.shape[0], SCAN_SUB, True)
    tri_b = jnp.where(tri, 1.0, 0.0).astype(BF16)
    cols = [slice(h * HEAD_DIM, (h + 1) * HEAD_DIM) for h in range(hb)]
    heads = _run_interleaved([
        _hgrn_head(q_ref[:, cs].astype(F32), f_ref[:, cs].astype(F32), v_ref[:, cs].astype(F32),
                   lb_ref[1:2, cs], st_ref[h], ck_ref.at[h], reverse=True, sel=sel, tri_b=tri_b,
                   same_sub=same_sub)
        for h, cs in enumerate(cols)])
    for h, (cs, (o, st)) in enumerate(zip(cols, heads)):
        st_ref[h] = st
        o = o + ofw_ref[:, cs]
        ms = jnp.mean(o * o, axis=-1, keepdims=True)
        y = o * lax.rsqrt(ms + NORM_EPS) * ng_ref[...]
        g = g_ref[:, cs].astype(F32)
        y_ref[:, cs] = (y * (g * _sigmoid(g))).astype(y_ref.dtype)


def hgrn_mixer(p, lb, norm_g, n_ctx, hb=A_HEADS):
    n = p.shape[0]
    c = SCAN_CHUNK
    nc = n // c
    nc_ctx = n_ctx // c
    ngroups = A_HEADS // hb
    w = hb * HEAD_DIM
    per = A_WIDTH // w
    sel = _hgrn_sel(c, SCAN_SUB)

    def fw_chunk(s):
        return s

    def bw_chunk(s):
        return jnp.where(s < nc_ctx, nc_ctx - 1 - s, nc - 1 - (s - nc_ctx))

    def col(section, chunk_of):
        return pl.BlockSpec((c, w), lambda g, s: (chunk_of(s), section * per + g))

    lb_spec = pl.BlockSpec((2, w), lambda g, s: (0, g))
    sel_spec = pl.BlockSpec(sel.shape, lambda g, s: (0, 0))
    scratch = [pltpu.VMEM((hb, HEAD_DIM, HEAD_DIM), F32), pltpu.VMEM((hb, c, HEAD_DIM), F32)]
    o_fw = pl.pallas_call(
        functools.partial(_hgrn_fw_kernel, hb=hb),
        grid=(ngroups, nc),
        in_specs=[col(0, fw_chunk), col(1, fw_chunk), col(3, fw_chunk), lb_spec, sel_spec],
        out_specs=pl.BlockSpec((c, w), lambda g, s: (s, g)),
        out_shape=jax.ShapeDtypeStruct((n, A_WIDTH), F32),
        scratch_shapes=scratch,
        compiler_params=_cparams(("parallel", "arbitrary")),
        name="hgrn_fw",
    )(p, p, p, lb, sel)
    return pl.pallas_call(
        functools.partial(_hgrn_bw_kernel, hb=hb),
        grid=(ngroups, nc),
        in_specs=[col(0, bw_chunk), col(2, bw_chunk), col(3, bw_chunk), col(4, bw_chunk),
                  pl.BlockSpec((c, w), lambda g, s: (bw_chunk(s), g)),
                  lb_spec, pl.BlockSpec((1, HEAD_DIM), lambda g, s: (0, 0)), sel_spec],
        out_specs=pl.BlockSpec((c, w), lambda g, s: (bw_chunk(s), g)),
        out_shape=jax.ShapeDtypeStruct((n, A_WIDTH), BF16),
        scratch_shapes=scratch,
        compiler_params=_cparams(("parallel", "arbitrary")),
        name="hgrn_bw",
    )(p, p, p, p, o_fw, lb, norm_g.reshape(1, HEAD_DIM), sel)


def rope_tables(n, n_ctx):
    t = jnp.arange(n - n_ctx, dtype=jnp.int32)
    half = HEAD_DIM // 4
    inv_freq = ROPE_BASE ** (-jnp.arange(half, dtype=F32) / half)
    ang_r = (t // GRID_W).astype(F32)[:, None] * inv_freq
    ang_c = (t % GRID_W).astype(F32)[:, None] * inv_freq
    cos = jnp.concatenate([jnp.cos(ang_r)] * 2 + [jnp.cos(ang_c)] * 2, axis=-1)
    sin = jnp.concatenate([-jnp.sin(ang_r), jnp.sin(ang_r), -jnp.sin(ang_c), jnp.sin(ang_c)], axis=-1)
    cos = jnp.concatenate([jnp.ones((n_ctx, HEAD_DIM), F32), cos], axis=0)
    sin = jnp.concatenate([jnp.zeros((n_ctx, HEAD_DIM), F32), sin], axis=0)
    return cos, sin


def _rope_kernel(q_ref, k_ref, cos_ref, sin_ref, qo_ref, kt_ref):
    cos = cos_ref[...]
    sin = sin_ref[...]
    lane = lax.broadcasted_iota(jnp.int32, cos.shape, 1)
    low = (lane & (HEAD_DIM // 4)) == 0

    def rotate(x):
        partner = jnp.where(low, pltpu.roll(x, HEAD_DIM - HEAD_DIM // 4, 1), pltpu.roll(x, HEAD_DIM // 4, 1))
        return x * cos + partner * sin

    for h in range(2 * B_HEADS):
        cs = slice(h * HEAD_DIM, (h + 1) * HEAD_DIM)
        qo_ref[:, cs] = (rotate(q_ref[:, cs].astype(F32)) * (LOG2_E * HEAD_DIM ** -0.5)).astype(qo_ref.dtype)
        kt_ref[cs, :] = rotate(k_ref[:, cs].astype(F32)).T.astype(kt_ref.dtype)


def rope_qk(p, cos, sin, tr=256):
    n = p.shape[0]
    qblk = A_COLS // B_WIDTH
    return pl.pallas_call(
        _rope_kernel,
        grid=(n // tr,),
        in_specs=[pl.BlockSpec((tr, B_WIDTH), lambda i: (i, qblk)),
                  pl.BlockSpec((tr, B_WIDTH), lambda i: (i, qblk + 1)),
                  pl.BlockSpec((tr, HEAD_DIM), lambda i: (i, 0)),
                  pl.BlockSpec((tr, HEAD_DIM), lambda i: (i, 0))],
        out_specs=[pl.BlockSpec((tr, B_WIDTH), lambda i: (i, 0)),
                   pl.BlockSpec((B_WIDTH, tr), lambda i: (0, i))],
        out_shape=[jax.ShapeDtypeStruct((n, B_WIDTH), BF16), jax.ShapeDtypeStruct((B_WIDTH, n), BF16)],
        compiler_params=_cparams(("parallel",)),
        name="rope_qk",
    )(p, p, cos, sin)


def _diff_attn_kernel(lam_ref, g_ref, q_ref, kt_ref, v_ref, *rest, lambda_init, nk, tk):
    o_ref, m_scr, l_scr, acc_scr, sa_scr, sb_scr = rest[-6:]
    tq = q_ref.shape[0]
    q = q_ref[...]
    q1 = q[:, :HEAD_DIM]
    q2 = q[:, HEAD_DIM:]
    m_scr[...] = jnp.full_like(m_scr, -jnp.inf)
    l_scr[...] = jnp.zeros_like(l_scr)
    acc_scr[...] = jnp.zeros_like(acc_scr)

    def scores(j, s_scr):
        off = pl.multiple_of(j * tk, 128)
        s_scr[:tq, :] = _dot(q1, kt_ref[:HEAD_DIM, pl.ds(off, tk)])
        s_scr[tq:, :] = _dot(q2, kt_ref[HEAD_DIM:, pl.ds(off, tk)])

    def accumulate(j, s_scr):
        off = pl.multiple_of(j * tk, tk)
        s = s_scr[...]
        m_prev = m_scr[...]
        m_new = jnp.maximum(m_prev, jnp.max(s, axis=-1, keepdims=True))
        alpha = jnp.exp2(m_prev - m_new)
        p = jnp.exp2(s - m_new)
        l_scr[...] = alpha * l_scr[...] + jnp.sum(p, axis=-1, keepdims=True)
        acc_scr[...] = alpha * acc_scr[...] + _dot(p.astype(BF16), v_ref[pl.ds(off, tk), :])
        m_scr[...] = m_new

    scores(0, sa_scr)

    def pair(jj, carry):
        j = 2 * jj
        scores(j + 1, sb_scr)
        accumulate(j, sa_scr)
        scores(jnp.minimum(j + 2, nk - 1), sa_scr)
        accumulate(j + 1, sb_scr)
        return carry

    lax.fori_loop(0, nk // 2, pair, 0)
    if nk % 2:
        accumulate(nk - 1, sa_scr)

    lp = lam_ref[...]
    lam = (jnp.exp(jnp.sum(lp[0:1] * lp[1:2], axis=-1, keepdims=True))
           - jnp.exp(jnp.sum(lp[2:3] * lp[3:4], axis=-1, keepdims=True)) + lambda_init)
    o = acc_scr[:tq] / l_scr[:tq] - lam * (acc_scr[tq:] / l_scr[tq:])
    ms = jnp.mean(o * o, axis=-1, keepdims=True)
    y = o * lax.rsqrt(ms + NORM_EPS) * g_ref[...] * (1.0 - lambda_init)
    o_ref[...] = y.astype(o_ref.dtype)


def diff_attention(q, kt, p, lam_p, subln_g, lambda_init, q_row0, n_q, n_kv, tq, tk, out_rows=None):
    hw = 2 * HEAD_DIM
    n = q.shape[0]
    nq, nk = n_q // tq, n_kv // tk
    vblk = (A_COLS + 2 * B_WIDTH) // hw
    qb0 = q_row0 // tq
    q_spec = pl.BlockSpec((tq, hw), lambda h, i: (i + qb0, h))
    in_specs = [pl.BlockSpec((4, HEAD_DIM), lambda h, i: (0, 0)),
                pl.BlockSpec((1, hw), lambda h, i: (0, 0)),
                q_spec,
                pl.BlockSpec((hw, n_kv), lambda h, i: (h, 0)),
                pl.BlockSpec((n_kv, hw), lambda h, i: (0, vblk + h))]
    args = [lam_p, subln_g.reshape(1, hw), q, kt, p]
    aliases = {}
    if out_rows is not None:
        in_specs.append(pl.BlockSpec(memory_space=pl.ANY))
        args.append(out_rows)
        aliases = {len(args) - 1: 0}
    return pl.pallas_call(
        functools.partial(_diff_attn_kernel, lambda_init=lambda_init, nk=nk, tk=tk),
        grid=(B_HEADS, nq),
        in_specs=in_specs,
        out_specs=q_spec,
        out_shape=jax.ShapeDtypeStruct((n, B_WIDTH), BF16),
        scratch_shapes=[pltpu.VMEM((2 * tq, 1), F32), pltpu.VMEM((2 * tq, 1), F32),
                        pltpu.VMEM((2 * tq, hw), F32),
                        pltpu.VMEM((2 * tq, tk), F32), pltpu.VMEM((2 * tq, tk), F32)],
        input_output_aliases=aliases,
        compiler_params=_cparams(("parallel", "arbitrary")),
        name="diff_attention",
    )(*args)


def na_bias_tables(rpb, rows):
    krows = NA_KBLKS * NA_QROWS
    n_dr, n_dc = 2 * NA_KH - 1, 2 * NA_KW - 1
    qc = np.arange(GRID_W)[:, None]
    kc = np.arange(GRID_W)[None, :]
    cs = np.clip(qc - NA_KW // 2, 0, GRID_W - NA_KW)
    col_ok = (kc >= cs) & (kc < cs + NA_KW)
    dc = np.clip(kc - qc + NA_KW - 1, 0, n_dc - 1)
    onehot = (np.arange(n_dc)[:, None, None] == dc[None]).astype(np.float32).reshape(n_dc, GRID_W * GRID_W)
    toep = jnp.dot(rpb.astype(F32).reshape(C_HEADS * n_dr, n_dc), jnp.asarray(onehot),
                   precision=lax.Precision.HIGHEST).reshape(C_HEADS, n_dr, GRID_W, GRID_W)
    toep = jnp.where(jnp.asarray(col_ok)[None, None], toep, MASK_NEG)
    masked = jnp.full((C_HEADS, GRID_W, GRID_W), MASK_NEG, F32)
    cases = [(0, 0), (NA_QROWS, 0), (rows - NA_QROWS, rows - krows)]
    out = []
    for r0, start in cases:
        q_blocks = []
        for qr in range(NA_QROWS):
            r = r0 + qr
            rs = min(max(r - NA_KH // 2, 0), rows - NA_KH)
            k_blocks = []
            for kr in range(krows):
                kra = start + kr
                k_blocks.append(toep[:, kra - r + NA_KH - 1] if rs <= kra < rs + NA_KH else masked)
            q_blocks.append(jnp.concatenate(k_blocks, axis=-1))
        out.append(jnp.concatenate(q_blocks, axis=1))
    return jnp.stack(out)


def _na_kernel(q_ref, k0_ref, k1_ref, k2_ref, kc_ref, v0_ref, v1_ref, v2_ref, vc_ref, bias_ref, o_ref):
    scale = HEAD_DIM ** -0.5
    k_refs = (k0_ref, k1_ref, k2_ref)
    v_refs = (v0_ref, v1_ref, v2_ref, vc_ref)
    for h in range(C_HEADS):
        cs = slice(h * HEAD_DIM, (h + 1) * HEAD_DIM)
        q = q_ref[:, cs]
        s_parts = [_dot_nt(q, k_refs[b][:, cs]) * scale + bias_ref[h, :, b * NA_QBLK:(b + 1) * NA_QBLK]
                   for b in range(NA_KBLKS)]
        s_parts.append(_dot_nt(q, kc_ref[:, cs]) * scale)
        m = s_parts[0].max(axis=-1, keepdims=True)
        for s in s_parts[1:]:
            m = jnp.maximum(m, s.max(axis=-1, keepdims=True))
        l = jnp.zeros_like(m)
        acc = jnp.zeros((q.shape[0], HEAD_DIM), F32)
        for s, v_ref in zip(s_parts, v_refs):
            pexp = jnp.exp(s - m)
            l = l + pexp.sum(axis=-1, keepdims=True)
            acc = acc + _dot(pexp.astype(BF16), v_ref[:, cs])
        o_ref[:, cs] = (acc / l).astype(o_ref.dtype)


def na_attention(p, bias, n_ctx):
    n = p.shape[0]
    nqb = (n - n_ctx) // NA_QBLK
    cb = n_ctx // NA_QBLK
    qblk = (A_COLS + B_COLS) // C_WIDTH

    def kv_spec(section, b):
        return pl.BlockSpec((NA_QBLK, C_WIDTH),
                            lambda i: (cb + jnp.clip(i - 1, 0, nqb - NA_KBLKS) + b, qblk + section))

    def ctx_spec(section):
        return pl.BlockSpec((n_ctx, C_WIDTH), lambda i: (0, qblk + section))

    def case(i):
        return jnp.where(i == 0, 0, jnp.where(i == nqb - 1, 2, 1))

    return pl.pallas_call(
        _na_kernel,
        grid=(nqb,),
        in_specs=[pl.BlockSpec((NA_QBLK, C_WIDTH), lambda i: (cb + i, qblk)),
                  kv_spec(1, 0), kv_spec(1, 1), kv_spec(1, 2), ctx_spec(1),
                  kv_spec(2, 0), kv_spec(2, 1), kv_spec(2, 2), ctx_spec(2),
                  pl.BlockSpec((None, C_HEADS, NA_QBLK, NA_KBLKS * NA_QBLK), lambda i: (case(i), 0, 0, 0))],
        out_specs=pl.BlockSpec((NA_QBLK, C_WIDTH), lambda i: (cb + i, 0)),
        out_shape=jax.ShapeDtypeStruct((n, C_WIDTH), BF16),
        compiler_params=_cparams(("parallel",)),
        name="na_attention",
    )(p, p, p, p, p, p, p, p, p, bias)


def _ctx_attn_kernel(q_ref, k_ref, v_ref, rows_hbm, o_ref):
    del rows_hbm
    scale = HEAD_DIM ** -0.5
    for h in range(C_HEADS):
        cs = slice(h * HEAD_DIM, (h + 1) * HEAD_DIM)
        s = _dot_nt(q_ref[:, cs], k_ref[:, cs]) * scale
        pexp = jnp.exp(s - s.max(axis=-1, keepdims=True))
        o = _dot(pexp.astype(BF16), v_ref[:, cs]) / pexp.sum(axis=-1, keepdims=True)
        o_ref[:, cs] = o.astype(o_ref.dtype)


def na_ctx_attention(p, n_ctx, out_rows):
    qblk = (A_COLS + B_COLS) // C_WIDTH
    return pl.pallas_call(
        _ctx_attn_kernel,
        grid=(1,),
        in_specs=[pl.BlockSpec((n_ctx, C_WIDTH), lambda i, s=s: (0, qblk + s)) for s in range(3)]
        + [pl.BlockSpec(memory_space=pl.ANY)],
        out_specs=pl.BlockSpec((n_ctx, C_WIDTH), lambda i: (0, 0)),
        out_shape=jax.ShapeDtypeStruct(out_rows.shape, BF16),
        input_output_aliases={3: 0},
        compiler_params=_cparams(("arbitrary",)),
        name="na_ctx_attention",
    )(p, p, p, out_rows)


MOE_TM = 512
MOE_COMBINE_TR = 128


DEINT_BLOCK = 256


def _deint_kernel(w_ref, p_ref, o_ref):
    f2 = w_ref.shape[1]
    hb = DEINT_BLOCK // 2
    perm = p_ref[...]
    for j in range(f2 // DEINT_BLOCK):
        part = _dot(w_ref[:, j * DEINT_BLOCK:(j + 1) * DEINT_BLOCK].astype(BF16), perm).astype(o_ref.dtype)
        o_ref[:, j * hb:(j + 1) * hb] = part[:, :hb]
        o_ref[:, f2 // 2 + j * hb:f2 // 2 + (j + 1) * hb] = part[:, hb:]


def deinterleave_gate_up(w_gu, tk=1024):
    g, d, f2 = w_gu.shape
    src = np.concatenate([np.arange(0, DEINT_BLOCK, 2), np.arange(1, DEINT_BLOCK, 2)])
    perm = jnp.asarray((np.arange(DEINT_BLOCK)[:, None] == src[None, :]).astype(np.float32), dtype=BF16)
    return pl.pallas_call(
        _deint_kernel,
        grid=(g, d // tk),
        in_specs=[pl.BlockSpec((None, tk, f2), lambda e, i: (e, i, 0)),
                  pl.BlockSpec((DEINT_BLOCK, DEINT_BLOCK), lambda e, i: (0, 0))],
        out_specs=pl.BlockSpec((None, tk, f2), lambda e, i: (e, i, 0)),
        out_shape=jax.ShapeDtypeStruct((g, d, f2), BF16),
        compiler_params=_cparams(("parallel", "parallel")),
        name="deinterleave_gate_up",
    )(w_gu, perm)


def _row_copy(src_hbm, src_row, dst, dst_row, sem):
    return pltpu.make_async_copy(src_hbm.at[pl.ds(src_row, 1), :], dst.at[pl.ds(dst_row, 1), :], sem)


def _moe_expert_kernel(te_ref, src_ref, next_ref, h_hbm, wgu_ref, bgu_ref, wd_ref, bd_ref, y_ref, buf, sem):
    t = pl.program_id(0)
    slot = t % 2
    tm = src_ref.shape[1]

    def gather(idx_ref, s):
        def body(r, carry):
            _row_copy(h_hbm, idx_ref[0, r], buf.at[s], r, sem.at[s]).start()
            return carry
        lax.fori_loop(0, tm, body, 0, unroll=8)

    @pl.when(t == 0)
    def _():
        gather(src_ref, 0)

    @pl.when(t + 1 < pl.num_programs(0))
    def _():
        gather(next_ref, 1 - slot)

    pltpu.make_async_copy(h_hbm.at[pl.ds(0, tm), :], buf.at[slot], sem.at[slot]).wait()

    lo, hi = _unpack_bf16_pair(buf[slot])
    half = lo.shape[1]
    gu = (_dot(lo.astype(BF16), wgu_ref[:half, :]) + _dot(hi.astype(BF16), wgu_ref[half:, :])
          + bgu_ref[...])
    f = gu.shape[1] // 2
    g = jnp.minimum(gu[:, :f], SWIGLU_LIMIT)
    u = jnp.clip(gu[:, f:], -SWIGLU_LIMIT, SWIGLU_LIMIT)
    act = (u + 1.0) * g * _sigmoid(g * SWIGLU_ALPHA)
    y = _dot(act.astype(BF16), wd_ref[...]) + bd_ref[...]
    y_ref[...] = _pack_bf16_pair(y[:, :half], y[:, half:])


def moe_experts(h_packed, src_rows, tile_expert, wgu, bgu, wd, bd):
    tm = MOE_TM
    n_tiles = tile_expert.shape[0]
    half = h_packed.shape[1]
    d, f2 = wgu.shape[1], wgu.shape[2]
    src3 = src_rows.reshape(n_tiles, 1, tm)
    grid_spec = pltpu.PrefetchScalarGridSpec(
        num_scalar_prefetch=1,
        grid=(n_tiles,),
        in_specs=[pl.BlockSpec((None, 1, tm), lambda t, te: (t, 0, 0), memory_space=pltpu.SMEM),
                  pl.BlockSpec((None, 1, tm), lambda t, te: (jnp.minimum(t + 1, n_tiles - 1), 0, 0),
                               memory_space=pltpu.SMEM),
                  pl.BlockSpec(memory_space=pl.ANY),
                  pl.BlockSpec((None, d, f2), lambda t, te: (te[t], 0, 0)),
                  pl.BlockSpec((None, 1, f2), lambda t, te: (te[t], 0, 0)),
                  pl.BlockSpec((None, f2 // 2, d), lambda t, te: (te[t], 0, 0)),
                  pl.BlockSpec((None, 1, d), lambda t, te: (te[t], 0, 0))],
        out_specs=pl.BlockSpec((tm, half), lambda t, te: (t, 0)),
        scratch_shapes=[pltpu.VMEM((2, tm, half), h_packed.dtype), pltpu.SemaphoreType.DMA((2,))],
    )
    return pl.pallas_call(
        _moe_expert_kernel,
        grid_spec=grid_spec,
        out_shape=jax.ShapeDtypeStruct((n_tiles * tm, half), h_packed.dtype),
        compiler_params=_cparams(("arbitrary",)),
        name="moe_experts",
    )(tile_expert, src3, src3, h_packed, wgu, bgu, wd, bd)


def _moe_combine_kernel(dcur_ref, dnext_ref, wts_ref, x_ref, mod_ref, ys_hbm, o_ref, buf, sem, *, gi, n_ctx, tr):
    t = pl.program_id(0)
    slot = t % 2

    def gather(d_ref, s):
        def body(i, carry):
            for k in range(TOP_K):
                _row_copy(ys_hbm, d_ref[k, i], buf.at[s, k], i, sem.at[s]).start()
            return carry
        lax.fori_loop(0, tr, body, 0, unroll=4)

    @pl.when(t == 0)
    def _():
        gather(dcur_ref, 0)

    @pl.when(t + 1 < pl.num_programs(0))
    def _():
        gather(dnext_ref, 1 - slot)

    for k in range(TOP_K):
        pltpu.make_async_copy(ys_hbm.at[pl.ds(0, tr), :], buf.at[slot, k], sem.at[slot]).wait()

    half = buf.shape[-1]
    acc_lo = jnp.zeros((tr, half), F32)
    acc_hi = jnp.zeros((tr, half), F32)
    for k in range(TOP_K):
        lo, hi = _unpack_bf16_pair(buf[slot, k])
        w = wts_ref[:, k:k + 1]
        acc_lo = acc_lo + w * lo
        acc_hi = acc_hi + w * hi
    is_ctx = _row_is_ctx(t * tr, tr, n_ctx)
    gate = jnp.where(is_ctx, mod_ref[1, gi:gi + 1, :], mod_ref[0, gi:gi + 1, :])
    o_ref[:, :half] = x_ref[:, :half] + gate[:, :half] * acc_lo
    o_ref[:, half:] = x_ref[:, half:] + gate[:, half:] * acc_hi


def moe_combine(ys, dest_t, wts, x, mods, gi, n_ctx):
    tr = MOE_COMBINE_TR
    n, d = x.shape
    nt = n // tr
    return pl.pallas_call(
        functools.partial(_moe_combine_kernel, gi=gi, n_ctx=n_ctx, tr=tr),
        grid=(nt,),
        in_specs=[pl.BlockSpec((TOP_K, tr), lambda t: (0, t), memory_space=pltpu.SMEM),
                  pl.BlockSpec((TOP_K, tr), lambda t: (0, jnp.minimum(t + 1, nt - 1)), memory_space=pltpu.SMEM),
                  pl.BlockSpec((tr, TOP_K), lambda t: (t, 0)),
                  pl.BlockSpec((tr, d), lambda t: (t, 0)),
                  pl.BlockSpec((2, N_MOD, d), lambda t: (0, 0, 0)),
                  pl.BlockSpec(memory_space=pl.ANY)],
        out_specs=pl.BlockSpec((tr, d), lambda t: (t, 0)),
        out_shape=jax.ShapeDtypeStruct((n, d), F32),
        scratch_shapes=[pltpu.VMEM((2, TOP_K, tr, ys.shape[1]), ys.dtype), pltpu.SemaphoreType.DMA((2,))],
        compiler_params=_cparams(("arbitrary",)),
        name="moe_combine",
    )(dest_t, dest_t, wts, x, mods, ys)


def moe_routing_tables(idx, rank, counts, n_tiles, expert_base):
    tm = MOE_TM
    n = idx.shape[0]
    cnt = counts[0].astype(jnp.int32)
    padded = (cnt + tm - 1) // tm * tm
    ends = jnp.cumsum(padded)
    offs = ends - padded
    onehot = idx[:, :, None] == jnp.arange(N_EXPERTS, dtype=jnp.int32)
    dest = jnp.sum(jnp.where(onehot, offs, 0), axis=-1) + rank
    token = jnp.broadcast_to(jnp.arange(n, dtype=jnp.int32)[:, None], dest.shape)
    src_rows = jnp.zeros((n_tiles * tm,), jnp.int32).at[dest.reshape(-1)].set(token.reshape(-1))
    tile_start = jnp.arange(n_tiles, dtype=jnp.int32) * tm
    tile_expert = jnp.minimum(jnp.sum(tile_start[:, None] >= ends[None, :], axis=1), N_EXPERTS - 1)
    return dest.T, src_rows, tile_expert.astype(jnp.int32) + expert_base


def moe_ffn(x, norm_g, mods, router_w, router_b, wgu, bgu, wd, bd, expert_base, n_ctx):
    n = x.shape[0]
    n_tiles = -(-(TOP_K * n + N_EXPERTS * (MOE_TM - 1)) // MOE_TM)
    h_packed, idx, wts, rank, counts = norm_router(x, norm_g, mods, 3, router_w, router_b, n_ctx)
    dest_t, src_rows, tile_expert = moe_routing_tables(idx, rank, counts, n_tiles, expert_base)
    ys = moe_experts(h_packed, src_rows, tile_expert, wgu, bgu, wd, bd)
    return moe_combine(ys, dest_t, wts, x, mods, 5, n_ctx)


def _mm_tile(m):
    for tm in (1280, 1024, 640, 512, 256):
        if m % tm == 0:
            return tm
    raise ValueError(f"unsupported row count {m}")


def kernel(x, c, ctx, c_ctx, ada_w, ada_b, norm_mix, norm_ffn, w_in, w_out, hgrn_lb_logits, hgrn_norm,
           diff_lambda, diff_subln, na_rpb, router_w, router_b, expert_w_gu, expert_b_gu, expert_w_down,
           expert_b_down, final_norm_g):
    assert x.shape[0] == 1 and c.shape[0] == 1
    depth = ada_w.shape[0]
    n_ctx = ctx.shape[1]
    s_len = x.shape[1]
    n = n_ctx + s_len
    rows = s_len // GRID_W
    tm = _mm_tile(n)

    hall = jnp.concatenate([ctx[0], x[0]], axis=0)
    c8 = jnp.zeros((8, D_MODEL), F32).at[0].set(c[0]).at[1].set(c_ctx)
    mods_all = ada_mod(c8, ada_w, ada_b)[:, :2].reshape(depth, 2, N_MOD, D_MODEL)

    p_lb = jax.nn.softmax(hgrn_lb_logits.astype(F32), axis=0)
    lower_bounds = jnp.clip(jnp.cumsum(p_lb, axis=0) - p_lb[0], 0.0, 1.0)
    cos, sin = rope_tables(n, n_ctx)

    wgu_all = deinterleave_gate_up(expert_w_gu.reshape(depth * N_EXPERTS, D_MODEL, 2 * D_EXPERT))
    bgu_all = jnp.concatenate([expert_b_gu[..., 0::2], expert_b_gu[..., 1::2]], axis=-1)
    bgu_all = bgu_all.reshape(depth * N_EXPERTS, 1, 2 * D_EXPERT)
    wd_all = expert_w_down.reshape(depth * N_EXPERTS, D_EXPERT, D_MODEL).astype(BF16)
    bd_all = expert_b_down.reshape(depth * N_EXPERTS, 1, D_MODEL)
    tk = n // 13 if n % (13 * 256) == 0 else 256

    for layer in range(depth):
        lambda_init = 0.8 - 0.6 * math.exp(-0.3 * layer)
        mods = mods_all[layer]
        w_in_b = w_in[layer].astype(BF16)
        w_out_b = w_out[layer].astype(BF16)

        h = norm_mod(hall, norm_mix[layer], mods, 0, n_ctx)
        p = matmul(h, w_in_b, tm, 1024, BF16)

        ya = hgrn_mixer(p, lower_bounds[layer], hgrn_norm[layer], n_ctx)
        qr, kt = rope_qk(p, cos, sin)
        yb = diff_attention(qr, kt, p, diff_lambda[layer], diff_subln[layer], lambda_init,
                            n_ctx, s_len, n, 256, tk)
        yb = diff_attention(qr, kt, p, diff_lambda[layer], diff_subln[layer], lambda_init,
                            0, n_ctx, n_ctx, n_ctx, n_ctx, out_rows=yb)
        yc = na_attention(p, na_bias_tables(na_rpb[layer], rows), n_ctx)
        yc = na_ctx_attention(p, n_ctx, yc)
        hall = matmul_residual([ya, yb, yc], w_out_b, hall, mods, 2, n_ctx, tm, 512)
        hall = moe_ffn(hall, norm_ffn[layer], mods, router_w[layer], router_b[layer], wgu_all, bgu_all,
                       wd_all, bd_all, layer * N_EXPERTS, n_ctx)

    return final_norm(hall, final_norm_g, n_ctx)[None]
```

```python
import functools
import math

import numpy as np
import jax
import jax.numpy as jnp
from jax import lax
from jax.experimental import pallas as pl
from jax.experimental.pallas import tpu as pltpu

F32 = jnp.float32
BF16 = jnp.bfloat16

D_MODEL = 4096
HEAD_DIM = 128
GRID_W = 64
A_HEADS = 12
A_WIDTH = A_HEADS * HEAD_DIM
B_HEADS = 5
B_WIDTH = B_HEADS * 2 * HEAD_DIM
C_HEADS = 10
C_WIDTH = C_HEADS * HEAD_DIM
A_COLS = 5 * A_WIDTH
B_COLS = 3 * B_WIDTH
C_COLS = 3 * C_WIDTH
IN_COLS = A_COLS + B_COLS + C_COLS
NA_KH = 8
NA_KW = 16
ROPE_BASE = 10000.0
N_EXPERTS = 32
TOP_K = 4
D_EXPERT = 512
SWIGLU_ALPHA = 1.702
SWIGLU_LIMIT = 7.0
N_MOD = 6
NORM_EPS = 1e-6
F_TINY = 1e-30
MASK_NEG = -1e30
LOG2_E = 1.4426950408889634

V7X_VMEM_LIMIT_BYTES = 56 * 1024 * 1024

SCAN_CHUNK = 64
SCAN_SUB = 16
NA_QROWS = 4
NA_QBLK = NA_QROWS * GRID_W
NA_KBLKS = 3


def _cparams(sem):
    return pltpu.CompilerParams(dimension_semantics=sem, vmem_limit_bytes=V7X_VMEM_LIMIT_BYTES)


def _dot(a, b):
    return jnp.dot(a, b, preferred_element_type=F32)


def _dot_nt(a, b):
    return lax.dot_general(a, b, (((1,), (1,)), ((), ())), preferred_element_type=F32)


def _dot_tn(a, b):
    return lax.dot_general(a, b, (((0,), (0,)), ((), ())), preferred_element_type=F32)


def _sigmoid(x):
    return 1.0 / (1.0 + jnp.exp(-x))


def _row_is_ctx(row0, rows, n_ctx):
    return (row0 + lax.broadcasted_iota(jnp.int32, (rows, 1), 0)) < n_ctx


def _ada_kernel(c_ref, w_ref, b_ref, o_ref):
    c = c_ref[...]
    a = (c * _sigmoid(c)).astype(BF16)
    o_ref[...] = _dot(a, w_ref[...].astype(BF16)) + b_ref[...]


def ada_mod(c8, ada_w, ada_b, tn=512):
    depth, d, n = ada_w.shape
    return pl.pallas_call(
        _ada_kernel,
        grid=(depth, n // tn),
        in_specs=[pl.BlockSpec((8, d), lambda l, j: (0, 0)),
                  pl.BlockSpec((None, d, tn), lambda l, j: (l, 0, j)),
                  pl.BlockSpec((None, 1, tn), lambda l, j: (l, 0, j))],
        out_specs=pl.BlockSpec((None, 8, tn), lambda l, j: (l, 0, j)),
        out_shape=jax.ShapeDtypeStruct((depth, 8, n), F32),
        compiler_params=_cparams(("arbitrary", "arbitrary")),
        name="ada_mod",
    )(c8, ada_w, ada_b.reshape(depth, 1, n))


def _norm_mod(x, g, mod_ref, si, row0, n_ctx):
    ms = jnp.mean(x * x, axis=-1, keepdims=True)
    y = x * lax.rsqrt(ms + NORM_EPS) * g
    is_ctx = _row_is_ctx(row0, x.shape[0], n_ctx)
    shift = jnp.where(is_ctx, mod_ref[1, si:si + 1, :], mod_ref[0, si:si + 1, :])
    scale = jnp.where(is_ctx, mod_ref[1, si + 1:si + 2, :], mod_ref[0, si + 1:si + 2, :])
    return y * (1.0 + scale) + shift


def _norm_kernel(x_ref, g_ref, mod_ref, o_ref, *, si, n_ctx, tr):
    row0 = pl.program_id(0) * tr
    o_ref[...] = _norm_mod(x_ref[...], g_ref[...], mod_ref, si, row0, n_ctx).astype(o_ref.dtype)


def norm_mod(x, g, mods, si, n_ctx, tr=256):
    n, d = x.shape
    return pl.pallas_call(
        functools.partial(_norm_kernel, si=si, n_ctx=n_ctx, tr=tr),
        grid=(n // tr,),
        in_specs=[pl.BlockSpec((tr, d), lambda i: (i, 0)),
                  pl.BlockSpec((1, d), lambda i: (0, 0)),
                  pl.BlockSpec((2, N_MOD, d), lambda i: (0, 0, 0))],
        out_specs=pl.BlockSpec((tr, d), lambda i: (i, 0)),
        out_shape=jax.ShapeDtypeStruct((n, d), BF16),
        compiler_params=_cparams(("parallel",)),
        name="norm_mod",
    )(x, g.reshape(1, d), mods)


def _split_bf16(x):
    hi = x.astype(BF16)
    lo = (x - hi.astype(F32)).astype(BF16)
    return hi, lo


def _pack_bf16_pair(lo, hi):
    lo_b = lax.bitcast_convert_type(lo.astype(BF16).astype(F32), jnp.uint32) >> 16
    hi_b = lax.bitcast_convert_type(hi.astype(BF16).astype(F32), jnp.uint32) & jnp.uint32(0xFFFF0000)
    return hi_b | lo_b


def _unpack_bf16_pair(w):
    lo = lax.bitcast_convert_type(w << 16, F32)
    hi = lax.bitcast_convert_type(w & jnp.uint32(0xFFFF0000), F32)
    return lo, hi


def _norm_router_kernel(x_ref, g_ref, mod_ref, rw_ref, rb_ref, h_ref, idx_ref, wts_ref, rank_ref, cnt_ref,
                        carry_ref, *, si, n_ctx, tr):
    @pl.when(pl.program_id(0) == 0)
    def _():
        carry_ref[...] = jnp.zeros_like(carry_ref)

    row0 = pl.program_id(0) * tr
    h = _norm_mod(x_ref[...], g_ref[...], mod_ref, si, row0, n_ctx)
    half = h.shape[1] // 2
    h_ref[...] = _pack_bf16_pair(h[:, :half], h[:, half:])
    hh, hl = _split_bf16(h)
    wh, wl = _split_bf16(rw_ref[...])
    logits = _dot(hh, wh) + _dot(hh, wl) + _dot(hl, wh) + rb_ref[...]
    iota = lax.broadcasted_iota(jnp.int32, logits.shape, 1)
    work = logits
    vals, sels = [], []
    for k in range(TOP_K):
        m = jnp.max(work, axis=-1, keepdims=True)
        idx = jnp.min(jnp.where(work == m, iota, N_EXPERTS), axis=-1, keepdims=True)
        sel = iota == idx
        vals.append(m)
        sels.append(sel)
        idx_ref[:, k:k + 1] = idx
        work = jnp.where(sel, -jnp.inf, work)
    es = [jnp.exp(v - vals[0]) for v in vals]
    tot = es[0] + es[1] + es[2] + es[3]
    for k in range(TOP_K):
        wts_ref[:, k:k + 1] = es[k] / tot
    used = jnp.zeros_like(logits)
    for k in range(TOP_K):
        used = used + jnp.where(sels[k], 1.0, 0.0)
    r = lax.broadcasted_iota(jnp.int32, (tr, tr), 0)
    c = lax.broadcasted_iota(jnp.int32, (tr, tr), 1)
    before = jnp.where(c < r, 1.0, 0.0).astype(BF16)
    prefix = _dot(before, used.astype(BF16)) + carry_ref[...]
    for k in range(TOP_K):
        rank = jnp.sum(jnp.where(sels[k], prefix, 0.0), axis=-1, keepdims=True)
        rank_ref[:, k:k + 1] = rank.astype(jnp.int32)
    carry_ref[...] += jnp.sum(used, axis=0, keepdims=True)
    cnt_ref[...] = carry_ref[...]


def norm_router(x, g, mods, si, router_w, router_b, n_ctx, tr=256):
    n, d = x.shape
    col = lambda i: (i, 0)
    fixed = lambda i: (0, 0)
    return pl.pallas_call(
        functools.partial(_norm_router_kernel, si=si, n_ctx=n_ctx, tr=tr),
        grid=(n // tr,),
        in_specs=[pl.BlockSpec((tr, d), col),
                  pl.BlockSpec((1, d), fixed),
                  pl.BlockSpec((2, N_MOD, d), lambda i: (0, 0, 0)),
                  pl.BlockSpec((d, N_EXPERTS), fixed),
                  pl.BlockSpec((1, N_EXPERTS), fixed)],
        out_specs=[pl.BlockSpec((tr, d // 2), col),
                   pl.BlockSpec((tr, TOP_K), col),
                   pl.BlockSpec((tr, TOP_K), col),
                   pl.BlockSpec((tr, TOP_K), col),
                   pl.BlockSpec((1, N_EXPERTS), fixed)],
        out_shape=[jax.ShapeDtypeStruct((n, d // 2), jnp.uint32),
                   jax.ShapeDtypeStruct((n, TOP_K), jnp.int32),
                   jax.ShapeDtypeStruct((n, TOP_K), F32),
                   jax.ShapeDtypeStruct((n, TOP_K), jnp.int32),
                   jax.ShapeDtypeStruct((1, N_EXPERTS), F32)],
        scratch_shapes=[pltpu.VMEM((1, N_EXPERTS), F32)],
        compiler_params=_cparams(("arbitrary",)),
        name="norm_router",
    )(x, g.reshape(1, d), mods, router_w, router_b.reshape(1, N_EXPERTS))


def _final_norm_kernel(x_ref, g_ref, o_ref):
    x = x_ref[...]
    ms = jnp.mean(x * x, axis=-1, keepdims=True)
    o_ref[...] = x * lax.rsqrt(ms + NORM_EPS) * g_ref[...]


def final_norm(x, g, n_ctx, tr=256):
    n, d = x.shape
    skip = n_ctx // tr
    return pl.pallas_call(
        _final_norm_kernel,
        grid=((n - n_ctx) // tr,),
        in_specs=[pl.BlockSpec((tr, d), lambda i: (i + skip, 0)),
                  pl.BlockSpec((1, d), lambda i: (0, 0))],
        out_specs=pl.BlockSpec((tr, d), lambda i: (i, 0)),
        out_shape=jax.ShapeDtypeStruct((n - n_ctx, d), F32),
        compiler_params=_cparams(("parallel",)),
        name="final_norm",
    )(x, g.reshape(1, d))


def _mm_kernel(a_ref, b_ref, o_ref):
    o_ref[...] = _dot(a_ref[...], b_ref[...]).astype(o_ref.dtype)


def matmul(a, b, tm, tn, out_dtype):
    m, k = a.shape
    n = b.shape[1]
    return pl.pallas_call(
        _mm_kernel,
        grid=(m // tm, n // tn),
        in_specs=[pl.BlockSpec((tm, k), lambda i, j: (i, 0)),
                  pl.BlockSpec((k, tn), lambda i, j: (0, j))],
        out_specs=pl.BlockSpec((tm, tn), lambda i, j: (i, j)),
        out_shape=jax.ShapeDtypeStruct((m, n), out_dtype),
        compiler_params=_cparams(("parallel", "arbitrary")),
        name="matmul",
    )(a, b)


def _mm_res_kernel(*refs, gi, n_ctx, tm, widths):
    a_refs = refs[:len(widths)]
    b_ref, x_ref, mod_ref, o_ref = refs[len(widths):]
    row0 = pl.program_id(0) * tm
    is_ctx = _row_is_ctx(row0, tm, n_ctx)
    gate = jnp.where(is_ctx, mod_ref[1, gi:gi + 1, :], mod_ref[0, gi:gi + 1, :])
    acc = None
    k0 = 0
    for a_ref, w in zip(a_refs, widths):
        part = _dot(a_ref[...], b_ref[k0:k0 + w, :])
        acc = part if acc is None else acc + part
        k0 += w
    o_ref[...] = x_ref[...] + gate * acc


def matmul_residual(a_parts, b, x, mods, gi, n_ctx, tm, tn):
    m = x.shape[0]
    k, n = b.shape
    widths = tuple(a.shape[1] for a in a_parts)
    assert sum(widths) == k
    return pl.pallas_call(
        functools.partial(_mm_res_kernel, gi=gi, n_ctx=n_ctx, tm=tm, widths=widths),
        grid=(m // tm, n // tn),
        in_specs=[pl.BlockSpec((tm, w), lambda i, j: (i, 0)) for w in widths]
        + [pl.BlockSpec((k, tn), lambda i, j: (0, j)),
           pl.BlockSpec((tm, tn), lambda i, j: (i, j)),
           pl.BlockSpec((2, N_MOD, tn), lambda i, j: (0, 0, j))],
        out_specs=pl.BlockSpec((tm, tn), lambda i, j: (i, j)),
        out_shape=jax.ShapeDtypeStruct((m, n), F32),
        compiler_params=_cparams(("parallel", "arbitrary")),
        name="matmul_residual",
    )(*a_parts, b, x, mods)


def _hgrn_masks(c, sub, reverse):
    r = lax.broadcasted_iota(jnp.int32, (c, c), 0)
    s = lax.broadcasted_iota(jnp.int32, (c, c), 1)
    tri = (s >= r) if reverse else (s <= r)
    same_sub = (r // sub) == (s // sub)
    return tri, same_sub


def _hgrn_head(q_raw, f_pre, v, lb, st_in, ck_ref, *, reverse, sel, tri_b, same_sub):
    c, d = q_raw.shape
    sub = SCAN_SUB
    nsub = c // sub
    q = q_raw * _sigmoid(q_raw) * (HEAD_DIM ** -0.5)
    f = lb + (1.0 - lb) * _sigmoid(f_pre)
    log_f = jnp.log2(jnp.maximum(f, F_TINY))
    k = (1.0 - lb) * _sigmoid(-f_pre)

    hi = log_f.astype(BF16)
    r1 = log_f - hi.astype(F32)
    mid = r1.astype(BF16)
    lo = (r1 - mid.astype(F32)).astype(BF16)
    yield
    cum = _dot(tri_b, hi) + _dot(tri_b, mid) + _dot(tri_b, lo)
    yield
    cum_k = cum - jnp.log2(k)
    ck_ref[...] = cum_k

    order = list(range(nsub))
    if reverse:
        order = order[::-1]
    zero_row = jnp.zeros((1, d), F32)
    bnd_in, bnd_out = {}, {}
    prev = zero_row
    for i in order:
        last = i * sub if reverse else i * sub + sub - 1
        bnd_in[i] = prev
        bnd_out[i] = cum[last:last + 1, :]
        prev = bnd_out[i]
    cum_end = prev

    t_in_sub = lax.broadcasted_iota(jnp.int32, (sub, 1), 0)
    q_hat, u_rows, off_rows = [], [], []
    for i in range(nsub):
        sl = slice(i * sub, (i + 1) * sub)
        q_i, cum_i = q[sl], cum[sl]
        q_hat_i = q_i * jnp.exp2(cum_i - bnd_in[i])
        k_parts = []
        for j in range(nsub):
            earlier = (j > i) if reverse else (j < i)
            sj = slice(j * sub, (j + 1) * sub)
            if earlier:
                k_parts.append(jnp.exp2(bnd_in[i] - cum_k[sj]))
            else:
                k_parts.append(jnp.zeros((sub, d), F32))
        k_til = jnp.concatenate(k_parts, axis=0).astype(BF16)
        off_rows.append(_dot_nt(q_hat_i.astype(BF16), k_til))
        yield
        u_parts = []
        for s in range(sub):
            groups = []
            for g8 in range(sub // 8):
                rows = slice(g8 * 8, g8 * 8 + 8)
                if g8 == s // 8:
                    rows_t = t_in_sub[rows]
                    ok = (rows_t <= s) if reverse else (rows_t >= s)
                    e = jnp.where(ok, cum_i[rows] - ck_ref[i * sub + s:i * sub + s + 1, :], MASK_NEG)
                elif (g8 < s // 8) == reverse:
                    e = cum_i[rows] - ck_ref[i * sub + s:i * sub + s + 1, :]
                else:
                    groups.append(jnp.zeros((8, d), BF16))
                    continue
                groups.append((q_i[rows] * jnp.exp2(e)).astype(BF16))
            u_parts.append(jnp.concatenate(groups, axis=0))
        u_rows.append(jnp.concatenate(u_parts, axis=1))
        yield
    u = jnp.concatenate(u_rows, axis=0)
    diag = _dot(u, sel)
    v_b = v.astype(BF16)
    k_end = jnp.exp2(cum_end - cum_k).astype(BF16)
    st_out = st_in * jnp.exp2(cum_end) + _dot_tn(v_b, k_end)
    o_state = _dot_nt((q * jnp.exp2(cum)).astype(BF16), st_in.astype(BF16))
    yield
    attn = jnp.concatenate(off_rows, axis=0) + jnp.where(same_sub, diag, 0.0)
    o = _dot(attn.astype(BF16), v_b) + o_state
    return o, st_out


def _run_interleaved(gens):
    results = [None] * len(gens)
    live = list(range(len(gens)))
    while live:
        for idx in list(live):
            try:
                next(gens[idx])
            except StopIteration as done:
                results[idx] = done.value
                live.remove(idx)
    return results


def _hgrn_sel(c, sub):
    rows = np.arange(sub * HEAD_DIM) // HEAD_DIM
    cols = np.arange(c) % sub
    return jnp.asarray((rows[:, None] == cols[None, :]).astype(np.float32), dtype=BF16)


def _hgrn_fw_kernel(q_ref, f_ref, v_ref, lb_ref, sel_ref, o_ref, st_ref, ck_ref, *, hb):
    @pl.when(pl.program_id(1) == 0)
    def _():
        st_ref[...] = jnp.zeros_like(st_ref)

    sel = sel_ref[...]
    tri, same_sub = _hgrn_masks(q_ref.shape[0], SCAN_SUB, False)
    tri_b = jnp.where(tri, 1.0, 0.0).astype(BF16)
    cols = [slice(h * HEAD_DIM, (h + 1) * HEAD_DIM) for h in range(hb)]
    heads = _run_interleaved([
        _hgrn_head(q_ref[:, cs].astype(F32), f_ref[:, cs].astype(F32), v_ref[:, cs].astype(F32),
                   lb_ref[0:1, cs], st_ref[h], ck_ref.at[h], reverse=False, sel=sel, tri_b=tri_b,
                   same_sub=same_sub)
        for h, cs in enumerate(cols)])
    for h, (cs, (o, st)) in enumerate(zip(cols, heads)):
        st_ref[h] = st
        o_ref[:, cs] = o


def _hgrn_bw_kernel(q_ref, f_ref, v_ref, g_ref, ofw_ref, lb_ref, ng_ref, sel_ref, y_ref, st_ref, ck_ref, *, hb):
    @pl.when(pl.program_id(1) == 0)
    def _():
        st_ref[...] = jnp.zeros_like(st_ref)

    sel = sel_ref[...]
    tri, same_sub = _hgrn_masks(q_ref.shape[0], SCAN_SUB, True)
    tri_b = jnp.where(tri, 1.0, 0.0).astype(BF16)
    cols = [slice(h * HEAD_DIM, (h + 1) * HEAD_DIM) for h in range(hb)]
    heads = _run_interleaved([
        _hgrn_head(q_ref[:, cs].astype(F32), f_ref[:, cs].astype(F32), v_ref[:, cs].astype(F32),
                   lb_ref[1:2, cs], st_ref[h], ck_ref.at[h], reverse=True, sel=sel, tri_b=tri_b,
                   same_sub=same_sub)
        for h, cs in enumerate(cols)])
    for h, (cs, (o, st)) in enumerate(zip(cols, heads)):
        st_ref[h] = st
        o = o + ofw_ref[:, cs]
        ms = jnp.mean(o * o, axis=-1, keepdims=True)
        y = o * lax.rsqrt(ms + NORM_EPS) * ng_ref[...]
        g = g_ref[:, cs].astype(F32)
        y_ref[:, cs] = (y * (g * _sigmoid(g))).astype(y_ref.dtype)


def hgrn_mixer(p, lb, norm_g, n_ctx, hb=A_HEADS):
    n = p.shape[0]
    c = SCAN_CHUNK
    nc = n // c
    nc_ctx = n_ctx // c
    ngroups = A_HEADS // hb
    w = hb * HEAD_DIM
    per = A_WIDTH // w
    sel = _hgrn_sel(c, SCAN_SUB)

    def fw_chunk(s):
        return s

    def bw_chunk(s):
        return jnp.where(s < nc_ctx, nc_ctx - 1 - s, nc - 1 - (s - nc_ctx))

    def col(section, chunk_of):
        return pl.BlockSpec((c, w), lambda g, s: (chunk_of(s), section * per + g))

    lb_spec = pl.BlockSpec((2, w), lambda g, s: (0, g))
    sel_spec = pl.BlockSpec(sel.shape, lambda g, s: (0, 0))
    scratch = [pltpu.VMEM((hb, HEAD_DIM, HEAD_DIM), F32), pltpu.VMEM((hb, c, HEAD_DIM), F32)]
    o_fw = pl.pallas_call(
        functools.partial(_hgrn_fw_kernel, hb=hb),
        grid=(ngroups, nc),
        in_specs=[col(0, fw_chunk), col(1, fw_chunk), col(3, fw_chunk), lb_spec, sel_spec],
        out_specs=pl.BlockSpec((c, w), lambda g, s: (s, g)),
        out_shape=jax.ShapeDtypeStruct((n, A_WIDTH), F32),
        scratch_shapes=scratch,
        compiler_params=_cparams(("parallel", "arbitrary")),
        name="hgrn_fw",
    )(p, p, p, lb, sel)
    return pl.pallas_call(
        functools.partial(_hgrn_bw_kernel, hb=hb),
        grid=(ngroups, nc),
        in_specs=[col(0, bw_chunk), col(2, bw_chunk), col(3, bw_chunk), col(4, bw_chunk),
                  pl.BlockSpec((c, w), lambda g, s: (bw_chunk(s), g)),
                  lb_spec, pl.BlockSpec((1, HEAD_DIM), lambda g, s: (0, 0)), sel_spec],
        out_specs=pl.BlockSpec((c, w), lambda g, s: (bw_chunk(s), g)),
        out_shape=jax.ShapeDtypeStruct((n, A_WIDTH), BF16),
        scratch_shapes=scratch,
        compiler_params=_cparams(("parallel", "arbitrary")),
        name="hgrn_bw",
    )(p, p, p, p, o_fw, lb, norm_g.reshape(1, HEAD_DIM), sel)


def rope_tables(n, n_ctx):
    t = jnp.arange(n - n_ctx, dtype=jnp.int32)
    half = HEAD_DIM // 4
    inv_freq = ROPE_BASE ** (-jnp.arange(half, dtype=F32) / half)
    ang_r = (t // GRID_W).astype(F32)[:, None] * inv_freq
    ang_c = (t % GRID_W).astype(F32)[:, None] * inv_freq
    cos = jnp.concatenate([jnp.cos(ang_r)] * 2 + [jnp.cos(ang_c)] * 2, axis=-1)
    sin = jnp.concatenate([-jnp.sin(ang_r), jnp.sin(ang_r), -jnp.sin(ang_c), jnp.sin(ang_c)], axis=-1)
    cos = jnp.concatenate([jnp.ones((n_ctx, HEAD_DIM), F32), cos], axis=0)
    sin = jnp.concatenate([jnp.zeros((n_ctx, HEAD_DIM), F32), sin], axis=0)
    return cos, sin


def _rope_kernel(q_ref, k_ref, cos_ref, sin_ref, qo_ref, kt_ref):
    cos = cos_ref[...]
    sin = sin_ref[...]
    lane = lax.broadcasted_iota(jnp.int32, cos.shape, 1)
    low = (lane & (HEAD_DIM // 4)) == 0

    def rotate(x):
        partner = jnp.where(low, pltpu.roll(x, HEAD_DIM - HEAD_DIM // 4, 1), pltpu.roll(x, HEAD_DIM // 4, 1))
        return x * cos + partner * sin

    for h in range(2 * B_HEADS):
        cs = slice(h * HEAD_DIM, (h + 1) * HEAD_DIM)
        qo_ref[:, cs] = (rotate(q_ref[:, cs].astype(F32)) * (LOG2_E * HEAD_DIM ** -0.5)).astype(qo_ref.dtype)
        kt_ref[cs, :] = rotate(k_ref[:, cs].astype(F32)).T.astype(kt_ref.dtype)


def rope_qk(p, cos, sin, tr=256):
    n = p.shape[0]
    qblk = A_COLS // B_WIDTH
    return pl.pallas_call(
        _rope_kernel,
        grid=(n // tr,),
        in_specs=[pl.BlockSpec((tr, B_WIDTH), lambda i: (i, qblk)),
                  pl.BlockSpec((tr, B_WIDTH), lambda i: (i, qblk + 1)),
                  pl.BlockSpec((tr, HEAD_DIM), lambda i: (i, 0)),
                  pl.BlockSpec((tr, HEAD_DIM), lambda i: (i, 0))],
        out_specs=[pl.BlockSpec((tr, B_WIDTH), lambda i: (i, 0)),
                   pl.BlockSpec((B_WIDTH, tr), lambda i: (0, i))],
        out_shape=[jax.ShapeDtypeStruct((n, B_WIDTH), BF16), jax.ShapeDtypeStruct((B_WIDTH, n), BF16)],
        compiler_params=_cparams(("parallel",)),
        name="rope_qk",
    )(p, p, cos, sin)


def _diff_attn_kernel(lam_ref, g_ref, q_ref, kt_ref, v_ref, *rest, lambda_init, nk, tk):
    o_ref, m_scr, l_scr, acc_scr, sa_scr, sb_scr = rest[-6:]
    tq = q_ref.shape[0]
    q = q_ref[...]
    q1 = q[:, :HEAD_DIM]
    q2 = q[:, HEAD_DIM:]
    m_scr[...] = jnp.full_like(m_scr, -jnp.inf)
    l_scr[...] = jnp.zeros_like(l_scr)
    acc_scr[...] = jnp.zeros_like(acc_scr)

    def scores(j, s_scr):
        off = pl.multiple_of(j * tk, 128)
        s_scr[:tq, :] = _dot(q1, kt_ref[:HEAD_DIM, pl.ds(off, tk)])
        s_scr[tq:, :] = _dot(q2, kt_ref[HEAD_DIM:, pl.ds(off, tk)])

    def accumulate(j, s_scr):
        off = pl.multiple_of(j * tk, tk)
        s = s_scr[...]
        m_prev = m_scr[...]
        m_new = jnp.maximum(m_prev, jnp.max(s, axis=-1, keepdims=True))
        alpha = jnp.exp2(m_prev - m_new)
        p = jnp.exp2(s - m_new)
        l_scr[...] = alpha * l_scr[...] + jnp.sum(p, axis=-1, keepdims=True)
        acc_scr[...] = alpha * acc_scr[...] + _dot(p.astype(BF16), v_ref[pl.ds(off, tk), :])
        m_scr[...] = m_new

    scores(0, sa_scr)

    def pair(jj, carry):
        j = 2 * jj
        scores(j + 1, sb_scr)
        accumulate(j, sa_scr)
        scores(jnp.minimum(j + 2, nk - 1), sa_scr)
        accumulate(j + 1, sb_scr)
        return carry

    lax.fori_loop(0, nk // 2, pair, 0)
    if nk % 2:
        accumulate(nk - 1, sa_scr)

    lp = lam_ref[...]
    lam = (jnp.exp(jnp.sum(lp[0:1] * lp[1:2], axis=-1, keepdims=True))
           - jnp.exp(jnp.sum(lp[2:3] * lp[3:4], axis=-1, keepdims=True)) + lambda_init)
    o = acc_scr[:tq] / l_scr[:tq] - lam * (acc_scr[tq:] / l_scr[tq:])
    ms = jnp.mean(o * o, axis=-1, keepdims=True)
    y = o * lax.rsqrt(ms + NORM_EPS) * g_ref[...] * (1.0 - lambda_init)
    o_ref[...] = y.astype(o_ref.dtype)


def diff_attention(q, kt, p, lam_p, subln_g, lambda_init, q_row0, n_q, n_kv, tq, tk, out_rows=None):
    hw = 2 * HEAD_DIM
    n = q.shape[0]
    nq, nk = n_q // tq, n_kv // tk
    vblk = (A_COLS + 2 * B_WIDTH) // hw
    qb0 = q_row0 // tq
    q_spec = pl.BlockSpec((tq, hw), lambda h, i: (i + qb0, h))
    in_specs = [pl.BlockSpec((4, HEAD_DIM), lambda h, i: (0, 0)),
                pl.BlockSpec((1, hw), lambda h, i: (0, 0)),
                q_spec,
                pl.BlockSpec((hw, n_kv), lambda h, i: (h, 0)),
                pl.BlockSpec((n_kv, hw), lambda h, i: (0, vblk + h))]
    args = [lam_p, subln_g.reshape(1, hw), q, kt, p]
    aliases = {}
    if out_rows is not None:
        in_specs.append(pl.BlockSpec(memory_space=pl.ANY))
        args.append(out_rows)
        aliases = {len(args) - 1: 0}
    return pl.pallas_call(
        functools.partial(_diff_attn_kernel, lambda_init=lambda_init, nk=nk, tk=tk),
        grid=(B_HEADS, nq),
        in_specs=in_specs,
        out_specs=q_spec,
        out_shape=jax.ShapeDtypeStruct((n, B_WIDTH), BF16),
        scratch_shapes=[pltpu.VMEM((2 * tq, 1), F32), pltpu.VMEM((2 * tq, 1), F32),
                        pltpu.VMEM((2 * tq, hw), F32),
                        pltpu.VMEM((2 * tq, tk), F32), pltpu.VMEM((2 * tq, tk), F32)],
        input_output_aliases=aliases,
        compiler_params=_cparams(("parallel", "arbitrary")),
        name="diff_attention",
    )(*args)


def na_bias_tables(rpb, rows):
    krows = NA_KBLKS * NA_QROWS
    n_dr, n_dc = 2 * NA_KH - 1, 2 * NA_KW - 1
    qc = np.arange(GRID_W)[:, None]
    kc = np.arange(GRID_W)[None, :]
    cs = np.clip(qc - NA_KW // 2, 0, GRID_W - NA_KW)
    col_ok = (kc >= cs) & (kc < cs + NA_KW)
    dc = np.clip(kc - qc + NA_KW - 1, 0, n_dc - 1)
    onehot = (np.arange(n_dc)[:, None, None] == dc[None]).astype(np.float32).reshape(n_dc, GRID_W * GRID_W)
    toep = jnp.dot(rpb.astype(F32).reshape(C_HEADS * n_dr, n_dc), jnp.asarray(onehot),
                   precision=lax.Precision.HIGHEST).reshape(C_HEADS, n_dr, GRID_W, GRID_W)
    toep = jnp.where(jnp.asarray(col_ok)[None, None], toep, MASK_NEG)
    masked = jnp.full((C_HEADS, GRID_W, GRID_W), MASK_NEG, F32)
    cases = [(0, 0), (NA_QROWS, 0), (rows - NA_QROWS, rows - krows)]
    out = []
    for r0, start in cases:
        q_blocks = []
        for qr in range(NA_QROWS):
            r = r0 + qr
            rs = min(max(r - NA_KH // 2, 0), rows - NA_KH)
            k_blocks = []
            for kr in range(krows):
                kra = start + kr
                k_blocks.append(toep[:, kra - r + NA_KH - 1] if rs <= kra < rs + NA_KH else masked)
            q_blocks.append(jnp.concatenate(k_blocks, axis=-1))
        out.append(jnp.concatenate(q_blocks, axis=1))
    return jnp.stack(out)


def _na_kernel(q_ref, k0_ref, k1_ref, k2_ref, kc_ref, v0_ref, v1_ref, v2_ref, vc_ref, bias_ref, o_ref):
    scale = HEAD_DIM ** -0.5
    k_refs = (k0_ref, k1_ref, k2_ref)
    v_refs = (v0_ref, v1_ref, v2_ref, vc_ref)
    for h in range(C_HEADS):
        cs = slice(h * HEAD_DIM, (h + 1) * HEAD_DIM)
        q = q_ref[:, cs]
        s_parts = [_dot_nt(q, k_refs[b][:, cs]) * scale + bias_ref[h, :, b * NA_QBLK:(b + 1) * NA_QBLK]
                   for b in range(NA_KBLKS)]
        s_parts.append(_dot_nt(q, kc_ref[:, cs]) * scale)
        m = s_parts[0].max(axis=-1, keepdims=True)
        for s in s_parts[1:]:
            m = jnp.maximum(m, s.max(axis=-1, keepdims=True))
        l = jnp.zeros_like(m)
        acc = jnp.zeros((q.shape[0], HEAD_DIM), F32)
        for s, v_ref in zip(s_parts, v_refs):
            pexp = jnp.exp(s - m)
            l = l + pexp.sum(axis=-1, keepdims=True)
            acc = acc + _dot(pexp.astype(BF16), v_ref[:, cs])
        o_ref[:, cs] = (acc / l).astype(o_ref.dtype)


def na_attention(p, bias, n_ctx):
    n = p.shape[0]
    nqb = (n - n_ctx) // NA_QBLK
    cb = n_ctx // NA_QBLK
    qblk = (A_COLS + B_COLS) // C_WIDTH

    def kv_spec(section, b):
        return pl.BlockSpec((NA_QBLK, C_WIDTH),
                            lambda i: (cb + jnp.clip(i - 1, 0, nqb - NA_KBLKS) + b, qblk + section))

    def ctx_spec(section):
        return pl.BlockSpec((n_ctx, C_WIDTH), lambda i: (0, qblk + section))

    def case(i):
        return jnp.where(i == 0, 0, jnp.where(i == nqb - 1, 2, 1))

    return pl.pallas_call(
        _na_kernel,
        grid=(nqb,),
        in_specs=[pl.BlockSpec((NA_QBLK, C_WIDTH), lambda i: (cb + i, qblk)),
                  kv_spec(1, 0), kv_spec(1, 1), kv_spec(1, 2), ctx_spec(1),
                  kv_spec(2, 0), kv_spec(2, 1), kv_spec(2, 2), ctx_spec(2),
                  pl.BlockSpec((None, C_HEADS, NA_QBLK, NA_KBLKS * NA_QBLK), lambda i: (case(i), 0, 0, 0))],
        out_specs=pl.BlockSpec((NA_QBLK, C_WIDTH), lambda i: (cb + i, 0)),
        out_shape=jax.ShapeDtypeStruct((n, C_WIDTH), BF16),
        compiler_params=_cparams(("parallel",)),
        name="na_attention",
    )(p, p, p, p, p, p, p, p, p, bias)


def _ctx_attn_kernel(q_ref, k_ref, v_ref, rows_hbm, o_ref):
    del rows_hbm
    scale = HEAD_DIM ** -0.5
    for h in range(C_HEADS):
        cs = slice(h * HEAD_DIM, (h + 1) * HEAD_DIM)
        s = _dot_nt(q_ref[:, cs], k_ref[:, cs]) * scale
        pexp = jnp.exp(s - s.max(axis=-1, keepdims=True))
        o = _dot(pexp.astype(BF16), v_ref[:, cs]) / pexp.sum(axis=-1, keepdims=True)
        o_ref[:, cs] = o.astype(o_ref.dtype)


def na_ctx_attention(p, n_ctx, out_rows):
    qblk = (A_COLS + B_COLS) // C_WIDTH
    return pl.pallas_call(
        _ctx_attn_kernel,
        grid=(1,),
        in_specs=[pl.BlockSpec((n_ctx, C_WIDTH), lambda i, s=s: (0, qblk + s)) for s in range(3)]
        + [pl.BlockSpec(memory_space=pl.ANY)],
        out_specs=pl.BlockSpec((n_ctx, C_WIDTH), lambda i: (0, 0)),
        out_shape=jax.ShapeDtypeStruct(out_rows.shape, BF16),
        input_output_aliases={3: 0},
        compiler_params=_cparams(("arbitrary",)),
        name="na_ctx_attention",
    )(p, p, p, out_rows)


MOE_TM = 256
MOE_COMBINE_TR = 128


DEINT_BLOCK = 256


def _deint_kernel(w_ref, p_ref, o_ref):
    f2 = w_ref.shape[1]
    hb = DEINT_BLOCK // 2
    perm = p_ref[...]
    for j in range(f2 // DEINT_BLOCK):
        part = _dot(w_ref[:, j * DEINT_BLOCK:(j + 1) * DEINT_BLOCK].astype(BF16), perm).astype(o_ref.dtype)
        o_ref[:, j * hb:(j + 1) * hb] = part[:, :hb]
        o_ref[:, f2 // 2 + j * hb:f2 // 2 + (j + 1) * hb] = part[:, hb:]


def deinterleave_gate_up(w_gu, tk=1024):
    g, d, f2 = w_gu.shape
    src = np.concatenate([np.arange(0, DEINT_BLOCK, 2), np.arange(1, DEINT_BLOCK, 2)])
    perm = jnp.asarray((np.arange(DEINT_BLOCK)[:, None] == src[None, :]).astype(np.float32), dtype=BF16)
    return pl.pallas_call(
        _deint_kernel,
        grid=(g, d // tk),
        in_specs=[pl.BlockSpec((None, tk, f2), lambda e, i: (e, i, 0)),
                  pl.BlockSpec((DEINT_BLOCK, DEINT_BLOCK), lambda e, i: (0, 0))],
        out_specs=pl.BlockSpec((None, tk, f2), lambda e, i: (e, i, 0)),
        out_shape=jax.ShapeDtypeStruct((g, d, f2), BF16),
        compiler_params=_cparams(("parallel", "parallel")),
        name="deinterleave_gate_up",
    )(w_gu, perm)


def _row_copy(src_hbm, src_row, dst, dst_row, sem):
    return pltpu.make_async_copy(src_hbm.at[pl.ds(src_row, 1), :], dst.at[pl.ds(dst_row, 1), :], sem)


def _moe_expert_kernel(te_ref, src_ref, next_ref, h_hbm, wgu_ref, bgu_ref, wd_ref, bd_ref, y_ref, buf, sem):
    t = pl.program_id(0)
    slot = t % 2
    tm = src_ref.shape[1]

    def gather(idx_ref, s):
        def body(r, carry):
            _row_copy(h_hbm, idx_ref[0, r], buf.at[s], r, sem.at[s]).start()
            return carry
        lax.fori_loop(0, tm, body, 0, unroll=8)

    @pl.when(t == 0)
    def _():
        gather(src_ref, 0)

    @pl.when(t + 1 < pl.num_programs(0))
    def _():
        gather(next_ref, 1 - slot)

    pltpu.make_async_copy(h_hbm.at[pl.ds(0, tm), :], buf.at[slot], sem.at[slot]).wait()

    lo, hi = _unpack_bf16_pair(buf[slot])
    half = lo.shape[1]
    gu = (_dot(lo.astype(BF16), wgu_ref[:half, :]) + _dot(hi.astype(BF16), wgu_ref[half:, :])
          + bgu_ref[...])
    f = gu.shape[1] // 2
    g = jnp.minimum(gu[:, :f], SWIGLU_LIMIT)
    u = jnp.clip(gu[:, f:], -SWIGLU_LIMIT, SWIGLU_LIMIT)
    act = (u + 1.0) * g * _sigmoid(g * SWIGLU_ALPHA)
    y = _dot(act.astype(BF16), wd_ref[...]) + bd_ref[...]
    y_ref[...] = _pack_bf16_pair(y[:, :half], y[:, half:])


def moe_experts(h_packed, src_rows, tile_expert, wgu, bgu, wd, bd):
    tm = MOE_TM
    n_tiles = tile_expert.shape[0]
    half = h_packed.shape[1]
    d, f2 = wgu.shape[1], wgu.shape[2]
    src3 = src_rows.reshape(n_tiles, 1, tm)
    grid_spec = pltpu.PrefetchScalarGridSpec(
        num_scalar_prefetch=1,
        grid=(n_tiles,),
        in_specs=[pl.BlockSpec((None, 1, tm), lambda t, te: (t, 0, 0), memory_space=pltpu.SMEM),
                  pl.BlockSpec((None, 1, tm), lambda t, te: (jnp.minimum(t + 1, n_tiles - 1), 0, 0),
                               memory_space=pltpu.SMEM),
                  pl.BlockSpec(memory_space=pl.ANY),
                  pl.BlockSpec((None, d, f2), lambda t, te: (te[t], 0, 0)),
                  pl.BlockSpec((None, 1, f2), lambda t, te: (te[t], 0, 0)),
                  pl.BlockSpec((None, f2 // 2, d), lambda t, te: (te[t], 0, 0)),
                  pl.BlockSpec((None, 1, d), lambda t, te: (te[t], 0, 0))],
        out_specs=pl.BlockSpec((tm, half), lambda t, te: (t, 0)),
        scratch_shapes=[pltpu.VMEM((2, tm, half), h_packed.dtype), pltpu.SemaphoreType.DMA((2,))],
    )
    return pl.pallas_call(
        _moe_expert_kernel,
        grid_spec=grid_spec,
        out_shape=jax.ShapeDtypeStruct((n_tiles * tm, half), h_packed.dtype),
        compiler_params=_cparams(("arbitrary",)),
        name="moe_experts",
    )(tile_expert, src3, src3, h_packed, wgu, bgu, wd, bd)


def _moe_combine_kernel(dcur_ref, dnext_ref, wts_ref, x_ref, mod_ref, ys_hbm, o_ref, buf, sem, *, gi, n_ctx, tr):
    t = pl.program_id(0)
    slot = t % 2

    def gather(d_ref, s):
        def body(i, carry):
            for k in range(TOP_K):
                _row_copy(ys_hbm, d_ref[k, i], buf.at[s, k], i, sem.at[s]).start()
            return carry
        lax.fori_loop(0, tr, body, 0, unroll=4)

    @pl.when(t == 0)
    def _():
        gather(dcur_ref, 0)

    @pl.when(t + 1 < pl.num_programs(0))
    def _():
        gather(dnext_ref, 1 - slot)

    for k in range(TOP_K):
        pltpu.make_async_copy(ys_hbm.at[pl.ds(0, tr), :], buf.at[slot, k], sem.at[slot]).wait()

    half = buf.shape[-1]
    acc_lo = jnp.zeros((tr, half), F32)
    acc_hi = jnp.zeros((tr, half), F32)
    for k in range(TOP_K):
        lo, hi = _unpack_bf16_pair(buf[slot, k])
        w = wts_ref[:, k:k + 1]
        acc_lo = acc_lo + w * lo
        acc_hi = acc_hi + w * hi
    is_ctx = _row_is_ctx(t * tr, tr, n_ctx)
    gate = jnp.where(is_ctx, mod_ref[1, gi:gi + 1, :], mod_ref[0, gi:gi + 1, :])
    o_ref[:, :half] = x_ref[:, :half] + gate[:, :half] * acc_lo
    o_ref[:, half:] = x_ref[:, half:] + gate[:, half:] * acc_hi


def moe_combine(ys, dest_t, wts, x, mods, gi, n_ctx):
    tr = MOE_COMBINE_TR
    n, d = x.shape
    nt = n // tr
    return pl.pallas_call(
        functools.partial(_moe_combine_kernel, gi=gi, n_ctx=n_ctx, tr=tr),
        grid=(nt,),
        in_specs=[pl.BlockSpec((TOP_K, tr), lambda t: (0, t), memory_space=pltpu.SMEM),
                  pl.BlockSpec((TOP_K, tr), lambda t: (0, jnp.minimum(t + 1, nt - 1)), memory_space=pltpu.SMEM),
                  pl.BlockSpec((tr, TOP_K), lambda t: (t, 0)),
                  pl.BlockSpec((tr, d), lambda t: (t, 0)),
                  pl.BlockSpec((2, N_MOD, d), lambda t: (0, 0, 0)),
                  pl.BlockSpec(memory_space=pl.ANY)],
        out_specs=pl.BlockSpec((tr, d), lambda t: (t, 0)),
        out_shape=jax.ShapeDtypeStruct((n, d), F32),
        scratch_shapes=[pltpu.VMEM((2, TOP_K, tr, ys.shape[1]), ys.dtype), pltpu.SemaphoreType.DMA((2,))],
        compiler_params=_cparams(("arbitrary",)),
        name="moe_combine",
    )(dest_t, dest_t, wts, x, mods, ys)


def moe_routing_tables(idx, rank, counts, n_tiles, expert_base):
    tm = MOE_TM
    n = idx.shape[0]
    cnt = counts[0].astype(jnp.int32)
    padded = (cnt + tm - 1) // tm * tm
    ends = jnp.cumsum(padded)
    offs = ends - padded
    onehot = idx[:, :, None] == jnp.arange(N_EXPERTS, dtype=jnp.int32)
    dest = jnp.sum(jnp.where(onehot, offs, 0), axis=-1) + rank
    token = jnp.broadcast_to(jnp.arange(n, dtype=jnp.int32)[:, None], dest.shape)
    src_rows = jnp.zeros((n_tiles * tm,), jnp.int32).at[dest.reshape(-1)].set(token.reshape(-1))
    tile_start = jnp.arange(n_tiles, dtype=jnp.int32) * tm
    tile_expert = jnp.minimum(jnp.sum(tile_start[:, None] >= ends[None, :], axis=1), N_EXPERTS - 1)
    return dest.T, src_rows, tile_expert.astype(jnp.int32) + expert_base


def moe_ffn(x, norm_g, mods, router_w, router_b, wgu, bgu, wd, bd, expert_base, n_ctx):
    n = x.shape[0]
    n_tiles = -(-(TOP_K * n + N_EXPERTS * (MOE_TM - 1)) // MOE_TM)
    h_packed, idx, wts, rank, counts = norm_router(x, norm_g, mods, 3, router_w, router_b, n_ctx)
    dest_t, src_rows, tile_expert = moe_routing_tables(idx, rank, counts, n_tiles, expert_base)
    ys = moe_experts(h_packed, src_rows, tile_expert, wgu, bgu, wd, bd)
    return moe_combine(ys, dest_t, wts, x, mods, 5, n_ctx)


def _mm_tile(m):
    for tm in (1280, 1024, 640, 512, 256):
        if m % tm == 0:
            return tm
    raise ValueError(f"unsupported row count {m}")


def kernel(x, c, ctx, c_ctx, ada_w, ada_b, norm_mix, norm_ffn, w_in, w_out, hgrn_lb_logits, hgrn_norm,
           diff_lambda, diff_subln, na_rpb, router_w, router_b, expert_w_gu, expert_b_gu, expert_w_down,
           expert_b_down, final_norm_g):
    assert x.shape[0] == 1 and c.shape[0] == 1
    depth = ada_w.shape[0]
    n_ctx = ctx.shape[1]
    s_len = x.shape[1]
    n = n_ctx + s_len
    rows = s_len // GRID_W
    tm = _mm_tile(n)

    hall = jnp.concatenate([ctx[0], x[0]], axis=0)
    c8 = jnp.zeros((8, D_MODEL), F32).at[0].set(c[0]).at[1].set(c_ctx)
    mods_all = ada_mod(c8, ada_w, ada_b)[:, :2].reshape(depth, 2, N_MOD, D_MODEL)

    p_lb = jax.nn.softmax(hgrn_lb_logits.astype(F32), axis=0)
    lower_bounds = jnp.clip(jnp.cumsum(p_lb, axis=0) - p_lb[0], 0.0, 1.0)
    cos, sin = rope_tables(n, n_ctx)

    wgu_all = deinterleave_gate_up(expert_w_gu.reshape(depth * N_EXPERTS, D_MODEL, 2 * D_EXPERT))
    bgu_all = jnp.concatenate([expert_b_gu[..., 0::2], expert_b_gu[..., 1::2]], axis=-1)
    bgu_all = bgu_all.reshape(depth * N_EXPERTS, 1, 2 * D_EXPERT)
    wd_all = expert_w_down.reshape(depth * N_EXPERTS, D_EXPERT, D_MODEL).astype(BF16)
    bd_all = expert_b_down.reshape(depth * N_EXPERTS, 1, D_MODEL)
    tk = n // 13 if n % (13 * 256) == 0 else 256

    for layer in range(depth):
        lambda_init = 0.8 - 0.6 * math.exp(-0.3 * layer)
        mods = mods_all[layer]
        w_in_b = w_in[layer].astype(BF16)
        w_out_b = w_out[layer].astype(BF16)

        h = norm_mod(hall, norm_mix[layer], mods, 0, n_ctx)
        p = matmul(h, w_in_b, tm, 1024, BF16)

        ya = hgrn_mixer(p, lower_bounds[layer], hgrn_norm[layer], n_ctx)
        qr, kt = rope_qk(p, cos, sin)
        yb = diff_attention(qr, kt, p, diff_lambda[layer], diff_subln[layer], lambda_init,
                            n_ctx, s_len, n, 256, tk)
        yb = diff_attention(qr, kt, p, diff_lambda[layer], diff_subln[layer], lambda_init,
                            0, n_ctx, n_ctx, n_ctx, n_ctx, out_rows=yb)
        yc = na_attention(p, na_bias_tables(na_rpb[layer], rows), n_ctx)
        yc = na_ctx_attention(p, n_ctx, yc)
        hall = matmul_residual([ya, yb, yc], w_out_b, hall, mods, 2, n_ctx, tm, 512)
        hall = moe_ffn(hall, norm_ffn[layer], mods, router_w[layer], router_b[layer], wgu_all, bgu_all,
                       wd_all, bd_all, layer * N_EXPERTS, n_ctx)

    return final_norm(hall, final_norm_g, n_ctx)[None]
```

```python
import functools
import math

import numpy as np
import jax
import jax.numpy as jnp
from jax import lax
from jax.experimental import pallas as pl
from jax.experimental.pallas import tpu as pltpu

F32 = jnp.float32
BF16 = jnp.bfloat16

D_MODEL = 4096
HEAD_DIM = 128
GRID_W = 64
A_HEADS = 12
A_WIDTH = A_HEADS * HEAD_DIM
B_HEADS = 5
B_WIDTH = B_HEADS * 2 * HEAD_DIM
C_HEADS = 10
C_WIDTH = C_HEADS * HEAD_DIM
A_COLS = 5 * A_WIDTH
B_COLS = 3 * B_WIDTH
C_COLS = 3 * C_WIDTH
IN_COLS = A_COLS + B_COLS + C_COLS
NA_KH = 8
NA_KW = 16
ROPE_BASE = 10000.0
N_EXPERTS = 32
TOP_K = 4
D_EXPERT = 512
SWIGLU_ALPHA = 1.702
SWIGLU_LIMIT = 7.0
N_MOD = 6
NORM_EPS = 1e-6
F_TINY = 1e-30
MASK_NEG = -1e30
LOG2_E = 1.4426950408889634

V7X_VMEM_LIMIT_BYTES = 56 * 1024 * 1024

SCAN_CHUNK = 64
SCAN_SUB = 16
NA_QROWS = 4
NA_QBLK = NA_QROWS * GRID_W
NA_KBLKS = 3


def _cparams(sem):
    return pltpu.CompilerParams(dimension_semantics=sem, vmem_limit_bytes=V7X_VMEM_LIMIT_BYTES)


def _dot(a, b):
    return jnp.dot(a, b, preferred_element_type=F32)


def _dot_nt(a, b):
    return lax.dot_general(a, b, (((1,), (1,)), ((), ())), preferred_element_type=F32)


def _dot_tn(a, b):
    return lax.dot_general(a, b, (((0,), (0,)), ((), ())), preferred_element_type=F32)


def _sigmoid(x):
    return 1.0 / (1.0 + jnp.exp(-x))


def _row_is_ctx(row0, rows, n_ctx):
    return (row0 + lax.broadcasted_iota(jnp.int32, (rows, 1), 0)) < n_ctx


def _ada_kernel(c_ref, w_ref, b_ref, o_ref):
    c = c_ref[...]
    a = (c * _sigmoid(c)).astype(BF16)
    o_ref[...] = _dot(a, w_ref[...].astype(BF16)) + b_ref[...]


def ada_mod(c8, ada_w, ada_b, tn=512):
    depth, d, n = ada_w.shape
    return pl.pallas_call(
        _ada_kernel,
        grid=(depth, n // tn),
        in_specs=[pl.BlockSpec((8, d), lambda l, j: (0, 0)),
                  pl.BlockSpec((None, d, tn), lambda l, j: (l, 0, j)),
                  pl.BlockSpec((None, 1, tn), lambda l, j: (l, 0, j))],
        out_specs=pl.BlockSpec((None, 8, tn), lambda l, j: (l, 0, j)),
        out_shape=jax.ShapeDtypeStruct((depth, 8, n), F32),
        compiler_params=_cparams(("arbitrary", "arbitrary")),
        name="ada_mod",
    )(c8, ada_w, ada_b.reshape(depth, 1, n))


def _norm_mod(x, g, mod_ref, si, row0, n_ctx):
    ms = jnp.mean(x * x, axis=-1, keepdims=True)
    y = x * lax.rsqrt(ms + NORM_EPS) * g
    is_ctx = _row_is_ctx(row0, x.shape[0], n_ctx)
    shift = jnp.where(is_ctx, mod_ref[1, si:si + 1, :], mod_ref[0, si:si + 1, :])
    scale = jnp.where(is_ctx, mod_ref[1, si + 1:si + 2, :], mod_ref[0, si + 1:si + 2, :])
    return y * (1.0 + scale) + shift


def _norm_kernel(x_ref, g_ref, mod_ref, o_ref, *, si, n_ctx, tr):
    row0 = pl.program_id(0) * tr
    o_ref[...] = _norm_mod(x_ref[...], g_ref[...], mod_ref, si, row0, n_ctx).astype(o_ref.dtype)


def norm_mod(x, g, mods, si, n_ctx, tr=256):
    n, d = x.shape
    return pl.pallas_call(
        functools.partial(_norm_kernel, si=si, n_ctx=n_ctx, tr=tr),
        grid=(n // tr,),
        in_specs=[pl.BlockSpec((tr, d), lambda i: (i, 0)),
                  pl.BlockSpec((1, d), lambda i: (0, 0)),
                  pl.BlockSpec((2, N_MOD, d), lambda i: (0, 0, 0))],
        out_specs=pl.BlockSpec((tr, d), lambda i: (i, 0)),
        out_shape=jax.ShapeDtypeStruct((n, d), BF16),
        compiler_params=_cparams(("parallel",)),
        name="norm_mod",
    )(x, g.reshape(1, d), mods)


def _split_bf16(x):
    hi = x.astype(BF16)
    lo = (x - hi.astype(F32)).astype(BF16)
    return hi, lo


def _pack_bf16_pair(lo, hi):
    lo_b = lax.bitcast_convert_type(lo.astype(BF16).astype(F32), jnp.uint32) >> 16
    hi_b = lax.bitcast_convert_type(hi.astype(BF16).astype(F32), jnp.uint32) & jnp.uint32(0xFFFF0000)
    return hi_b | lo_b


def _unpack_bf16_pair(w):
    lo = lax.bitcast_convert_type(w << 16, F32)
    hi = lax.bitcast_convert_type(w & jnp.uint32(0xFFFF0000), F32)
    return lo, hi


ROW_SLABS = D_MODEL // 2 // 128


def _store_token_rows(ref, at, packed):
    m = packed.shape[0]
    for j in range(ROW_SLABS):
        ref[at + (pl.ds(j, m, stride=ROW_SLABS), slice(None))] = packed[:, j * 128:(j + 1) * 128]


def _load_token_rows(ref, at, m):
    return jnp.concatenate([ref[at + (pl.ds(j, m, stride=ROW_SLABS), slice(None))] for j in range(ROW_SLABS)],
                           axis=1)


def _norm_router_kernel(x_ref, g_ref, mod_ref, rw_ref, rb_ref, h_ref, idx_ref, wts_ref, rank_ref, cnt_ref,
                        carry_ref, *, si, n_ctx, tr):
    @pl.when(pl.program_id(0) == 0)
    def _():
        carry_ref[...] = jnp.zeros_like(carry_ref)

    row0 = pl.program_id(0) * tr
    h = _norm_mod(x_ref[...], g_ref[...], mod_ref, si, row0, n_ctx)
    half = h.shape[1] // 2
    _store_token_rows(h_ref, (), _pack_bf16_pair(h[:, :half], h[:, half:]))
    hh, hl = _split_bf16(h)
    wh, wl = _split_bf16(rw_ref[...])
    logits = _dot(hh, wh) + _dot(hh, wl) + _dot(hl, wh) + rb_ref[...]
    iota = lax.broadcasted_iota(jnp.int32, logits.shape, 1)
    work = logits
    vals, sels = [], []
    for k in range(TOP_K):
        m = jnp.max(work, axis=-1, keepdims=True)
        idx = jnp.min(jnp.where(work == m, iota, N_EXPERTS), axis=-1, keepdims=True)
        sel = iota == idx
        vals.append(m)
        sels.append(sel)
        idx_ref[:, k:k + 1] = idx
        work = jnp.where(sel, -jnp.inf, work)
    es = [jnp.exp(v - vals[0]) for v in vals]
    tot = es[0] + es[1] + es[2] + es[3]
    for k in range(TOP_K):
        wts_ref[:, k:k + 1] = es[k] / tot
    used = jnp.zeros_like(logits)
    for k in range(TOP_K):
        used = used + jnp.where(sels[k], 1.0, 0.0)
    r = lax.broadcasted_iota(jnp.int32, (tr, tr), 0)
    c = lax.broadcasted_iota(jnp.int32, (tr, tr), 1)
    before = jnp.where(c < r, 1.0, 0.0).astype(BF16)
    prefix = _dot(before, used.astype(BF16)) + carry_ref[...]
    for k in range(TOP_K):
        rank = jnp.sum(jnp.where(sels[k], prefix, 0.0), axis=-1, keepdims=True)
        rank_ref[:, k:k + 1] = rank.astype(jnp.int32)
    carry_ref[...] += jnp.sum(used, axis=0, keepdims=True)
    cnt_ref[...] = carry_ref[...]


def norm_router(x, g, mods, si, router_w, router_b, n_ctx, tr=256):
    n, d = x.shape
    col = lambda i: (i, 0)
    fixed = lambda i: (0, 0)
    return pl.pallas_call(
        functools.partial(_norm_router_kernel, si=si, n_ctx=n_ctx, tr=tr),
        grid=(n // tr,),
        in_specs=[pl.BlockSpec((tr, d), col),
                  pl.BlockSpec((1, d), fixed),
                  pl.BlockSpec((2, N_MOD, d), lambda i: (0, 0, 0)),
                  pl.BlockSpec((d, N_EXPERTS), fixed),
                  pl.BlockSpec((1, N_EXPERTS), fixed)],
        out_specs=[pl.BlockSpec((tr * ROW_SLABS, 128), col),
                   pl.BlockSpec((tr, TOP_K), col),
                   pl.BlockSpec((tr, TOP_K), col),
                   pl.BlockSpec((tr, TOP_K), col),
                   pl.BlockSpec((1, N_EXPERTS), fixed)],
        out_shape=[jax.ShapeDtypeStruct((n * ROW_SLABS, 128), jnp.uint32),
                   jax.ShapeDtypeStruct((n, TOP_K), jnp.int32),
                   jax.ShapeDtypeStruct((n, TOP_K), F32),
                   jax.ShapeDtypeStruct((n, TOP_K), jnp.int32),
                   jax.ShapeDtypeStruct((1, N_EXPERTS), F32)],
        scratch_shapes=[pltpu.VMEM((1, N_EXPERTS), F32)],
        compiler_params=_cparams(("arbitrary",)),
        name="norm_router",
    )(x, g.reshape(1, d), mods, router_w, router_b.reshape(1, N_EXPERTS))


def _final_norm_kernel(x_ref, g_ref, o_ref):
    x = x_ref[...]
    ms = jnp.mean(x * x, axis=-1, keepdims=True)
    o_ref[...] = x * lax.rsqrt(ms + NORM_EPS) * g_ref[...]


def final_norm(x, g, n_ctx, tr=256):
    n, d = x.shape
    skip = n_ctx // tr
    return pl.pallas_call(
        _final_norm_kernel,
        grid=((n - n_ctx) // tr,),
        in_specs=[pl.BlockSpec((tr, d), lambda i: (i + skip, 0)),
                  pl.BlockSpec((1, d), lambda i: (0, 0))],
        out_specs=pl.BlockSpec((tr, d), lambda i: (i, 0)),
        out_shape=jax.ShapeDtypeStruct((n - n_ctx, d), F32),
        compiler_params=_cparams(("parallel",)),
        name="final_norm",
    )(x, g.reshape(1, d))


def _mm_kernel(a_ref, b_ref, o_ref):
    o_ref[...] = _dot(a_ref[...], b_ref[...]).astype(o_ref.dtype)


def matmul(a, b, tm, tn, out_dtype):
    m, k = a.shape
    n = b.shape[1]
    return pl.pallas_call(
        _mm_kernel,
        grid=(m // tm, n // tn),
        in_specs=[pl.BlockSpec((tm, k), lambda i, j: (i, 0)),
                  pl.BlockSpec((k, tn), lambda i, j: (0, j))],
        out_specs=pl.BlockSpec((tm, tn), lambda i, j: (i, j)),
        out_shape=jax.ShapeDtypeStruct((m, n), out_dtype),
        compiler_params=_cparams(("parallel", "arbitrary")),
        name="matmul",
    )(a, b)


def _mm_res_kernel(*refs, gi, n_ctx, tm, widths):
    a_refs = refs[:len(widths)]
    b_ref, x_ref, mod_ref, o_ref = refs[len(widths):]
    row0 = pl.program_id(0) * tm
    is_ctx = _row_is_ctx(row0, tm, n_ctx)
    gate = jnp.where(is_ctx, mod_ref[1, gi:gi + 1, :], mod_ref[0, gi:gi + 1, :])
    acc = None
    k0 = 0
    for a_ref, w in zip(a_refs, widths):
        part = _dot(a_ref[...], b_ref[k0:k0 + w, :])
        acc = part if acc is None else acc + part
        k0 += w
    o_ref[...] = x_ref[...] + gate * acc


def matmul_residual(a_parts, b, x, mods, gi, n_ctx, tm, tn):
    m = x.shape[0]
    k, n = b.shape
    widths = tuple(a.shape[1] for a in a_parts)
    assert sum(widths) == k
    return pl.pallas_call(
        functools.partial(_mm_res_kernel, gi=gi, n_ctx=n_ctx, tm=tm, widths=widths),
        grid=(m // tm, n // tn),
        in_specs=[pl.BlockSpec((tm, w), lambda i, j: (i, 0)) for w in widths]
        + [pl.BlockSpec((k, tn), lambda i, j: (0, j)),
           pl.BlockSpec((tm, tn), lambda i, j: (i, j)),
           pl.BlockSpec((2, N_MOD, tn), lambda i, j: (0, 0, j))],
        out_specs=pl.BlockSpec((tm, tn), lambda i, j: (i, j)),
        out_shape=jax.ShapeDtypeStruct((m, n), F32),
        compiler_params=_cparams(("parallel", "arbitrary")),
        name="matmul_residual",
    )(*a_parts, b, x, mods)


def _hgrn_masks(c, sub, reverse):
    r = lax.broadcasted_iota(jnp.int32, (c, c), 0)
    s = lax.broadcasted_iota(jnp.int32, (c, c), 1)
    tri = (s >= r) if reverse else (s <= r)
    same_sub = (r // sub) == (s // sub)
    return tri, same_sub


def _hgrn_head(q_raw, f_pre, v, lb, st_in, ck_ref, *, reverse, sel, tri_b, same_sub):
    c, d = q_raw.shape
    sub = SCAN_SUB
    nsub = c // sub
    q = q_raw * _sigmoid(q_raw) * (HEAD_DIM ** -0.5)
    f = lb + (1.0 - lb) * _sigmoid(f_pre)
    log_f = jnp.log2(jnp.maximum(f, F_TINY))
    k = (1.0 - lb) * _sigmoid(-f_pre)

    hi = log_f.astype(BF16)
    r1 = log_f - hi.astype(F32)
    mid = r1.astype(BF16)
    lo = (r1 - mid.astype(F32)).astype(BF16)
    yield
    cum = _dot(tri_b, hi) + _dot(tri_b, mid) + _dot(tri_b, lo)
    yield
    cum_k = cum - jnp.log2(k)
    ck_ref[...] = cum_k

    order = list(range(nsub))
    if reverse:
        order = order[::-1]
    zero_row = jnp.zeros((1, d), F32)
    bnd_in, bnd_out = {}, {}
    prev = zero_row
    for i in order:
        last = i * sub if reverse else i * sub + sub - 1
        bnd_in[i] = prev
        bnd_out[i] = cum[last:last + 1, :]
        prev = bnd_out[i]
    cum_end = prev

    t_in_sub = lax.broadcasted_iota(jnp.int32, (sub, 1), 0)
    q_hat, u_rows, off_rows = [], [], []
    for i in range(nsub):
        sl = slice(i * sub, (i + 1) * sub)
        q_i, cum_i = q[sl], cum[sl]
        q_hat_i = q_i * jnp.exp2(cum_i - bnd_in[i])
        k_parts = []
        for j in range(nsub):
            earlier = (j > i) if reverse else (j < i)
            sj = slice(j * sub, (j + 1) * sub)
            if earlier:
                k_parts.append(jnp.exp2(bnd_in[i] - cum_k[sj]))
            else:
                k_parts.append(jnp.zeros((sub, d), F32))
        k_til = jnp.concatenate(k_parts, axis=0).astype(BF16)
        off_rows.append(_dot_nt(q_hat_i.astype(BF16), k_til))
        yield
        u_parts = []
        for s in range(sub):
            groups = []
            for g8 in range(sub // 8):
                rows = slice(g8 * 8, g8 * 8 + 8)
                if g8 == s // 8:
                    rows_t = t_in_sub[rows]
                    ok = (rows_t <= s) if reverse else (rows_t >= s)
                    e = jnp.where(ok, cum_i[rows] - ck_ref[i * sub + s:i * sub + s + 1, :], MASK_NEG)
                elif (g8 < s // 8) == reverse:
                    e = cum_i[rows] - ck_ref[i * sub + s:i * sub + s + 1, :]
                else:
                    groups.append(jnp.zeros((8, d), BF16))
                    continue
                groups.append((q_i[rows] * jnp.exp2(e)).astype(BF16))
            u_parts.append(jnp.concatenate(groups, axis=0))
        u_rows.append(jnp.concatenate(u_parts, axis=1))
        yield
    u = jnp.concatenate(u_rows, axis=0)
    diag = _dot(u, sel)
    v_b = v.astype(BF16)
    k_end = jnp.exp2(cum_end - cum_k).astype(BF16)
    st_out = st_in * jnp.exp2(cum_end) + _dot_tn(v_b, k_end)
    o_state = _dot_nt((q * jnp.exp2(cum)).astype(BF16), st_in.astype(BF16))
    yield
    attn = jnp.concatenate(off_rows, axis=0) + jnp.where(same_sub, diag, 0.0)
    o = _dot(attn.astype(BF16), v_b) + o_state
    return o, st_out


def _run_interleaved(gens):
    results = [None] * len(gens)
    live = list(range(len(gens)))
    while live:
        for idx in list(live):
            try:
                next(gens[idx])
            except StopIteration as done:
                results[idx] = done.value
                live.remove(idx)
    return results


def _hgrn_sel(c, sub):
    rows = np.arange(sub * HEAD_DIM) // HEAD_DIM
    cols = np.arange(c) % sub
    return jnp.asarray((rows[:, None] == cols[None, :]).astype(np.float32), dtype=BF16)


def _hgrn_fw_kernel(q_ref, f_ref, v_ref, lb_ref, sel_ref, o_ref, st_ref, ck_ref, *, hb):
    @pl.when(pl.program_id(1) == 0)
    def _():
        st_ref[...] = jnp.zeros_like(st_ref)

    sel = sel_ref[...]
    tri, same_sub = _hgrn_masks(q_ref.shape[0], SCAN_SUB, False)
    tri_b = jnp.where(tri, 1.0, 0.0).astype(BF16)
    cols = [slice(h * HEAD_DIM, (h + 1) * HEAD_DIM) for h in range(hb)]
    heads = _run_interleaved([
        _hgrn_head(q_ref[:, cs].astype(F32), f_ref[:, cs].astype(F32), v_ref[:, cs].astype(F32),
                   lb_ref[0:1, cs], st_ref[h], ck_ref.at[h], reverse=False, sel=sel, tri_b=tri_b,
                   same_sub=same_sub)
        for h, cs in enumerate(cols)])
    for h, (cs, (o, st)) in enumerate(zip(cols, heads)):
        st_ref[h] = st
        o_ref[:, cs] = o


def _hgrn_bw_kernel(q_ref, f_ref, v_ref, g_ref, ofw_ref, lb_ref, ng_ref, sel_ref, y_ref, st_ref, ck_ref, *, hb):
    @pl.when(pl.program_id(1) == 0)
    def _():
        st_ref[...] = jnp.zeros_like(st_ref)

    sel = sel_ref[...]
    tri, same_sub = _hgrn_masks(q_ref.shape[0], SCAN_SUB, True)
    tri_b = jnp.where(tri, 1.0, 0.0).astype(BF16)
    cols = [slice(h * HEAD_DIM, (h + 1) * HEAD_DIM) for h in range(hb)]
    heads = _run_interleaved([
        _hgrn_head(q_ref[:, cs].astype(F32), f_ref[:, cs].astype(F32), v_ref[:, cs].astype(F32),
                   lb_ref[1:2, cs], st_ref[h], ck_ref.at[h], reverse=True, sel=sel, tri_b=tri_b,
                   same_sub=same_sub)
        for h, cs in enumerate(cols)])
    for h, (cs, (o, st)) in enumerate(zip(cols, heads)):
        st_ref[h] = st
        o = o + ofw_ref[:, cs]
        ms = jnp.mean(o * o, axis=-1, keepdims=True)
        y = o * lax.rsqrt(ms + NORM_EPS) * ng_ref[...]
        g = g_ref[:, cs].astype(F32)
        y_ref[:, cs] = (y * (g * _sigmoid(g))).astype(y_ref.dtype)


def hgrn_mixer(p, lb, norm_g, n_ctx, hb=A_HEADS):
    n = p.shape[0]
    c = SCAN_CHUNK
    nc = n // c
    nc_ctx = n_ctx // c
    ngroups = A_HEADS // hb
    w = hb * HEAD_DIM
    per = A_WIDTH // w
    sel = _hgrn_sel(c, SCAN_SUB)

    def fw_chunk(s):
        return s

    def bw_chunk(s):
        return jnp.where(s < nc_ctx, nc_ctx - 1 - s, nc - 1 - (s - nc_ctx))

    def col(section, chunk_of):
        return pl.BlockSpec((c, w), lambda g, s: (chunk_of(s), section * per + g))

    lb_spec = pl.BlockSpec((2, w), lambda g, s: (0, g))
    sel_spec = pl.BlockSpec(sel.shape, lambda g, s: (0, 0))
    scratch = [pltpu.VMEM((hb, HEAD_DIM, HEAD_DIM), F32), pltpu.VMEM((hb, c, HEAD_DIM), F32)]
    o_fw = pl.pallas_call(
        functools.partial(_hgrn_fw_kernel, hb=hb),
        grid=(ngroups, nc),
        in_specs=[col(0, fw_chunk), col(1, fw_chunk), col(3, fw_chunk), lb_spec, sel_spec],
        out_specs=pl.BlockSpec((c, w), lambda g, s: (s, g)),
        out_shape=jax.ShapeDtypeStruct((n, A_WIDTH), F32),
        scratch_shapes=scratch,
        compiler_params=_cparams(("parallel", "arbitrary")),
        name="hgrn_fw",
    )(p, p, p, lb, sel)
    return pl.pallas_call(
        functools.partial(_hgrn_bw_kernel, hb=hb),
        grid=(ngroups, nc),
        in_specs=[col(0, bw_chunk), col(2, bw_chunk), col(3, bw_chunk), col(4, bw_chunk),
                  pl.BlockSpec((c, w), lambda g, s: (bw_chunk(s), g)),
                  lb_spec, pl.BlockSpec((1, HEAD_DIM), lambda g, s: (0, 0)), sel_spec],
        out_specs=pl.BlockSpec((c, w), lambda g, s: (bw_chunk(s), g)),
        out_shape=jax.ShapeDtypeStruct((n, A_WIDTH), BF16),
        scratch_shapes=scratch,
        compiler_params=_cparams(("parallel", "arbitrary")),
        name="hgrn_bw",
    )(p, p, p, p, o_fw, lb, norm_g.reshape(1, HEAD_DIM), sel)


def rope_tables(n, n_ctx):
    t = jnp.arange(n - n_ctx, dtype=jnp.int32)
    half = HEAD_DIM // 4
    inv_freq = ROPE_BASE ** (-jnp.arange(half, dtype=F32) / half)
    ang_r = (t // GRID_W).astype(F32)[:, None] * inv_freq
    ang_c = (t % GRID_W).astype(F32)[:, None] * inv_freq
    cos = jnp.concatenate([jnp.cos(ang_r)] * 2 + [jnp.cos(ang_c)] * 2, axis=-1)
    sin = jnp.concatenate([-jnp.sin(ang_r), jnp.sin(ang_r), -jnp.sin(ang_c), jnp.sin(ang_c)], axis=-1)
    cos = jnp.concatenate([jnp.ones((n_ctx, HEAD_DIM), F32), cos], axis=0)
    sin = jnp.concatenate([jnp.zeros((n_ctx, HEAD_DIM), F32), sin], axis=0)
    return cos, sin


def _rope_kernel(q_ref, k_ref, cos_ref, sin_ref, qo_ref, kt_ref):
    cos = cos_ref[...]
    sin = sin_ref[...]
    lane = lax.broadcasted_iota(jnp.int32, cos.shape, 1)
    low = (lane & (HEAD_DIM // 4)) == 0

    def rotate(x):
        partner = jnp.where(low, pltpu.roll(x, HEAD_DIM - HEAD_DIM // 4, 1), pltpu.roll(x, HEAD_DIM // 4, 1))
        return x * cos + partner * sin

    for h in range(2 * B_HEADS):
        cs = slice(h * HEAD_DIM, (h + 1) * HEAD_DIM)
        qo_ref[:, cs] = (rotate(q_ref[:, cs].astype(F32)) * (LOG2_E * HEAD_DIM ** -0.5)).astype(qo_ref.dtype)
        kt_ref[cs, :] = rotate(k_ref[:, cs].astype(F32)).T.astype(kt_ref.dtype)


def rope_qk(p, cos, sin, tr=256):
    n = p.shape[0]
    qblk = A_COLS // B_WIDTH
    return pl.pallas_call(
        _rope_kernel,
        grid=(n // tr,),
        in_specs=[pl.BlockSpec((tr, B_WIDTH), lambda i: (i, qblk)),
                  pl.BlockSpec((tr, B_WIDTH), lambda i: (i, qblk + 1)),
                  pl.BlockSpec((tr, HEAD_DIM), lambda i: (i, 0)),
                  pl.BlockSpec((tr, HEAD_DIM), lambda i: (i, 0))],
        out_specs=[pl.BlockSpec((tr, B_WIDTH), lambda i: (i, 0)),
                   pl.BlockSpec((B_WIDTH, tr), lambda i: (0, i))],
        out_shape=[jax.ShapeDtypeStruct((n, B_WIDTH), BF16), jax.ShapeDtypeStruct((B_WIDTH, n), BF16)],
        compiler_params=_cparams(("parallel",)),
        name="rope_qk",
    )(p, p, cos, sin)


def _diff_attn_kernel(lam_ref, g_ref, q_ref, kt_ref, v_ref, *rest, lambda_init, nk, tk):
    o_ref, m_scr, l_scr, acc_scr, sa_scr, sb_scr = rest[-6:]
    tq = q_ref.shape[0]
    q = q_ref[...]
    q1 = q[:, :HEAD_DIM]
    q2 = q[:, HEAD_DIM:]
    m_scr[...] = jnp.full_like(m_scr, -jnp.inf)
    l_scr[...] = jnp.zeros_like(l_scr)
    acc_scr[...] = jnp.zeros_like(acc_scr)

    def scores(j, s_scr):
        off = pl.multiple_of(j * tk, 128)
        s_scr[:tq, :] = _dot(q1, kt_ref[:HEAD_DIM, pl.ds(off, tk)])
        s_scr[tq:, :] = _dot(q2, kt_ref[HEAD_DIM:, pl.ds(off, tk)])

    def accumulate(j, s_scr):
        off = pl.multiple_of(j * tk, tk)
        s = s_scr[...]
        m_prev = m_scr[...]
        m_new = jnp.maximum(m_prev, jnp.max(s, axis=-1, keepdims=True))
        alpha = jnp.exp2(m_prev - m_new)
        p = jnp.exp2(s - m_new)
        l_scr[...] = alpha * l_scr[...] + jnp.sum(p, axis=-1, keepdims=True)
        acc_scr[...] = alpha * acc_scr[...] + _dot(p.astype(BF16), v_ref[pl.ds(off, tk), :])
        m_scr[...] = m_new

    scores(0, sa_scr)

    def pair(jj, carry):
        j = 2 * jj
        scores(j + 1, sb_scr)
        accumulate(j, sa_scr)
        scores(jnp.minimum(j + 2, nk - 1), sa_scr)
        accumulate(j + 1, sb_scr)
        return carry

    lax.fori_loop(0, nk // 2, pair, 0)
    if nk % 2:
        accumulate(nk - 1, sa_scr)

    lp = lam_ref[...]
    lam = (jnp.exp(jnp.sum(lp[0:1] * lp[1:2], axis=-1, keepdims=True))
           - jnp.exp(jnp.sum(lp[2:3] * lp[3:4], axis=-1, keepdims=True)) + lambda_init)
    o = acc_scr[:tq] / l_scr[:tq] - lam * (acc_scr[tq:] / l_scr[tq:])
    ms = jnp.mean(o * o, axis=-1, keepdims=True)
    y = o * lax.rsqrt(ms + NORM_EPS) * g_ref[...] * (1.0 - lambda_init)
    o_ref[...] = y.astype(o_ref.dtype)


def diff_attention(q, kt, p, lam_p, subln_g, lambda_init, q_row0, n_q, n_kv, tq, tk, out_rows=None):
    hw = 2 * HEAD_DIM
    n = q.shape[0]
    nq, nk = n_q // tq, n_kv // tk
    vblk = (A_COLS + 2 * B_WIDTH) // hw
    qb0 = q_row0 // tq
    q_spec = pl.BlockSpec((tq, hw), lambda h, i: (i + qb0, h))
    in_specs = [pl.BlockSpec((4, HEAD_DIM), lambda h, i: (0, 0)),
                pl.BlockSpec((1, hw), lambda h, i: (0, 0)),
                q_spec,
                pl.BlockSpec((hw, n_kv), lambda h, i: (h, 0)),
                pl.BlockSpec((n_kv, hw), lambda h, i: (0, vblk + h))]
    args = [lam_p, subln_g.reshape(1, hw), q, kt, p]
    aliases = {}
    if out_rows is not None:
        in_specs.append(pl.BlockSpec(memory_space=pl.ANY))
        args.append(out_rows)
        aliases = {len(args) - 1: 0}
    return pl.pallas_call(
        functools.partial(_diff_attn_kernel, lambda_init=lambda_init, nk=nk, tk=tk),
        grid=(B_HEADS, nq),
        in_specs=in_specs,
        out_specs=q_spec,
        out_shape=jax.ShapeDtypeStruct((n, B_WIDTH), BF16),
        scratch_shapes=[pltpu.VMEM((2 * tq, 1), F32), pltpu.VMEM((2 * tq, 1), F32),
                        pltpu.VMEM((2 * tq, hw), F32),
                        pltpu.VMEM((2 * tq, tk), F32), pltpu.VMEM((2 * tq, tk), F32)],
        input_output_aliases=aliases,
        compiler_params=_cparams(("parallel", "arbitrary")),
        name="diff_attention",
    )(*args)


def na_bias_tables(rpb, rows):
    krows = NA_KBLKS * NA_QROWS
    n_dr, n_dc = 2 * NA_KH - 1, 2 * NA_KW - 1
    qc = np.arange(GRID_W)[:, None]
    kc = np.arange(GRID_W)[None, :]
    cs = np.clip(qc - NA_KW // 2, 0, GRID_W - NA_KW)
    col_ok = (kc >= cs) & (kc < cs + NA_KW)
    dc = np.clip(kc - qc + NA_KW - 1, 0, n_dc - 1)
    onehot = (np.arange(n_dc)[:, None, None] == dc[None]).astype(np.float32).reshape(n_dc, GRID_W * GRID_W)
    toep = jnp.dot(rpb.astype(F32).reshape(C_HEADS * n_dr, n_dc), jnp.asarray(onehot),
                   precision=lax.Precision.HIGHEST).reshape(C_HEADS, n_dr, GRID_W, GRID_W)
    toep = jnp.where(jnp.asarray(col_ok)[None, None], toep, MASK_NEG)
    masked = jnp.full((C_HEADS, GRID_W, GRID_W), MASK_NEG, F32)
    cases = [(0, 0), (NA_QROWS, 0), (rows - NA_QROWS, rows - krows)]
    out = []
    for r0, start in cases:
        q_blocks = []
        for qr in range(NA_QROWS):
            r = r0 + qr
            rs = min(max(r - NA_KH // 2, 0), rows - NA_KH)
            k_blocks = []
            for kr in range(krows):
                kra = start + kr
                k_blocks.append(toep[:, kra - r + NA_KH - 1] if rs <= kra < rs + NA_KH else masked)
            q_blocks.append(jnp.concatenate(k_blocks, axis=-1))
        out.append(jnp.concatenate(q_blocks, axis=1))
    return jnp.stack(out)


def _na_kernel(q_ref, k0_ref, k1_ref, k2_ref, kc_ref, v0_ref, v1_ref, v2_ref, vc_ref, bias_ref, o_ref):
    scale = HEAD_DIM ** -0.5
    k_refs = (k0_ref, k1_ref, k2_ref)
    v_refs = (v0_ref, v1_ref, v2_ref, vc_ref)
    for h in range(C_HEADS):
        cs = slice(h * HEAD_DIM, (h + 1) * HEAD_DIM)
        q = q_ref[:, cs]
        s_parts = [_dot_nt(q, k_refs[b][:, cs]) * scale + bias_ref[h, :, b * NA_QBLK:(b + 1) * NA_QBLK]
                   for b in range(NA_KBLKS)]
        s_parts.append(_dot_nt(q, kc_ref[:, cs]) * scale)
        m = s_parts[0].max(axis=-1, keepdims=True)
        for s in s_parts[1:]:
            m = jnp.maximum(m, s.max(axis=-1, keepdims=True))
        l = jnp.zeros_like(m)
        acc = jnp.zeros((q.shape[0], HEAD_DIM), F32)
        for s, v_ref in zip(s_parts, v_refs):
            pexp = jnp.exp(s - m)
            l = l + pexp.sum(axis=-1, keepdims=True)
            acc = acc + _dot(pexp.astype(BF16), v_ref[:, cs])
        o_ref[:, cs] = (acc / l).astype(o_ref.dtype)


def na_attention(p, bias, n_ctx):
    n = p.shape[0]
    nqb = (n - n_ctx) // NA_QBLK
    cb = n_ctx // NA_QBLK
    qblk = (A_COLS + B_COLS) // C_WIDTH

    def kv_spec(section, b):
        return pl.BlockSpec((NA_QBLK, C_WIDTH),
                            lambda i: (cb + jnp.clip(i - 1, 0, nqb - NA_KBLKS) + b, qblk + section))

    def ctx_spec(section):
        return pl.BlockSpec((n_ctx, C_WIDTH), lambda i: (0, qblk + section))

    def case(i):
        return jnp.where(i == 0, 0, jnp.where(i == nqb - 1, 2, 1))

    return pl.pallas_call(
        _na_kernel,
        grid=(nqb,),
        in_specs=[pl.BlockSpec((NA_QBLK, C_WIDTH), lambda i: (cb + i, qblk)),
                  kv_spec(1, 0), kv_spec(1, 1), kv_spec(1, 2), ctx_spec(1),
                  kv_spec(2, 0), kv_spec(2, 1), kv_spec(2, 2), ctx_spec(2),
                  pl.BlockSpec((None, C_HEADS, NA_QBLK, NA_KBLKS * NA_QBLK), lambda i: (case(i), 0, 0, 0))],
        out_specs=pl.BlockSpec((NA_QBLK, C_WIDTH), lambda i: (cb + i, 0)),
        out_shape=jax.ShapeDtypeStruct((n, C_WIDTH), BF16),
        compiler_params=_cparams(("parallel",)),
        name="na_attention",
    )(p, p, p, p, p, p, p, p, p, bias)


def _ctx_attn_kernel(q_ref, k_ref, v_ref, rows_hbm, o_ref):
    del rows_hbm
    scale = HEAD_DIM ** -0.5
    for h in range(C_HEADS):
        cs = slice(h * HEAD_DIM, (h + 1) * HEAD_DIM)
        s = _dot_nt(q_ref[:, cs], k_ref[:, cs]) * scale
        pexp = jnp.exp(s - s.max(axis=-1, keepdims=True))
        o = _dot(pexp.astype(BF16), v_ref[:, cs]) / pexp.sum(axis=-1, keepdims=True)
        o_ref[:, cs] = o.astype(o_ref.dtype)


def na_ctx_attention(p, n_ctx, out_rows):
    qblk = (A_COLS + B_COLS) // C_WIDTH
    return pl.pallas_call(
        _ctx_attn_kernel,
        grid=(1,),
        in_specs=[pl.BlockSpec((n_ctx, C_WIDTH), lambda i, s=s: (0, qblk + s)) for s in range(3)]
        + [pl.BlockSpec(memory_space=pl.ANY)],
        out_specs=pl.BlockSpec((n_ctx, C_WIDTH), lambda i: (0, 0)),
        out_shape=jax.ShapeDtypeStruct(out_rows.shape, BF16),
        input_output_aliases={3: 0},
        compiler_params=_cparams(("arbitrary",)),
        name="na_ctx_attention",
    )(p, p, p, out_rows)


MOE_TM = 256
MOE_COMBINE_TR = 128


DEINT_BLOCK = 256


def _deint_kernel(w_ref, p_ref, o_ref):
    f2 = w_ref.shape[1]
    hb = DEINT_BLOCK // 2
    perm = p_ref[...]
    for j in range(f2 // DEINT_BLOCK):
        part = _dot(w_ref[:, j * DEINT_BLOCK:(j + 1) * DEINT_BLOCK].astype(BF16), perm).astype(o_ref.dtype)
        o_ref[:, j * hb:(j + 1) * hb] = part[:, :hb]
        o_ref[:, f2 // 2 + j * hb:f2 // 2 + (j + 1) * hb] = part[:, hb:]


def deinterleave_gate_up(w_gu, tk=1024):
    g, d, f2 = w_gu.shape
    src = np.concatenate([np.arange(0, DEINT_BLOCK, 2), np.arange(1, DEINT_BLOCK, 2)])
    perm = jnp.asarray((np.arange(DEINT_BLOCK)[:, None] == src[None, :]).astype(np.float32), dtype=BF16)
    return pl.pallas_call(
        _deint_kernel,
        grid=(g, d // tk),
        in_specs=[pl.BlockSpec((None, tk, f2), lambda e, i: (e, i, 0)),
                  pl.BlockSpec((DEINT_BLOCK, DEINT_BLOCK), lambda e, i: (0, 0))],
        out_specs=pl.BlockSpec((None, tk, f2), lambda e, i: (e, i, 0)),
        out_shape=jax.ShapeDtypeStruct((g, d, f2), BF16),
        compiler_params=_cparams(("parallel", "parallel")),
        name="deinterleave_gate_up",
    )(w_gu, perm)


def _row_copy(src_hbm, src_row, dst, dst_row, sem):
    src0 = pl.multiple_of(src_row * ROW_SLABS, ROW_SLABS)
    dst0 = pl.multiple_of(dst_row * ROW_SLABS, ROW_SLABS)
    return pltpu.make_async_copy(src_hbm.at[pl.ds(src0, ROW_SLABS), :], dst.at[pl.ds(dst0, ROW_SLABS), :], sem)


def _moe_expert_kernel(te_ref, src_ref, next_ref, h_hbm, wgu_ref, bgu_ref, wd_ref, bd_ref, y_ref, buf, sem):
    t = pl.program_id(0)
    slot = t % 2
    tm = src_ref.shape[1]

    def gather(idx_ref, s):
        def body(r, carry):
            _row_copy(h_hbm, idx_ref[0, r], buf.at[s], r, sem.at[s]).start()
            return carry
        lax.fori_loop(0, tm, body, 0, unroll=8)

    @pl.when(t == 0)
    def _():
        gather(src_ref, 0)

    @pl.when(t + 1 < pl.num_programs(0))
    def _():
        gather(next_ref, 1 - slot)

    pltpu.make_async_copy(h_hbm.at[pl.ds(0, tm * ROW_SLABS), :], buf.at[slot], sem.at[slot]).wait()

    lo, hi = _unpack_bf16_pair(_load_token_rows(buf, (slot,), tm))
    half = lo.shape[1]
    gu = (_dot(lo.astype(BF16), wgu_ref[:half, :]) + _dot(hi.astype(BF16), wgu_ref[half:, :])
          + bgu_ref[...])
    f = gu.shape[1] // 2
    g = jnp.minimum(gu[:, :f], SWIGLU_LIMIT)
    u = jnp.clip(gu[:, f:], -SWIGLU_LIMIT, SWIGLU_LIMIT)
    act = (u + 1.0) * g * _sigmoid(g * SWIGLU_ALPHA)
    y = _dot(act.astype(BF16), wd_ref[...]) + bd_ref[...]
    _store_token_rows(y_ref, (), _pack_bf16_pair(y[:, :half], y[:, half:]))


def moe_experts(h_packed, src_rows, tile_expert, wgu, bgu, wd, bd):
    tm = MOE_TM
    n_tiles = tile_expert.shape[0]
    d, f2 = wgu.shape[1], wgu.shape[2]
    src3 = src_rows.reshape(n_tiles, 1, tm)
    grid_spec = pltpu.PrefetchScalarGridSpec(
        num_scalar_prefetch=1,
        grid=(n_tiles,),
        in_specs=[pl.BlockSpec((None, 1, tm), lambda t, te: (t, 0, 0), memory_space=pltpu.SMEM),
                  pl.BlockSpec((None, 1, tm), lambda t, te: (jnp.minimum(t + 1, n_tiles - 1), 0, 0),
                               memory_space=pltpu.SMEM),
                  pl.BlockSpec(memory_space=pl.ANY),
                  pl.BlockSpec((None, d, f2), lambda t, te: (te[t], 0, 0)),
                  pl.BlockSpec((None, 1, f2), lambda t, te: (te[t], 0, 0)),
                  pl.BlockSpec((None, f2 // 2, d), lambda t, te: (te[t], 0, 0)),
                  pl.BlockSpec((None, 1, d), lambda t, te: (te[t], 0, 0))],
        out_specs=pl.BlockSpec((tm * ROW_SLABS, 128), lambda t, te: (t, 0)),
        scratch_shapes=[pltpu.VMEM((2, tm * ROW_SLABS, 128), h_packed.dtype), pltpu.SemaphoreType.DMA((2,))],
    )
    return pl.pallas_call(
        _moe_expert_kernel,
        grid_spec=grid_spec,
        out_shape=jax.ShapeDtypeStruct((n_tiles * tm * ROW_SLABS, 128), h_packed.dtype),
        compiler_params=_cparams(("arbitrary",)),
        name="moe_experts",
    )(tile_expert, src3, src3, h_packed, wgu, bgu, wd, bd)


def _moe_combine_kernel(dcur_ref, dnext_ref, wts_ref, x_ref, mod_ref, ys_hbm, o_ref, buf, sem, *, gi, n_ctx, tr):
    t = pl.program_id(0)
    slot = t % 2

    def gather(d_ref, s):
        def body(i, carry):
            for k in range(TOP_K):
                _row_copy(ys_hbm, d_ref[k, i], buf.at[s, k], i, sem.at[s]).start()
            return carry
        lax.fori_loop(0, tr, body, 0, unroll=4)

    @pl.when(t == 0)
    def _():
        gather(dcur_ref, 0)

    @pl.when(t + 1 < pl.num_programs(0))
    def _():
        gather(dnext_ref, 1 - slot)

    for k in range(TOP_K):
        pltpu.make_async_copy(ys_hbm.at[pl.ds(0, tr * ROW_SLABS), :], buf.at[slot, k], sem.at[slot]).wait()

    half = ROW_SLABS * 128
    acc_lo = jnp.zeros((tr, half), F32)
    acc_hi = jnp.zeros((tr, half), F32)
    for k in range(TOP_K):
        lo, hi = _unpack_bf16_pair(_load_token_rows(buf, (slot, k), tr))
        w = wts_ref[:, k:k + 1]
        acc_lo = acc_lo + w * lo
        acc_hi = acc_hi + w * hi
    is_ctx = _row_is_ctx(t * tr, tr, n_ctx)
    gate = jnp.where(is_ctx, mod_ref[1, gi:gi + 1, :], mod_ref[0, gi:gi + 1, :])
    o_ref[:, :half] = x_ref[:, :half] + gate[:, :half] * acc_lo
    o_ref[:, half:] = x_ref[:, half:] + gate[:, half:] * acc_hi


def moe_combine(ys, dest_t, wts, x, mods, gi, n_ctx):
    tr = MOE_COMBINE_TR
    n, d = x.shape
    nt = n // tr
    return pl.pallas_call(
        functools.partial(_moe_combine_kernel, gi=gi, n_ctx=n_ctx, tr=tr),
        grid=(nt,),
        in_specs=[pl.BlockSpec((TOP_K, tr), lambda t: (0, t), memory_space=pltpu.SMEM),
                  pl.BlockSpec((TOP_K, tr), lambda t: (0, jnp.minimum(t + 1, nt - 1)), memory_space=pltpu.SMEM),
                  pl.BlockSpec((tr, TOP_K), lambda t: (t, 0)),
                  pl.BlockSpec((tr, d), lambda t: (t, 0)),
                  pl.BlockSpec((2, N_MOD, d), lambda t: (0, 0, 0)),
                  pl.BlockSpec(memory_space=pl.ANY)],
        out_specs=pl.BlockSpec((tr, d), lambda t: (t, 0)),
        out_shape=jax.ShapeDtypeStruct((n, d), F32),
        scratch_shapes=[pltpu.VMEM((2, TOP_K, tr * ROW_SLABS, 128), ys.dtype), pltpu.SemaphoreType.DMA((2,))],
        compiler_params=_cparams(("arbitrary",)),
        name="moe_combine",
    )(dest_t, dest_t, wts, x, mods, ys)


def moe_routing_tables(idx, rank, counts, n_tiles, expert_base):
    tm = MOE_TM
    n = idx.shape[0]
    cnt = counts[0].astype(jnp.int32)
    padded = (cnt + tm - 1) // tm * tm
    ends = jnp.cumsum(padded)
    offs = ends - padded
    onehot = idx[:, :, None] == jnp.arange(N_EXPERTS, dtype=jnp.int32)
    dest = jnp.sum(jnp.where(onehot, offs, 0), axis=-1) + rank
    token = jnp.broadcast_to(jnp.arange(n, dtype=jnp.int32)[:, None], dest.shape)
    src_rows = jnp.zeros((n_tiles * tm,), jnp.int32).at[dest.reshape(-1)].set(token.reshape(-1))
    tile_start = jnp.arange(n_tiles, dtype=jnp.int32) * tm
    tile_expert = jnp.minimum(jnp.sum(tile_start[:, None] >= ends[None, :], axis=1), N_EXPERTS - 1)
    return dest.T, src_rows, tile_expert.astype(jnp.int32) + expert_base


def moe_ffn(x, norm_g, mods, router_w, router_b, wgu, bgu, wd, bd, expert_base, n_ctx):
    n = x.shape[0]
    n_tiles = -(-(TOP_K * n + N_EXPERTS * (MOE_TM - 1)) // MOE_TM)
    h_packed, idx, wts, rank, counts = norm_router(x, norm_g, mods, 3, router_w, router_b, n_ctx)
    dest_t, src_rows, tile_expert = moe_routing_tables(idx, rank, counts, n_tiles, expert_base)
    ys = moe_experts(h_packed, src_rows, tile_expert, wgu, bgu, wd, bd)
    return moe_combine(ys, dest_t, wts, x, mods, 5, n_ctx)


def _mm_tile(m):
    for tm in (1280, 1024, 640, 512, 256):
        if m % tm == 0:
            return tm
    raise ValueError(f"unsupported row count {m}")


def kernel(x, c, ctx, c_ctx, ada_w, ada_b, norm_mix, norm_ffn, w_in, w_out, hgrn_lb_logits, hgrn_norm,
           diff_lambda, diff_subln, na_rpb, router_w, router_b, expert_w_gu, expert_b_gu, expert_w_down,
           expert_b_down, final_norm_g):
    assert x.shape[0] == 1 and c.shape[0] == 1
    depth = ada_w.shape[0]
    n_ctx = ctx.shape[1]
    s_len = x.shape[1]
    n = n_ctx + s_len
    rows = s_len // GRID_W
    tm = _mm_tile(n)

    hall = jnp.concatenate([ctx[0], x[0]], axis=0)
    c8 = jnp.zeros((8, D_MODEL), F32).at[0].set(c[0]).at[1].set(c_ctx)
    mods_all = ada_mod(c8, ada_w, ada_b)[:, :2].reshape(depth, 2, N_MOD, D_MODEL)

    p_lb = jax.nn.softmax(hgrn_lb_logits.astype(F32), axis=0)
    lower_bounds = jnp.clip(jnp.cumsum(p_lb, axis=0) - p_lb[0], 0.0, 1.0)
    cos, sin = rope_tables(n, n_ctx)

    wgu_all = deinterleave_gate_up(expert_w_gu.reshape(depth * N_EXPERTS, D_MODEL, 2 * D_EXPERT))
    bgu_all = jnp.concatenate([expert_b_gu[..., 0::2], expert_b_gu[..., 1::2]], axis=-1)
    bgu_all = bgu_all.reshape(depth * N_EXPERTS, 1, 2 * D_EXPERT)
    wd_all = expert_w_down.reshape(depth * N_EXPERTS, D_EXPERT, D_MODEL).astype(BF16)
    bd_all = expert_b_down.reshape(depth * N_EXPERTS, 1, D_MODEL)
    tk = n // 13 if n % (13 * 256) == 0 else 256

    for layer in range(depth):
        lambda_init = 0.8 - 0.6 * math.exp(-0.3 * layer)
        mods = mods_all[layer]
        w_in_b = w_in[layer].astype(BF16)
        w_out_b = w_out[layer].astype(BF16)

        h = norm_mod(hall, norm_mix[layer], mods, 0, n_ctx)
        p = matmul(h, w_in_b, tm, 1024, BF16)

        ya = hgrn_mixer(p, lower_bounds[layer], hgrn_norm[layer], n_ctx)
        qr, kt = rope_qk(p, cos, sin)
        yb = diff_attention(qr, kt, p, diff_lambda[layer], diff_subln[layer], lambda_init,
                            n_ctx, s_len, n, 256, tk)
        yb = diff_attention(qr, kt, p, diff_lambda[layer], diff_subln[layer], lambda_init,
                            0, n_ctx, n_ctx, n_ctx, n_ctx, out_rows=yb)
        yc = na_attention(p, na_bias_tables(na_rpb[layer], rows), n_ctx)
        yc = na_ctx_attention(p, n_ctx, yc)
        hall = matmul_residual([ya, yb, yc], w_out_b, hall, mods, 2, n_ctx, tm, 512)
        hall = moe_ffn(hall, norm_ffn[layer], mods, router_w[layer], router_b[layer], wgu_all, bgu_all,
                       wd_all, bd_all, layer * N_EXPERTS, n_ctx)

    return final_norm(hall, final_norm_g, n_ctx)[None]
```

```python
import functools
import math

import numpy as np
import jax
import jax.numpy as jnp
from jax import lax
from jax.experimental import pallas as pl
from jax.experimental.pallas import tpu as pltpu

F32 = jnp.float32
BF16 = jnp.bfloat16

D_MODEL = 4096
HEAD_DIM = 128
GRID_W = 64
A_HEADS = 12
A_WIDTH = A_HEADS * HEAD_DIM
B_HEADS = 5
B_WIDTH = B_HEADS * 2 * HEAD_DIM
C_HEADS = 10
C_WIDTH = C_HEADS * HEAD_DIM
A_COLS = 5 * A_WIDTH
B_COLS = 3 * B_WIDTH
C_COLS = 3 * C_WIDTH
IN_COLS = A_COLS + B_COLS + C_COLS
NA_KH = 8
NA_KW = 16
ROPE_BASE = 10000.0
N_EXPERTS = 32
TOP_K = 4
D_EXPERT = 512
SWIGLU_ALPHA = 1.702
SWIGLU_LIMIT = 7.0
N_MOD = 6
NORM_EPS = 1e-6
F_TINY = 1e-30
MASK_NEG = -1e30
LOG2_E = 1.4426950408889634

V7X_VMEM_LIMIT_BYTES = 56 * 1024 * 1024

SCAN_CHUNK = 64
SCAN_SUB = 16
NA_QROWS = 4
NA_QBLK = NA_QROWS * GRID_W
NA_KBLKS = 3


def _cparams(sem):
    return pltpu.CompilerParams(dimension_semantics=sem, vmem_limit_bytes=V7X_VMEM_LIMIT_BYTES)


def _dot(a, b):
    return jnp.dot(a, b, preferred_element_type=F32)


def _dot_nt(a, b):
    return lax.dot_general(a, b, (((1,), (1,)), ((), ())), preferred_element_type=F32)


def _dot_tn(a, b):
    return lax.dot_general(a, b, (((0,), (0,)), ((), ())), preferred_element_type=F32)


def _sigmoid(x):
    return 1.0 / (1.0 + jnp.exp(-x))


def _row_is_ctx(row0, rows, n_ctx):
    return (row0 + lax.broadcasted_iota(jnp.int32, (rows, 1), 0)) < n_ctx


def _ada_kernel(c_ref, w_ref, b_ref, o_ref):
    c = c_ref[...]
    a = (c * _sigmoid(c)).astype(BF16)
    o_ref[...] = _dot(a, w_ref[...].astype(BF16)) + b_ref[...]


def ada_mod(c8, ada_w, ada_b, tn=512):
    depth, d, n = ada_w.shape
    return pl.pallas_call(
        _ada_kernel,
        grid=(depth, n // tn),
        in_specs=[pl.BlockSpec((8, d), lambda l, j: (0, 0)),
                  pl.BlockSpec((None, d, tn), lambda l, j: (l, 0, j)),
                  pl.BlockSpec((None, 1, tn), lambda l, j: (l, 0, j))],
        out_specs=pl.BlockSpec((None, 8, tn), lambda l, j: (l, 0, j)),
        out_shape=jax.ShapeDtypeStruct((depth, 8, n), F32),
        compiler_params=_cparams(("arbitrary", "arbitrary")),
        name="ada_mod",
    )(c8, ada_w, ada_b.reshape(depth, 1, n))


def _norm_mod(x, g, mod_ref, si, row0, n_ctx):
    ms = jnp.mean(x * x, axis=-1, keepdims=True)
    y = x * lax.rsqrt(ms + NORM_EPS) * g
    is_ctx = _row_is_ctx(row0, x.shape[0], n_ctx)
    shift = jnp.where(is_ctx, mod_ref[1, si:si + 1, :], mod_ref[0, si:si + 1, :])
    scale = jnp.where(is_ctx, mod_ref[1, si + 1:si + 2, :], mod_ref[0, si + 1:si + 2, :])
    return y * (1.0 + scale) + shift


def _norm_kernel(x_ref, g_ref, mod_ref, o_ref, *, si, n_ctx, tr):
    row0 = pl.program_id(0) * tr
    o_ref[...] = _norm_mod(x_ref[...], g_ref[...], mod_ref, si, row0, n_ctx).astype(o_ref.dtype)


def norm_mod(x, g, mods, si, n_ctx, tr=256):
    n, d = x.shape
    return pl.pallas_call(
        functools.partial(_norm_kernel, si=si, n_ctx=n_ctx, tr=tr),
        grid=(n // tr,),
        in_specs=[pl.BlockSpec((tr, d), lambda i: (i, 0)),
                  pl.BlockSpec((1, d), lambda i: (0, 0)),
                  pl.BlockSpec((2, N_MOD, d), lambda i: (0, 0, 0))],
        out_specs=pl.BlockSpec((tr, d), lambda i: (i, 0)),
        out_shape=jax.ShapeDtypeStruct((n, d), BF16),
        compiler_params=_cparams(("parallel",)),
        name="norm_mod",
    )(x, g.reshape(1, d), mods)


def _split_bf16(x):
    hi = x.astype(BF16)
    lo = (x - hi.astype(F32)).astype(BF16)
    return hi, lo


def _pack_bf16_pair(lo, hi):
    lo_b = lax.bitcast_convert_type(lo.astype(BF16).astype(F32), jnp.uint32) >> 16
    hi_b = lax.bitcast_convert_type(hi.astype(BF16).astype(F32), jnp.uint32) & jnp.uint32(0xFFFF0000)
    return hi_b | lo_b


def _unpack_bf16_pair(w):
    lo = lax.bitcast_convert_type(w << 16, F32)
    hi = lax.bitcast_convert_type(w & jnp.uint32(0xFFFF0000), F32)
    return lo, hi


def _norm_router_kernel(x_ref, g_ref, mod_ref, rw_ref, rb_ref, h_ref, idx_ref, wts_ref, rank_ref, cnt_ref,
                        carry_ref, *, si, n_ctx, tr):
    @pl.when(pl.program_id(0) == 0)
    def _():
        carry_ref[...] = jnp.zeros_like(carry_ref)

    row0 = pl.program_id(0) * tr
    h = _norm_mod(x_ref[...], g_ref[...], mod_ref, si, row0, n_ctx)
    half = h.shape[1] // 2
    h_ref[...] = _pack_bf16_pair(h[:, :half], h[:, half:])
    hh, hl = _split_bf16(h)
    wh, wl = _split_bf16(rw_ref[...])
    logits = _dot(hh, wh) + _dot(hh, wl) + _dot(hl, wh) + rb_ref[...]
    iota = lax.broadcasted_iota(jnp.int32, logits.shape, 1)
    work = logits
    vals, sels = [], []
    for k in range(TOP_K):
        m = jnp.max(work, axis=-1, keepdims=True)
        idx = jnp.min(jnp.where(work == m, iota, N_EXPERTS), axis=-1, keepdims=True)
        sel = iota == idx
        vals.append(m)
        sels.append(sel)
        idx_ref[:, k:k + 1] = idx
        work = jnp.where(sel, -jnp.inf, work)
    es = [jnp.exp(v - vals[0]) for v in vals]
    tot = es[0] + es[1] + es[2] + es[3]
    for k in range(TOP_K):
        wts_ref[:, k:k + 1] = es[k] / tot
    used = jnp.zeros_like(logits)
    for k in range(TOP_K):
        used = used + jnp.where(sels[k], 1.0, 0.0)
    r = lax.broadcasted_iota(jnp.int32, (tr, tr), 0)
    c = lax.broadcasted_iota(jnp.int32, (tr, tr), 1)
    before = jnp.where(c < r, 1.0, 0.0).astype(BF16)
    prefix = _dot(before, used.astype(BF16)) + carry_ref[...]
    for k in range(TOP_K):
        rank = jnp.sum(jnp.where(sels[k], prefix, 0.0), axis=-1, keepdims=True)
        rank_ref[:, k:k + 1] = rank.astype(jnp.int32)
    carry_ref[...] += jnp.sum(used, axis=0, keepdims=True)
    cnt_ref[...] = carry_ref[...]


def norm_router(x, g, mods, si, router_w, router_b, n_ctx, tr=256):
    n, d = x.shape
    col = lambda i: (i, 0)
    fixed = lambda i: (0, 0)
    return pl.pallas_call(
        functools.partial(_norm_router_kernel, si=si, n_ctx=n_ctx, tr=tr),
        grid=(n // tr,),
        in_specs=[pl.BlockSpec((tr, d), col),
                  pl.BlockSpec((1, d), fixed),
                  pl.BlockSpec((2, N_MOD, d), lambda i: (0, 0, 0)),
                  pl.BlockSpec((d, N_EXPERTS), fixed),
                  pl.BlockSpec((1, N_EXPERTS), fixed)],
        out_specs=[pl.BlockSpec((tr, d // 2), col),
                   pl.BlockSpec((tr, TOP_K), col),
                   pl.BlockSpec((tr, TOP_K), col),
                   pl.BlockSpec((tr, TOP_K), col),
                   pl.BlockSpec((1, N_EXPERTS), fixed)],
        out_shape=[jax.ShapeDtypeStruct((n, d // 2), jnp.uint32),
                   jax.ShapeDtypeStruct((n, TOP_K), jnp.int32),
                   jax.ShapeDtypeStruct((n, TOP_K), F32),
                   jax.ShapeDtypeStruct((n, TOP_K), jnp.int32),
                   jax.ShapeDtypeStruct((1, N_EXPERTS), F32)],
        scratch_shapes=[pltpu.VMEM((1, N_EXPERTS), F32)],
        compiler_params=_cparams(("arbitrary",)),
        name="norm_router",
    )(x, g.reshape(1, d), mods, router_w, router_b.reshape(1, N_EXPERTS))


def _final_norm_kernel(x_ref, g_ref, o_ref):
    x = x_ref[...]
    ms = jnp.mean(x * x, axis=-1, keepdims=True)
    o_ref[...] = x * lax.rsqrt(ms + NORM_EPS) * g_ref[...]


def final_norm(x, g, n_ctx, tr=256):
    n, d = x.shape
    skip = n_ctx // tr
    return pl.pallas_call(
        _final_norm_kernel,
        grid=((n - n_ctx) // tr,),
        in_specs=[pl.BlockSpec((tr, d), lambda i: (i + skip, 0)),
                  pl.BlockSpec((1, d), lambda i: (0, 0))],
        out_specs=pl.BlockSpec((tr, d), lambda i: (i, 0)),
        out_shape=jax.ShapeDtypeStruct((n - n_ctx, d), F32),
        compiler_params=_cparams(("parallel",)),
        name="final_norm",
    )(x, g.reshape(1, d))


def _mm_kernel(a_ref, b_ref, o_ref):
    o_ref[...] = _dot(a_ref[...], b_ref[...]).astype(o_ref.dtype)


def matmul(a, b, tm, tn, out_dtype):
    m, k = a.shape
    n = b.shape[1]
    return pl.pallas_call(
        _mm_kernel,
        grid=(m // tm, n // tn),
        in_specs=[pl.BlockSpec((tm, k), lambda i, j: (i, 0)),
                  pl.BlockSpec((k, tn), lambda i, j: (0, j))],
        out_specs=pl.BlockSpec((tm, tn), lambda i, j: (i, j)),
        out_shape=jax.ShapeDtypeStruct((m, n), out_dtype),
        compiler_params=_cparams(("parallel", "arbitrary")),
        name="matmul",
    )(a, b)


def _mm_res_kernel(*refs, gi, n_ctx, tm, widths):
    a_refs = refs[:len(widths)]
    b_ref, x_ref, mod_ref, o_ref = refs[len(widths):]
    row0 = pl.program_id(0) * tm
    is_ctx = _row_is_ctx(row0, tm, n_ctx)
    gate = jnp.where(is_ctx, mod_ref[1, gi:gi + 1, :], mod_ref[0, gi:gi + 1, :])
    acc = None
    k0 = 0
    for a_ref, w in zip(a_refs, widths):
        part = _dot(a_ref[...], b_ref[k0:k0 + w, :])
        acc = part if acc is None else acc + part
        k0 += w
    o_ref[...] = x_ref[...] + gate * acc


def matmul_residual(a_parts, b, x, mods, gi, n_ctx, tm, tn):
    m = x.shape[0]
    k, n = b.shape
    widths = tuple(a.shape[1] for a in a_parts)
    assert sum(widths) == k
    return pl.pallas_call(
        functools.partial(_mm_res_kernel, gi=gi, n_ctx=n_ctx, tm=tm, widths=widths),
        grid=(m // tm, n // tn),
        in_specs=[pl.BlockSpec((tm, w), lambda i, j: (i, 0)) for w in widths]
        + [pl.BlockSpec((k, tn), lambda i, j: (0, j)),
           pl.BlockSpec((tm, tn), lambda i, j: (i, j)),
           pl.BlockSpec((2, N_MOD, tn), lambda i, j: (0, 0, j))],
        out_specs=pl.BlockSpec((tm, tn), lambda i, j: (i, j)),
        out_shape=jax.ShapeDtypeStruct((m, n), F32),
        compiler_params=_cparams(("parallel", "arbitrary")),
        name="matmul_residual",
    )(*a_parts, b, x, mods)


def _hgrn_masks(c, sub, reverse):
    r = lax.broadcasted_iota(jnp.int32, (c, c), 0)
    s = lax.broadcasted_iota(jnp.int32, (c, c), 1)
    tri = (s >= r) if reverse else (s <= r)
    same_sub = (r // sub) == (s // sub)
    return tri, same_sub


def _hgrn_head(q_raw, f_pre, v, lb, st_in, ck_ref, *, reverse, sel, tri_b, same_sub):
    c, d = q_raw.shape
    sub = SCAN_SUB
    nsub = c // sub
    q = q_raw * _sigmoid(q_raw) * (HEAD_DIM ** -0.5)
    f = lb + (1.0 - lb) * _sigmoid(f_pre)
    log_f = jnp.log2(jnp.maximum(f, F_TINY))
    k = (1.0 - lb) * _sigmoid(-f_pre)

    hi = log_f.astype(BF16)
    r1 = log_f - hi.astype(F32)
    mid = r1.astype(BF16)
    lo = (r1 - mid.astype(F32)).astype(BF16)
    yield
    cum = _dot(tri_b, hi) + _dot(tri_b, mid) + _dot(tri_b, lo)
    yield
    cum_k = cum - jnp.log2(k)
    ck_ref[...] = cum_k

    order = list(range(nsub))
    if reverse:
        order = order[::-1]
    zero_row = jnp.zeros((1, d), F32)
    bnd_in, bnd_out = {}, {}
    prev = zero_row
    for i in order:
        last = i * sub if reverse else i * sub + sub - 1
        bnd_in[i] = prev
        bnd_out[i] = cum[last:last + 1, :]
        prev = bnd_out[i]
    cum_end = prev

    t_in_sub = lax.broadcasted_iota(jnp.int32, (sub, 1), 0)
    q_hat, u_rows, off_rows = [], [], []
    for i in range(nsub):
        sl = slice(i * sub, (i + 1) * sub)
        q_i, cum_i = q[sl], cum[sl]
        q_hat_i = q_i * jnp.exp2(cum_i - bnd_in[i])
        k_parts = []
        for j in range(nsub):
            earlier = (j > i) if reverse else (j < i)
            sj = slice(j * sub, (j + 1) * sub)
            if earlier:
                k_parts.append(jnp.exp2(bnd_in[i] - cum_k[sj]))
            else:
                k_parts.append(jnp.zeros((sub, d), F32))
        k_til = jnp.concatenate(k_parts, axis=0).astype(BF16)
        off_rows.append(_dot_nt(q_hat_i.astype(BF16), k_til))
        yield
        u_parts = []
        for s in range(sub):
            groups = []
            for g8 in range(sub // 8):
                rows = slice(g8 * 8, g8 * 8 + 8)
                if g8 == s // 8:
                    rows_t = t_in_sub[rows]
                    ok = (rows_t <= s) if reverse else (rows_t >= s)
                    e = jnp.where(ok, cum_i[rows] - ck_ref[i * sub + s:i * sub + s + 1, :], MASK_NEG)
                elif (g8 < s // 8) == reverse:
                    e = cum_i[rows] - ck_ref[i * sub + s:i * sub + s + 1, :]
                else:
                    groups.append(jnp.zeros((8, d), BF16))
                    continue
                groups.append((q_i[rows] * jnp.exp2(e)).astype(BF16))
            u_parts.append(jnp.concatenate(groups, axis=0))
        u_rows.append(jnp.concatenate(u_parts, axis=1))
        yield
    u = jnp.concatenate(u_rows, axis=0)
    diag = _dot(u, sel)
    v_b = v.astype(BF16)
    k_end = jnp.exp2(cum_end - cum_k).astype(BF16)
    st_out = st_in * jnp.exp2(cum_end) + _dot_tn(v_b, k_end)
    o_state = _dot_nt((q * jnp.exp2(cum)).astype(BF16), st_in.astype(BF16))
    yield
    attn = jnp.concatenate(off_rows, axis=0) + jnp.where(same_sub, diag, 0.0)
    o = _dot(attn.astype(BF16), v_b) + o_state
    return o, st_out


def _run_interleaved(gens):
    results = [None] * len(gens)
    live = list(range(len(gens)))
    while live:
        for idx in list(live):
            try:
                next(gens[idx])
            except StopIteration as done:
                results[idx] = done.value
                live.remove(idx)
    return results


def _hgrn_sel(c, sub):
    rows = np.arange(sub * HEAD_DIM) // HEAD_DIM
    cols = np.arange(c) % sub
    return jnp.asarray((rows[:, None] == cols[None, :]).astype(np.float32), dtype=BF16)


def _hgrn_fw_kernel(q_ref, f_ref, v_ref, lb_ref, sel_ref, o_ref, st_ref, ck_ref, *, hb):
    @pl.when(pl.program_id(1) == 0)
    def _():
        st_ref[...] = jnp.zeros_like(st_ref)

    sel = sel_ref[...]
    tri, same_sub = _hgrn_masks(q_ref.shape[0], SCAN_SUB, False)
    tri_b = jnp.where(tri, 1.0, 0.0).astype(BF16)
    cols = [slice(h * HEAD_DIM, (h + 1) * HEAD_DIM) for h in range(hb)]
    heads = _run_interleaved([
        _hgrn_head(q_ref[:, cs].astype(F32), f_ref[:, cs].astype(F32), v_ref[:, cs].astype(F32),
                   lb_ref[0:1, cs], st_ref[h], ck_ref.at[h], reverse=False, sel=sel, tri_b=tri_b,
                   same_sub=same_sub)
        for h, cs in enumerate(cols)])
    for h, (cs, (o, st)) in enumerate(zip(cols, heads)):
        st_ref[h] = st
        o_ref[:, cs] = o


def _hgrn_bw_kernel(q_ref, f_ref, v_ref, g_ref, ofw_ref, lb_ref, ng_ref, sel_ref, y_ref, st_ref, ck_ref, *, hb):
    @pl.when(pl.program_id(1) == 0)
    def _():
        st_ref[...] = jnp.zeros_like(st_ref)

    sel = sel_ref[...]
    tri, same_sub = _hgrn_masks(q_ref.shape[0], SCAN_SUB, True)
    tri_b = jnp.where(tri, 1.0, 0.0).astype(BF16)
    cols = [slice(h * HEAD_DIM, (h + 1) * HEAD_DIM) for h in range(hb)]
    heads = _run_interleaved([
        _hgrn_head(q_ref[:, cs].astype(F32), f_ref[:, cs].astype(F32), v_ref[:, cs].astype(F32),
                   lb_ref[1:2, cs], st_ref[h], ck_ref.at[h], reverse=True, sel=sel, tri_b=tri_b,
                   same_sub=same_sub)
        for h, cs in enumerate(cols)])
    for h, (cs, (o, st)) in enumerate(zip(cols, heads)):
        st_ref[h] = st
        o = o + ofw_ref[:, cs]
        ms = jnp.mean(o * o, axis=-1, keepdims=True)
        y = o * lax.rsqrt(ms + NORM_EPS) * ng_ref[...]
        g = g_ref[:, cs].astype(F32)
        y_ref[:, cs] = (y * (g * _sigmoid(g))).astype(y_ref.dtype)


def hgrn_mixer(p, lb, norm_g, n_ctx, hb=A_HEADS):
    n = p.shape[0]
    c = SCAN_CHUNK
    nc = n // c
    nc_ctx = n_ctx // c
    ngroups = A_HEADS // hb
    w = hb * HEAD_DIM
    per = A_WIDTH // w
    sel = _hgrn_sel(c, SCAN_SUB)

    def fw_chunk(s):
        return s

    def bw_chunk(s):
        return jnp.where(s < nc_ctx, nc_ctx - 1 - s, nc - 1 - (s - nc_ctx))

    def col(section, chunk_of):
        return pl.BlockSpec((c, w), lambda g, s: (chunk_of(s), section * per + g))

    lb_spec = pl.BlockSpec((2, w), lambda g, s: (0, g))
    sel_spec = pl.BlockSpec(sel.shape, lambda g, s: (0, 0))
    scratch = [pltpu.VMEM((hb, HEAD_DIM, HEAD_DIM), F32), pltpu.VMEM((hb, c, HEAD_DIM), F32)]
    o_fw = pl.pallas_call(
        functools.partial(_hgrn_fw_kernel, hb=hb),
        grid=(ngroups, nc),
        in_specs=[col(0, fw_chunk), col(1, fw_chunk), col(3, fw_chunk), lb_spec, sel_spec],
        out_specs=pl.BlockSpec((c, w), lambda g, s: (s, g)),
        out_shape=jax.ShapeDtypeStruct((n, A_WIDTH), F32),
        scratch_shapes=scratch,
        compiler_params=_cparams(("parallel", "arbitrary")),
        name="hgrn_fw",
    )(p, p, p, lb, sel)
    return pl.pallas_call(
        functools.partial(_hgrn_bw_kernel, hb=hb),
        grid=(ngroups, nc),
        in_specs=[col(0, bw_chunk), col(2, bw_chunk), col(3, bw_chunk), col(4, bw_chunk),
                  pl.BlockSpec((c, w), lambda g, s: (bw_chunk(s), g)),
                  lb_spec, pl.BlockSpec((1, HEAD_DIM), lambda g, s: (0, 0)), sel_spec],
        out_specs=pl.BlockSpec((c, w), lambda g, s: (bw_chunk(s), g)),
        out_shape=jax.ShapeDtypeStruct((n, A_WIDTH), BF16),
        scratch_shapes=scratch,
        compiler_params=_cparams(("parallel", "arbitrary")),
        name="hgrn_bw",
    )(p, p, p, p, o_fw, lb, norm_g.reshape(1, HEAD_DIM), sel)


def rope_tables(n, n_ctx):
    t = jnp.arange(n - n_ctx, dtype=jnp.int32)
    half = HEAD_DIM // 4
    inv_freq = ROPE_BASE ** (-jnp.arange(half, dtype=F32) / half)
    ang_r = (t // GRID_W).astype(F32)[:, None] * inv_freq
    ang_c = (t % GRID_W).astype(F32)[:, None] * inv_freq
    cos = jnp.concatenate([jnp.cos(ang_r)] * 2 + [jnp.cos(ang_c)] * 2, axis=-1)
    sin = jnp.concatenate([-jnp.sin(ang_r), jnp.sin(ang_r), -jnp.sin(ang_c), jnp.sin(ang_c)], axis=-1)
    cos = jnp.concatenate([jnp.ones((n_ctx, HEAD_DIM), F32), cos], axis=0)
    sin = jnp.concatenate([jnp.zeros((n_ctx, HEAD_DIM), F32), sin], axis=0)
    return cos, sin


def _rope_kernel(q_ref, k_ref, cos_ref, sin_ref, qo_ref, kt_ref):
    cos = cos_ref[...]
    sin = sin_ref[...]
    lane = lax.broadcasted_iota(jnp.int32, cos.shape, 1)
    low = (lane & (HEAD_DIM // 4)) == 0

    def rotate(x):
        partner = jnp.where(low, pltpu.roll(x, HEAD_DIM - HEAD_DIM // 4, 1), pltpu.roll(x, HEAD_DIM // 4, 1))
        return x * cos + partner * sin

    for h in range(2 * B_HEADS):
        cs = slice(h * HEAD_DIM, (h + 1) * HEAD_DIM)
        qo_ref[:, cs] = (rotate(q_ref[:, cs].astype(F32)) * (LOG2_E * HEAD_DIM ** -0.5)).astype(qo_ref.dtype)
        kt_ref[cs, :] = rotate(k_ref[:, cs].astype(F32)).T.astype(kt_ref.dtype)


def rope_qk(p, cos, sin, tr=256):
    n = p.shape[0]
    qblk = A_COLS // B_WIDTH
    return pl.pallas_call(
        _rope_kernel,
        grid=(n // tr,),
        in_specs=[pl.BlockSpec((tr, B_WIDTH), lambda i: (i, qblk)),
                  pl.BlockSpec((tr, B_WIDTH), lambda i: (i, qblk + 1)),
                  pl.BlockSpec((tr, HEAD_DIM), lambda i: (i, 0)),
                  pl.BlockSpec((tr, HEAD_DIM), lambda i: (i, 0))],
        out_specs=[pl.BlockSpec((tr, B_WIDTH), lambda i: (i, 0)),
                   pl.BlockSpec((B_WIDTH, tr), lambda i: (0, i))],
        out_shape=[jax.ShapeDtypeStruct((n, B_WIDTH), BF16), jax.ShapeDtypeStruct((B_WIDTH, n), BF16)],
        compiler_params=_cparams(("parallel",)),
        name="rope_qk",
    )(p, p, cos, sin)


def _diff_attn_kernel(lam_ref, g_ref, q_ref, kt_ref, v_ref, *rest, lambda_init, nk, tk):
    o_ref, m_scr, l_scr, acc_scr, sa_scr, sb_scr = rest[-6:]
    tq = q_ref.shape[0]
    q = q_ref[...]
    q1 = q[:, :HEAD_DIM]
    q2 = q[:, HEAD_DIM:]
    m_scr[...] = jnp.full_like(m_scr, -jnp.inf)
    l_scr[...] = jnp.zeros_like(l_scr)
    acc_scr[...] = jnp.zeros_like(acc_scr)

    def scores(j, s_scr):
        off = pl.multiple_of(j * tk, 128)
        s_scr[:tq, :] = _dot(q1, kt_ref[:HEAD_DIM, pl.ds(off, tk)])
        s_scr[tq:, :] = _dot(q2, kt_ref[HEAD_DIM:, pl.ds(off, tk)])

    def accumulate(j, s_scr):
        off = pl.multiple_of(j * tk, tk)
        s = s_scr[...]
        m_prev = m_scr[...]
        m_new = jnp.maximum(m_prev, jnp.max(s, axis=-1, keepdims=True))
        alpha = jnp.exp2(m_prev - m_new)
        p = jnp.exp2(s - m_new)
        l_scr[...] = alpha * l_scr[...] + jnp.sum(p, axis=-1, keepdims=True)
        acc_scr[...] = alpha * acc_scr[...] + _dot(p.astype(BF16), v_ref[pl.ds(off, tk), :])
        m_scr[...] = m_new

    scores(0, sa_scr)

    def pair(jj, carry):
        j = 2 * jj
        scores(j + 1, sb_scr)
        accumulate(j, sa_scr)
        scores(jnp.minimum(j + 2, nk - 1), sa_scr)
        accumulate(j + 1, sb_scr)
        return carry

    lax.fori_loop(0, nk // 2, pair, 0)
    if nk % 2:
        accumulate(nk - 1, sa_scr)

    lp = lam_ref[...]
    lam = (jnp.exp(jnp.sum(lp[0:1] * lp[1:2], axis=-1, keepdims=True))
           - jnp.exp(jnp.sum(lp[2:3] * lp[3:4], axis=-1, keepdims=True)) + lambda_init)
    o = acc_scr[:tq] / l_scr[:tq] - lam * (acc_scr[tq:] / l_scr[tq:])
    ms = jnp.mean(o * o, axis=-1, keepdims=True)
    y = o * lax.rsqrt(ms + NORM_EPS) * g_ref[...] * (1.0 - lambda_init)
    o_ref[...] = y.astype(o_ref.dtype)


def diff_attention(q, kt, p, lam_p, subln_g, lambda_init, q_row0, n_q, n_kv, tq, tk, out_rows=None):
    hw = 2 * HEAD_DIM
    n = q.shape[0]
    nq, nk = n_q // tq, n_kv // tk
    vblk = (A_COLS + 2 * B_WIDTH) // hw
    qb0 = q_row0 // tq
    q_spec = pl.BlockSpec((tq, hw), lambda h, i: (i + qb0, h))
    in_specs = [pl.BlockSpec((4, HEAD_DIM), lambda h, i: (0, 0)),
                pl.BlockSpec((1, hw), lambda h, i: (0, 0)),
                q_spec,
                pl.BlockSpec((hw, n_kv), lambda h, i: (h, 0)),
                pl.BlockSpec((n_kv, hw), lambda h, i: (0, vblk + h))]
    args = [lam_p, subln_g.reshape(1, hw), q, kt, p]
    aliases = {}
    if out_rows is not None:
        in_specs.append(pl.BlockSpec(memory_space=pl.ANY))
        args.append(out_rows)
        aliases = {len(args) - 1: 0}
    return pl.pallas_call(
        functools.partial(_diff_attn_kernel, lambda_init=lambda_init, nk=nk, tk=tk),
        grid=(B_HEADS, nq),
        in_specs=in_specs,
        out_specs=q_spec,
        out_shape=jax.ShapeDtypeStruct((n, B_WIDTH), BF16),
        scratch_shapes=[pltpu.VMEM((2 * tq, 1), F32), pltpu.VMEM((2 * tq, 1), F32),
                        pltpu.VMEM((2 * tq, hw), F32),
                        pltpu.VMEM((2 * tq, tk), F32), pltpu.VMEM((2 * tq, tk), F32)],
        input_output_aliases=aliases,
        compiler_params=_cparams(("parallel", "arbitrary")),
        name="diff_attention",
    )(*args)


def na_bias_tables(rpb, rows):
    krows = NA_KBLKS * NA_QROWS
    n_dr, n_dc = 2 * NA_KH - 1, 2 * NA_KW - 1
    qc = np.arange(GRID_W)[:, None]
    kc = np.arange(GRID_W)[None, :]
    cs = np.clip(qc - NA_KW // 2, 0, GRID_W - NA_KW)
    col_ok = (kc >= cs) & (kc < cs + NA_KW)
    dc = np.clip(kc - qc + NA_KW - 1, 0, n_dc - 1)
    onehot = (np.arange(n_dc)[:, None, None] == dc[None]).astype(np.float32).reshape(n_dc, GRID_W * GRID_W)
    toep = jnp.dot(rpb.astype(F32).reshape(C_HEADS * n_dr, n_dc), jnp.asarray(onehot),
                   precision=lax.Precision.HIGHEST).reshape(C_HEADS, n_dr, GRID_W, GRID_W)
    toep = jnp.where(jnp.asarray(col_ok)[None, None], toep, MASK_NEG)
    masked = jnp.full((C_HEADS, GRID_W, GRID_W), MASK_NEG, F32)
    cases = [(0, 0), (NA_QROWS, 0), (rows - NA_QROWS, rows - krows)]
    out = []
    for r0, start in cases:
        q_blocks = []
        for qr in range(NA_QROWS):
            r = r0 + qr
            rs = min(max(r - NA_KH // 2, 0), rows - NA_KH)
            k_blocks = []
            for kr in range(krows):
                kra = start + kr
                k_blocks.append(toep[:, kra - r + NA_KH - 1] if rs <= kra < rs + NA_KH else masked)
            q_blocks.append(jnp.concatenate(k_blocks, axis=-1))
        out.append(jnp.concatenate(q_blocks, axis=1))
    return jnp.stack(out)


def _na_kernel(q_ref, k0_ref, k1_ref, k2_ref, kc_ref, v0_ref, v1_ref, v2_ref, vc_ref, bias_ref, o_ref):
    scale = HEAD_DIM ** -0.5
    k_refs = (k0_ref, k1_ref, k2_ref)
    v_refs = (v0_ref, v1_ref, v2_ref, vc_ref)
    for h in range(C_HEADS):
        cs = slice(h * HEAD_DIM, (h + 1) * HEAD_DIM)
        q = q_ref[:, cs]
        s_parts = [_dot_nt(q, k_refs[b][:, cs]) * scale + bias_ref[h, :, b * NA_QBLK:(b + 1) * NA_QBLK]
                   for b in range(NA_KBLKS)]
        s_parts.append(_dot_nt(q, kc_ref[:, cs]) * scale)
        m = s_parts[0].max(axis=-1, keepdims=True)
        for s in s_parts[1:]:
            m = jnp.maximum(m, s.max(axis=-1, keepdims=True))
        l = jnp.zeros_like(m)
        acc = jnp.zeros((q.shape[0], HEAD_DIM), F32)
        for s, v_ref in zip(s_parts, v_refs):
            pexp = jnp.exp(s - m)
            l = l + pexp.sum(axis=-1, keepdims=True)
            acc = acc + _dot(pexp.astype(BF16), v_ref[:, cs])
        o_ref[:, cs] = (acc / l).astype(o_ref.dtype)


def na_attention(p, bias, n_ctx):
    n = p.shape[0]
    nqb = (n - n_ctx) // NA_QBLK
    cb = n_ctx // NA_QBLK
    qblk = (A_COLS + B_COLS) // C_WIDTH

    def kv_spec(section, b):
        return pl.BlockSpec((NA_QBLK, C_WIDTH),
                            lambda i: (cb + jnp.clip(i - 1, 0, nqb - NA_KBLKS) + b, qblk + section))

    def ctx_spec(section):
        return pl.BlockSpec((n_ctx, C_WIDTH), lambda i: (0, qblk + section))

    def case(i):
        return jnp.where(i == 0, 0, jnp.where(i == nqb - 1, 2, 1))

    return pl.pallas_call(
        _na_kernel,
        grid=(nqb,),
        in_specs=[pl.BlockSpec((NA_QBLK, C_WIDTH), lambda i: (cb + i, qblk)),
                  kv_spec(1, 0), kv_spec(1, 1), kv_spec(1, 2), ctx_spec(1),
                  kv_spec(2, 0), kv_spec(2, 1), kv_spec(2, 2), ctx_spec(2),
                  pl.BlockSpec((None, C_HEADS, NA_QBLK, NA_KBLKS * NA_QBLK), lambda i: (case(i), 0, 0, 0))],
        out_specs=pl.BlockSpec((NA_QBLK, C_WIDTH), lambda i: (cb + i, 0)),
        out_shape=jax.ShapeDtypeStruct((n, C_WIDTH), BF16),
        compiler_params=_cparams(("parallel",)),
        name="na_attention",
    )(p, p, p, p, p, p, p, p, p, bias)


def _ctx_attn_kernel(q_ref, k_ref, v_ref, rows_hbm, o_ref):
    del rows_hbm
    scale = HEAD_DIM ** -0.5
    for h in range(C_HEADS):
        cs = slice(h * HEAD_DIM, (h + 1) * HEAD_DIM)
        s = _dot_nt(q_ref[:, cs], k_ref[:, cs]) * scale
        pexp = jnp.exp(s - s.max(axis=-1, keepdims=True))
        o = _dot(pexp.astype(BF16), v_ref[:, cs]) / pexp.sum(axis=-1, keepdims=True)
        o_ref[:, cs] = o.astype(o_ref.dtype)


def na_ctx_attention(p, n_ctx, out_rows):
    qblk = (A_COLS + B_COLS) // C_WIDTH
    return pl.pallas_call(
        _ctx_attn_kernel,
        grid=(1,),
        in_specs=[pl.BlockSpec((n_ctx, C_WIDTH), lambda i, s=s: (0, qblk + s)) for s in range(3)]
        + [pl.BlockSpec(memory_space=pl.ANY)],
        out_specs=pl.BlockSpec((n_ctx, C_WIDTH), lambda i: (0, 0)),
        out_shape=jax.ShapeDtypeStruct(out_rows.shape, BF16),
        input_output_aliases={3: 0},
        compiler_params=_cparams(("arbitrary",)),
        name="na_ctx_attention",
    )(p, p, p, out_rows)


MOE_TM = 256
MOE_COMBINE_TR = 128


DEINT_BLOCK = 256


def _deint_kernel(w_ref, p_ref, o_ref):
    f2 = w_ref.shape[1]
    hb = DEINT_BLOCK // 2
    perm = p_ref[...]
    for j in range(f2 // DEINT_BLOCK):
        part = _dot(w_ref[:, j * DEINT_BLOCK:(j + 1) * DEINT_BLOCK].astype(BF16), perm).astype(o_ref.dtype)
        o_ref[:, j * hb:(j + 1) * hb] = part[:, :hb]
        o_ref[:, f2 // 2 + j * hb:f2 // 2 + (j + 1) * hb] = part[:, hb:]


def deinterleave_gate_up(w_gu, tk=1024):
    g, d, f2 = w_gu.shape
    src = np.concatenate([np.arange(0, DEINT_BLOCK, 2), np.arange(1, DEINT_BLOCK, 2)])
    perm = jnp.asarray((np.arange(DEINT_BLOCK)[:, None] == src[None, :]).astype(np.float32), dtype=BF16)
    return pl.pallas_call(
        _deint_kernel,
        grid=(g, d // tk),
        in_specs=[pl.BlockSpec((None, tk, f2), lambda e, i: (e, i, 0)),
                  pl.BlockSpec((DEINT_BLOCK, DEINT_BLOCK), lambda e, i: (0, 0))],
        out_specs=pl.BlockSpec((None, tk, f2), lambda e, i: (e, i, 0)),
        out_shape=jax.ShapeDtypeStruct((g, d, f2), BF16),
        compiler_params=_cparams(("parallel", "parallel")),
        name="deinterleave_gate_up",
    )(w_gu, perm)


def _row_copy(src_hbm, src_row, dst, dst_row, sem):
    return pltpu.make_async_copy(src_hbm.at[pl.ds(src_row, 1), :], dst.at[pl.ds(dst_row, 1), :], sem)


def _moe_expert_kernel(te_ref, src_ref, next_ref, h_hbm, wgu_ref, bgu_ref, wd_ref, bd_ref, y_ref, buf, sem):
    t = pl.program_id(0)
    slot = t % 2
    tm = src_ref.shape[1]

    def gather(idx_ref, s):
        def body(r2, carry):
            for pr in range(2):
                r = 2 * r2 + pr
                _row_copy(h_hbm, idx_ref[0, r], buf.at[s], r, sem.at[s]).start(priority=pr)
            return carry
        lax.fori_loop(0, tm // 2, body, 0, unroll=4)

    @pl.when(t == 0)
    def _():
        gather(src_ref, 0)

    @pl.when(t + 1 < pl.num_programs(0))
    def _():
        gather(next_ref, 1 - slot)

    pltpu.make_async_copy(h_hbm.at[pl.ds(0, tm), :], buf.at[slot], sem.at[slot]).wait()

    lo, hi = _unpack_bf16_pair(buf[slot])
    half = lo.shape[1]
    gu = (_dot(lo.astype(BF16), wgu_ref[:half, :]) + _dot(hi.astype(BF16), wgu_ref[half:, :])
          + bgu_ref[...])
    f = gu.shape[1] // 2
    g = jnp.minimum(gu[:, :f], SWIGLU_LIMIT)
    u = jnp.clip(gu[:, f:], -SWIGLU_LIMIT, SWIGLU_LIMIT)
    act = (u + 1.0) * g * _sigmoid(g * SWIGLU_ALPHA)
    y = _dot(act.astype(BF16), wd_ref[...]) + bd_ref[...]
    y_ref[...] = _pack_bf16_pair(y[:, :half], y[:, half:])


def moe_experts(h_packed, src_rows, tile_expert, wgu, bgu, wd, bd):
    tm = MOE_TM
    n_tiles = tile_expert.shape[0]
    half = h_packed.shape[1]
    d, f2 = wgu.shape[1], wgu.shape[2]
    src3 = src_rows.reshape(n_tiles, 1, tm)
    grid_spec = pltpu.PrefetchScalarGridSpec(
        num_scalar_prefetch=1,
        grid=(n_tiles,),
        in_specs=[pl.BlockSpec((None, 1, tm), lambda t, te: (t, 0, 0), memory_space=pltpu.SMEM),
                  pl.BlockSpec((None, 1, tm), lambda t, te: (jnp.minimum(t + 1, n_tiles - 1), 0, 0),
                               memory_space=pltpu.SMEM),
                  pl.BlockSpec(memory_space=pl.ANY),
                  pl.BlockSpec((None, d, f2), lambda t, te: (te[t], 0, 0)),
                  pl.BlockSpec((None, 1, f2), lambda t, te: (te[t], 0, 0)),
                  pl.BlockSpec((None, f2 // 2, d), lambda t, te: (te[t], 0, 0)),
                  pl.BlockSpec((None, 1, d), lambda t, te: (te[t], 0, 0))],
        out_specs=pl.BlockSpec((tm, half), lambda t, te: (t, 0)),
        scratch_shapes=[pltpu.VMEM((2, tm, half), h_packed.dtype), pltpu.SemaphoreType.DMA((2,))],
    )
    return pl.pallas_call(
        _moe_expert_kernel,
        grid_spec=grid_spec,
        out_shape=jax.ShapeDtypeStruct((n_tiles * tm, half), h_packed.dtype),
        compiler_params=_cparams(("arbitrary",)),
        name="moe_experts",
    )(tile_expert, src3, src3, h_packed, wgu, bgu, wd, bd)


def _moe_combine_kernel(dcur_ref, dnext_ref, wts_ref, x_ref, mod_ref, ys_hbm, o_ref, buf, sem, *, gi, n_ctx, tr):
    t = pl.program_id(0)
    slot = t % 2

    def gather(d_ref, s):
        def body(i, carry):
            for k in range(TOP_K):
                _row_copy(ys_hbm, d_ref[k, i], buf.at[s, k], i, sem.at[s]).start(priority=k % 2)
            return carry
        lax.fori_loop(0, tr, body, 0, unroll=4)

    @pl.when(t == 0)
    def _():
        gather(dcur_ref, 0)

    @pl.when(t + 1 < pl.num_programs(0))
    def _():
        gather(dnext_ref, 1 - slot)

    for k in range(TOP_K):
        pltpu.make_async_copy(ys_hbm.at[pl.ds(0, tr), :], buf.at[slot, k], sem.at[slot]).wait()

    half = buf.shape[-1]
    acc_lo = jnp.zeros((tr, half), F32)
    acc_hi = jnp.zeros((tr, half), F32)
    for k in range(TOP_K):
        lo, hi = _unpack_bf16_pair(buf[slot, k])
        w = wts_ref[:, k:k + 1]
        acc_lo = acc_lo + w * lo
        acc_hi = acc_hi + w * hi
    is_ctx = _row_is_ctx(t * tr, tr, n_ctx)
    gate = jnp.where(is_ctx, mod_ref[1, gi:gi + 1, :], mod_ref[0, gi:gi + 1, :])
    o_ref[:, :half] = x_ref[:, :half] + gate[:, :half] * acc_lo
    o_ref[:, half:] = x_ref[:, half:] + gate[:, half:] * acc_hi


def moe_combine(ys, dest_t, wts, x, mods, gi, n_ctx):
    tr = MOE_COMBINE_TR
    n, d = x.shape
    nt = n // tr
    return pl.pallas_call(
        functools.partial(_moe_combine_kernel, gi=gi, n_ctx=n_ctx, tr=tr),
        grid=(nt,),
        in_specs=[pl.BlockSpec((TOP_K, tr), lambda t: (0, t), memory_space=pltpu.SMEM),
                  pl.BlockSpec((TOP_K, tr), lambda t: (0, jnp.minimum(t + 1, nt - 1)), memory_space=pltpu.SMEM),
                  pl.BlockSpec((tr, TOP_K), lambda t: (t, 0)),
                  pl.BlockSpec((tr, d), lambda t: (t, 0)),
                  pl.BlockSpec((2, N_MOD, d), lambda t: (0, 0, 0)),
                  pl.BlockSpec(memory_space=pl.ANY)],
        out_specs=pl.BlockSpec((tr, d), lambda t: (t, 0)),
        out_shape=jax.ShapeDtypeStruct((n, d), F32),
        scratch_shapes=[pltpu.VMEM((2, TOP_K, tr, ys.shape[1]), ys.dtype), pltpu.SemaphoreType.DMA((2,))],
        compiler_params=_cparams(("arbitrary",)),
        name="moe_combine",
    )(dest_t, dest_t, wts, x, mods, ys)


def moe_routing_tables(idx, rank, counts, n_tiles, expert_base):
    tm = MOE_TM
    n = idx.shape[0]
    cnt = counts[0].astype(jnp.int32)
    padded = (cnt + tm - 1) // tm * tm
    ends = jnp.cumsum(padded)
    offs = ends - padded
    onehot = idx[:, :, None] == jnp.arange(N_EXPERTS, dtype=jnp.int32)
    dest = jnp.sum(jnp.where(onehot, offs, 0), axis=-1) + rank
    token = jnp.broadcast_to(jnp.arange(n, dtype=jnp.int32)[:, None], dest.shape)
    src_rows = jnp.zeros((n_tiles * tm,), jnp.int32).at[dest.reshape(-1)].set(token.reshape(-1))
    tile_start = jnp.arange(n_tiles, dtype=jnp.int32) * tm
    tile_expert = jnp.minimum(jnp.sum(tile_start[:, None] >= ends[None, :], axis=1), N_EXPERTS - 1)
    return dest.T, src_rows, tile_expert.astype(jnp.int32) + expert_base


def moe_ffn(x, norm_g, mods, router_w, router_b, wgu, bgu, wd, bd, expert_base, n_ctx):
    n = x.shape[0]
    n_tiles = -(-(TOP_K * n + N_EXPERTS * (MOE_TM - 1)) // MOE_TM)
    h_packed, idx, wts, rank, counts = norm_router(x, norm_g, mods, 3, router_w, router_b, n_ctx)
    dest_t, src_rows, tile_expert = moe_routing_tables(idx, rank, counts, n_tiles, expert_base)
    ys = moe_experts(h_packed, src_rows, tile_expert, wgu, bgu, wd, bd)
    return moe_combine(ys, dest_t, wts, x, mods, 5, n_ctx)


def _mm_tile(m):
    for tm in (1280, 1024, 640, 512, 256):
        if m % tm == 0:
            return tm
    raise ValueError(f"unsupported row count {m}")


def kernel(x, c, ctx, c_ctx, ada_w, ada_b, norm_mix, norm_ffn, w_in, w_out, hgrn_lb_logits, hgrn_norm,
           diff_lambda, diff_subln, na_rpb, router_w, router_b, expert_w_gu, expert_b_gu, expert_w_down,
           expert_b_down, final_norm_g):
    assert x.shape[0] == 1 and c.shape[0] == 1
    depth = ada_w.shape[0]
    n_ctx = ctx.shape[1]
    s_len = x.shape[1]
    n = n_ctx + s_len
    rows = s_len // GRID_W
    tm = _mm_tile(n)

    hall = jnp.concatenate([ctx[0], x[0]], axis=0)
    c8 = jnp.zeros((8, D_MODEL), F32).at[0].set(c[0]).at[1].set(c_ctx)
    mods_all = ada_mod(c8, ada_w, ada_b)[:, :2].reshape(depth, 2, N_MOD, D_MODEL)

    p_lb = jax.nn.softmax(hgrn_lb_logits.astype(F32), axis=0)
    lower_bounds = jnp.clip(jnp.cumsum(p_lb, axis=0) - p_lb[0], 0.0, 1.0)
    cos, sin = rope_tables(n, n_ctx)

    wgu_all = deinterleave_gate_up(expert_w_gu.reshape(depth * N_EXPERTS, D_MODEL, 2 * D_EXPERT))
    bgu_all = jnp.concatenate([expert_b_gu[..., 0::2], expert_b_gu[..., 1::2]], axis=-1)
    bgu_all = bgu_all.reshape(depth * N_EXPERTS, 1, 2 * D_EXPERT)
    wd_all = expert_w_down.reshape(depth * N_EXPERTS, D_EXPERT, D_MODEL).astype(BF16)
    bd_all = expert_b_down.reshape(depth * N_EXPERTS, 1, D_MODEL)
    tk = n // 13 if n % (13 * 256) == 0 else 256

    for layer in range(depth):
        lambda_init = 0.8 - 0.6 * math.exp(-0.3 * layer)
        mods = mods_all[layer]
        w_in_b = w_in[layer].astype(BF16)
        w_out_b = w_out[layer].astype(BF16)

        h = norm_mod(hall, norm_mix[layer], mods, 0, n_ctx)
        p = matmul(h, w_in_b, tm, 1024, BF16)

        ya = hgrn_mixer(p, lower_bounds[layer], hgrn_norm[layer], n_ctx)
        qr, kt = rope_qk(p, cos, sin)
        yb = diff_attention(qr, kt, p, diff_lambda[layer], diff_subln[layer], lambda_init,
                            n_ctx, s_len, n, 256, tk)
        yb = diff_attention(qr, kt, p, diff_lambda[layer], diff_subln[layer], lambda_init,
                            0, n_ctx, n_ctx, n_ctx, n_ctx, out_rows=yb)
        yc = na_attention(p, na_bias_tables(na_rpb[layer], rows), n_ctx)
        yc = na_ctx_attention(p, n_ctx, yc)
        hall = matmul_residual([ya, yb, yc], w_out_b, hall, mods, 2, n_ctx, tm, 512)
        hall = moe_ffn(hall, norm_ffn[layer], mods, router_w[layer], router_b[layer], wgu_all, bgu_all,
                       wd_all, bd_all, layer * N_EXPERTS, n_ctx)

    return final_norm(hall, final_norm_g, n_ctx)[None]
```
